```python
import jax, jax.numpy as jnp
from jax import lax
import numpy as np

D_MODEL = 2048
BATCH = 1
SEQ = 8192
DEPTH = 1
DEC_BATCH = 16
DEC_SEQ = 16
PAST_LEN = 2048

CHUNK = 64
N_PAST_CHUNKS = 8
BAND_CHUNKS = N_PAST_CHUNKS + 1
BAND_PAST = N_PAST_CHUNKS * CHUNK
ATTN_WIDTH = D_MODEL // 2
HEAD_DIM_A = 64
N_HEADS_A = ATTN_WIDTH // HEAD_DIM_A
REL_CLIP = 128
N_REL = 2 * REL_CLIP + 1
RWKV_WIDTH = D_MODEL - ATTN_WIDTH
HEAD_DIM_B = 64
N_HEADS_B = RWKV_WIDTH // HEAD_DIM_B
DECAY_RANK = 64
AAA_RANK = 64
GATE_RANK = 128
SHIFT_W = 3 * RWKV_WIDTH + DECAY_RANK + AAA_RANK + GATE_RANK
IN_W = 3 * ATTN_WIDTH + SHIFT_W
D_FF = -(-8 * D_MODEL // (3 * 256)) * 256
RMS_EPS = 1e-6
GN_EPS = 64e-5

kernel_name = 'hybrid_chunk_attn_rwkv7_stream_step'

F32 = jnp.float32


def rms_norm(x, g):
    xf = x.astype(F32)
    y = xf * lax.rsqrt(jnp.mean(xf * xf, axis=-1, keepdims=True) + RMS_EPS)
    return (y * g.astype(F32)).astype(x.dtype)


def rel_bias(table, dist):
    return table[:, jnp.clip(dist, -REL_CLIP, REL_CLIP) + REL_CLIP].astype(F32)


def band_attention_prompt(q, k, v, rel_table):
    B, T, H, D = q.shape
    nc = T // CHUNK
    qc = q.reshape(B, nc, CHUNK, H, D)
    pad = ((0, 0), (N_PAST_CHUNKS, 0), (0, 0), (0, 0), (0, 0))
    kp = jnp.pad(k.reshape(B, nc, CHUNK, H, D), pad)
    vp = jnp.pad(v.reshape(B, nc, CHUNK, H, D), pad)
    band_idx = jnp.arange(nc)[:, None] + jnp.arange(BAND_CHUNKS)[None, :]
    kb = kp[:, band_idx].reshape(B, nc, BAND_CHUNKS * CHUNK, H, D)
    vb = vp[:, band_idx].reshape(B, nc, BAND_CHUNKS * CHUNK, H, D)
    dist = BAND_PAST + jnp.arange(CHUNK)[:, None] - jnp.arange(BAND_CHUNKS * CHUNK)[None, :]
    bias = rel_bias(rel_table, dist)
    valid = jnp.repeat(band_idx >= N_PAST_CHUNKS, CHUNK, axis=1)
    s = jnp.einsum('bnqhd,bnkhd->bnhqk', qc, kb).astype(F32) * (HEAD_DIM_A ** -0.5) + bias[None, None]
    s = jnp.where(valid[None, :, None, None, :], s, -jnp.inf)
    p = jax.nn.softmax(s, axis=-1).astype(v.dtype)
    o = jnp.einsum('bnhqk,bnkhd->bnqhd', p, vb)
    return o.reshape(B, T, H, D)


def band_attention_cached(q, k, v, k_cache, v_cache, rel_table):
    P = k_cache.shape[1]
    S = q.shape[1]
    kk = jnp.concatenate([k_cache.astype(k.dtype), k], axis=1)
    vv = jnp.concatenate([v_cache.astype(v.dtype), v], axis=1)
    q_rel = jnp.arange(S) + P
    k_rel = jnp.arange(P + S)
    bias = rel_bias(rel_table, q_rel[:, None] - k_rel[None, :])
    s = jnp.einsum('bqhd,bkhd->bhqk', q, kk).astype(F32) * (HEAD_DIM_A ** -0.5) + bias[None]
    p = jax.nn.softmax(s, axis=-1).astype(v.dtype)
    return jnp.einsum('bhqk,bkhd->bqhd', p, vv)


def rwkv7_time_mix(z, shift0, wkv0, mu, w0, w_w2, a0, a_w2, g_w2, k_k, k_a, r_k, lnx_w, lnx_b):
    B, T, _ = z.shape
    H, N = N_HEADS_B, HEAD_DIM_B
    z_prev = jnp.concatenate([shift0.astype(z.dtype), z[:, :-1]], axis=1)
    zs = (z + (z_prev - z) * mu).astype(F32)
    cuts = [RWKV_WIDTH, 2 * RWKV_WIDTH, 3 * RWKV_WIDTH, 3 * RWKV_WIDTH + DECAY_RANK,
            3 * RWKV_WIDTH + DECAY_RANK + AAA_RANK]
    r, k, v, zw, za, zg = jnp.split(zs, cuts, axis=-1)
    wlog = -jax.nn.softplus(-(w0.astype(F32) + jnp.tanh(zw) @ w_w2.astype(F32))) - 0.5
    decay = jnp.exp(-jnp.exp(wlog))
    a = jax.nn.sigmoid(a0.astype(F32) + za @ a_w2.astype(F32))
    g = jax.nn.sigmoid(zg) @ g_w2.astype(F32)
    kk = (k * k_k.astype(F32)).reshape(B, T, H, N)
    kk = kk / jnp.maximum(jnp.linalg.norm(kk, axis=-1, keepdims=True), 1e-12)
    k = k * (1.0 + (a - 1.0) * k_a.astype(F32))

    def heads_tm(t):
        return jnp.moveaxis(t.reshape(B, T, H, N), 1, 0)

    xs = (heads_tm(r), heads_tm(decay), heads_tm(k), heads_tm(v), jnp.moveaxis(kk, 1, 0), heads_tm(a))

    def step(S, inp):
        r_t, w_t, k_t, v_t, kk_t, a_t = inp
        sa = jnp.einsum('bhvk,bhk->bhv', S, -kk_t)
        S = (S * w_t[:, :, None, :] + sa[..., None] * (kk_t * a_t)[:, :, None, :]
             + v_t[..., None] * k_t[:, :, None, :])
        return S, jnp.einsum('bhvk,bhk->bhv', S, r_t)

    S_T, ys = lax.scan(step, wkv0.astype(F32), xs)
    y = jnp.moveaxis(ys, 0, 1)
    mean = jnp.mean(y, axis=-1, keepdims=True)
    var = jnp.mean(jnp.square(y - mean), axis=-1, keepdims=True)
    yn = ((y - mean) * lax.rsqrt(var + GN_EPS)).reshape(B, T, RWKV_WIDTH) * lnx_w.astype(F32) + lnx_b.astype(F32)
    bonus = (jnp.sum((r * k * r_k.astype(F32)).reshape(B, T, H, N), axis=-1, keepdims=True)
             * v.reshape(B, T, H, N)).reshape(B, T, RWKV_WIDTH)
    out = ((yn + bonus) * g).astype(z.dtype)
    return out, S_T.astype(wkv0.dtype), z[:, -1:]


def trunk_layer(x, attn_fn, shift0, wkv0, lp):
    B, T, _ = x.shape
    n = rms_norm(x, lp['ln1'])
    z = n @ lp['w_in']
    q, k, v = jnp.split(z[..., :3 * ATTN_WIDTH], 3, axis=-1)
    q = q.reshape(B, T, N_HEADS_A, HEAD_DIM_A)
    k = k.reshape(B, T, N_HEADS_A, HEAD_DIM_A)
    v = v.reshape(B, T, N_HEADS_A, HEAD_DIM_A)
    attn_out = attn_fn(q, k, v).reshape(B, T, ATTN_WIDTH)
    rwkv_out, wkv_new, shift_new = rwkv7_time_mix(
        z[..., 3 * ATTN_WIDTH:], shift0, wkv0, lp['mu'], lp['w0'], lp['w_w2'], lp['a0'], lp['a_w2'],
        lp['g_w2'], lp['k_k'], lp['k_a'], lp['r_k'], lp['lnx_w'], lp['lnx_b'])
    h = x + jnp.concatenate([attn_out, rwkv_out.astype(attn_out.dtype)], axis=-1) @ lp['w_o']
    n2 = rms_norm(h, lp['ln2'])
    y = h + (jax.nn.silu(n2 @ lp['w_gate']) * (n2 @ lp['w_up'])) @ lp['w_down']
    return y, k, v, wkv_new, shift_new


def setup_inputs(seed: int = 0) -> dict:
    key = jax.random.key(seed)
    ks = jax.random.split(key, 32)
    kv_rows = min(BAND_PAST, PAST_LEN)
    nrm = jax.random.normal
    w0_base = jnp.linspace(-6.5, -1.0, RWKV_WIDTH, dtype=F32)
    return {
        'x_prompt': nrm(ks[0], (BATCH, SEQ, D_MODEL), F32),
        'x_sample': nrm(ks[1], (DEC_BATCH, DEC_SEQ, D_MODEL), F32),
        'cache_attn_k': nrm(ks[2], (DEPTH, DEC_BATCH, kv_rows, N_HEADS_A, HEAD_DIM_A), F32),
        'cache_attn_v': nrm(ks[3], (DEPTH, DEC_BATCH, kv_rows, N_HEADS_A, HEAD_DIM_A), F32),
        'state_rwkv_wkv': 0.1 * nrm(ks[4], (DEPTH, DEC_BATCH, N_HEADS_B, HEAD_DIM_B, HEAD_DIM_B), F32),
        'state_rwkv_shift': nrm(ks[5], (DEPTH, DEC_BATCH, 1, SHIFT_W), F32),
        'ln1': 1.0 + 0.01 * nrm(ks[6], (DEPTH, D_MODEL), F32),
        'w_in': nrm(ks[7], (DEPTH, D_MODEL, IN_W), F32) * D_MODEL ** -0.5,
        'rel_table': 0.2 * nrm(ks[8], (DEPTH, N_HEADS_A, N_REL), F32),
        'mu': jax.random.uniform(ks[9], (DEPTH, SHIFT_W), F32),
        'w0': w0_base[None] + 0.1 * nrm(ks[10], (DEPTH, RWKV_WIDTH), F32),
        'w_w2': 0.1 * nrm(ks[11], (DEPTH, DECAY_RANK, RWKV_WIDTH), F32) * DECAY_RANK ** -0.5,
        'a0': 0.1 * nrm(ks[12], (DEPTH, RWKV_WIDTH), F32),
        'a_w2': nrm(ks[13], (DEPTH, AAA_RANK, RWKV_WIDTH), F32) * AAA_RANK ** -0.5,
        'g_w2': nrm(ks[14], (DEPTH, GATE_RANK, RWKV_WIDTH), F32) * GATE_RANK ** -0.5,
        'k_k': 0.85 + 0.05 * nrm(ks[15], (DEPTH, RWKV_WIDTH), F32),
        'k_a': 1.0 + 0.05 * nrm(ks[16], (DEPTH, RWKV_WIDTH), F32),
        'r_k': 0.1 * nrm(ks[17], (DEPTH, RWKV_WIDTH), F32),
        'lnx_w': 1.0 + 0.01 * nrm(ks[18], (DEPTH, RWKV_WIDTH), F32),
        'lnx_b': 0.01 * nrm(ks[19], (DEPTH, RWKV_WIDTH), F32),
        'w_o': nrm(ks[20], (DEPTH, D_MODEL, D_MODEL), F32) * D_MODEL ** -0.5,
        'ln2': 1.0 + 0.01 * nrm(ks[21], (DEPTH, D_MODEL), F32),
        'w_gate': nrm(ks[22], (DEPTH, D_MODEL, D_FF), F32) * D_MODEL ** -0.5,
        'w_up': nrm(ks[23], (DEPTH, D_MODEL, D_FF), F32) * D_MODEL ** -0.5,
        'w_down': nrm(ks[24], (DEPTH, D_FF, D_MODEL), F32) * D_FF ** -0.5,
        'final_norm': 1.0 + 0.01 * nrm(ks[25], (D_MODEL,), F32),
    }


def reference(x_prompt, x_sample, cache_attn_k, cache_attn_v, state_rwkv_wkv, state_rwkv_shift,
              ln1, w_in, rel_table, mu, w0, w_w2, a0, a_w2, g_w2, k_k, k_a, r_k, lnx_w, lnx_b,
              w_o, ln2, w_gate, w_up, w_down, final_norm):
    B, T, _ = x_prompt.shape
    n_keep = min(BAND_PAST, T)
    hp, hs = x_prompt, x_sample
    kp_l, vp_l, wp_l, sp_l, ks_l, vs_l, ws_l, ss_l = [], [], [], [], [], [], [], []
    for l in range(DEPTH):
        lp = dict(ln1=ln1[l], w_in=w_in[l], mu=mu[l], w0=w0[l], w_w2=w_w2[l], a0=a0[l], a_w2=a_w2[l],
                  g_w2=g_w2[l], k_k=k_k[l], k_a=k_a[l], r_k=r_k[l], lnx_w=lnx_w[l], lnx_b=lnx_b[l],
                  w_o=w_o[l], ln2=ln2[l], w_gate=w_gate[l], w_up=w_up[l], w_down=w_down[l])
        table = rel_table[l]
        hp, k_p, v_p, wkv_p, sh_p = trunk_layer(
            hp, lambda q, k, v: band_attention_prompt(q, k, v, table),
            jnp.zeros((B, 1, SHIFT_W), hp.dtype), jnp.zeros((B, N_HEADS_B, HEAD_DIM_B, HEAD_DIM_B), F32), lp)
        kp_l.append(k_p[:, -n_keep:]); vp_l.append(v_p[:, -n_keep:]); wp_l.append(wkv_p); sp_l.append(sh_p)
        ck, cv = cache_attn_k[l], cache_attn_v[l]
        hs, k_s, v_s, wkv_s, sh_s = trunk_layer(
            hs, lambda q, k, v: band_attention_cached(q, k, v, ck, cv, table),
            state_rwkv_shift[l], state_rwkv_wkv[l], lp)
        ks_l.append(k_s); vs_l.append(v_s); ws_l.append(wkv_s); ss_l.append(sh_s)
    y_prompt = rms_norm(hp, final_norm)
    y_sample = rms_norm(hs, final_norm)
    return (y_prompt, y_sample,
            jnp.stack(kp_l), jnp.stack(vp_l), jnp.stack(wp_l), jnp.stack(sp_l),
            jnp.stack(ks_l), jnp.stack(vs_l), jnp.stack(ws_l), jnp.stack(ss_l))
```

```python
import functools
import math

import jax
import jax.numpy as jnp
from jax import lax
from jax.experimental import pallas as pl
from jax.experimental.pallas import tpu as pltpu

F32 = jnp.float32
BF16 = jnp.bfloat16

D_MODEL = 2048
CHUNK = 64
N_PAST_CHUNKS = 8
BAND_PAST = N_PAST_CHUNKS * CHUNK
ATTN_WIDTH = 1024
HEAD_DIM = 64
N_HEADS = 16
REL_CLIP = 128
N_REL = 2 * REL_CLIP + 1
RWKV_WIDTH = 1024
DECAY_RANK = 64
AAA_RANK = 64
GATE_RANK = 128
SHIFT_W = 3 * RWKV_WIDTH + DECAY_RANK + AAA_RANK + GATE_RANK
D_FF = 5632
RMS_EPS = 1e-6
GN_EPS = 64e-5

V7X_LANES = 128
V7X_VMEM_BYTES = 64 * 1024 * 1024
V7X_SCOPED_VMEM_CAP_BYTES = 60000 * 1024

PAIR_W = 2 * HEAD_DIM
N_PAIRS = N_HEADS // 2
CHUNK_T = 64
KEY_WIN = (N_PAST_CHUNKS + 2) * CHUNK
NEG_BIG = -1e30
ATTN_QB = 512

assert PAIR_W == V7X_LANES


def _vmem_limit(pipelined_bytes, scratch_bytes, temp_bytes):
    need = 2 * pipelined_bytes + scratch_bytes + temp_bytes
    return int(min(max(need, 16 * 1024 * 1024), V7X_SCOPED_VMEM_CAP_BYTES))


def _nbytes(shape, dtype):
    return math.prod(shape) * jnp.dtype(dtype).itemsize


def _params(n_grid, vmem):
    return pltpu.CompilerParams(dimension_semantics=("arbitrary",) * n_grid, vmem_limit_bytes=vmem)


def _rms(x, g):
    return x * lax.rsqrt(jnp.mean(x * x, axis=-1, keepdims=True) + RMS_EPS) * g


def _inproj_kernel(x_ref, g_ref, w_ref, o_ref, n_ref):
    @pl.when(pl.program_id(1) == 0)
    def _():
        n_ref[...] = _rms(x_ref[...], g_ref[...]).astype(BF16)

    o_ref[...] = jnp.dot(n_ref[...], w_ref[...], preferred_element_type=F32)


def _inproj(x, g, w, *, tm, tn, name):
    t, d = x.shape
    n = w.shape[1]
    assert t % tm == 0 and n % tn == 0
    blocks = _nbytes((tm, d), F32) + _nbytes((d, tn), BF16) + _nbytes((tm, tn), F32)
    return pl.pallas_call(
        _inproj_kernel,
        out_shape=jax.ShapeDtypeStruct((t, n), F32),
        grid=(t // tm, n // tn),
        in_specs=[pl.BlockSpec((tm, d), lambda i, j: (i, 0)),
                  pl.BlockSpec((1, d), lambda i, j: (0, 0)),
                  pl.BlockSpec((d, tn), lambda i, j: (0, j))],
        out_specs=pl.BlockSpec((tm, tn), lambda i, j: (i, j)),
        scratch_shapes=[pltpu.VMEM((tm, d), BF16)],
        compiler_params=_params(2, _vmem_limit(blocks, _nbytes((tm, d), BF16), 2 * _nbytes((tm, d), F32))),
        name=name,
    )(x, g, w)


BIAS_VAR_COL0 = KEY_WIN - 2 * V7X_LANES


def _bias_kernel(tab_ref, o_ref, *, mq, off, col_lo, col_hi):
    nvar = KEY_WIN - BIAS_VAR_COL0
    q = lax.broadcasted_iota(jnp.int32, (mq, nvar), 0)
    c = lax.broadcasted_iota(jnp.int32, (mq, nvar), 1) + BIAS_VAR_COL0
    idx = jnp.clip(off + q - c, -REL_CLIP, REL_CLIP) + REL_CLIP
    valid_var = (c >= col_lo) & (c < col_hi)
    c_far = lax.broadcasted_iota(jnp.int32, (mq, BIAS_VAR_COL0), 1)
    valid_far = (c_far >= col_lo) & (c_far < col_hi)

    def body(j, accs):
        hit = idx == j
        return tuple(jnp.where(hit, tab_ref[h, j], accs[h]) for h in range(N_HEADS))

    init = tuple(jnp.zeros((mq, nvar), F32) for _ in range(N_HEADS))
    accs = lax.fori_loop(0, N_REL, body, init)
    for h in range(N_HEADS):
        far = jnp.full((mq, BIAS_VAR_COL0), tab_ref[h, N_REL - 1], F32)
        o_ref[h, :, 0:BIAS_VAR_COL0] = jnp.where(valid_far, far, NEG_BIG)
        o_ref[h, :, BIAS_VAR_COL0:KEY_WIN] = jnp.where(valid_var, accs[h], NEG_BIG)


def _bias_tiles(table, *, mq, off, col_lo, col_hi, name):
    assert off - (BIAS_VAR_COL0 - 1) >= REL_CLIP
    return pl.pallas_call(
        functools.partial(_bias_kernel, mq=mq, off=off, col_lo=col_lo, col_hi=col_hi),
        out_shape=jax.ShapeDtypeStruct((N_HEADS, mq, KEY_WIN), F32),
        in_specs=[pl.BlockSpec(memory_space=pltpu.SMEM)],
        out_specs=pl.BlockSpec(memory_space=pltpu.VMEM),
        name=name,
    )(table)


def _lane_lo(rows):
    return lax.broadcasted_iota(jnp.int32, (rows, PAIR_W), 1) < HEAD_DIM


def _attend_pair(q, kw, vw, bias2, min_col):
    mq = q.shape[0]
    lo = _lane_lo(mq)
    qs = q * (HEAD_DIM ** -0.5)
    q2 = jnp.concatenate([jnp.where(lo, qs, 0.0), jnp.where(lo, 0.0, qs)], axis=0).astype(BF16)
    s = lax.dot_general(q2, kw, (((1,), (1,)), ((), ())), preferred_element_type=F32) + bias2
    if min_col is not None:
        col = lax.broadcasted_iota(jnp.int32, (1, KEY_WIN), 1)
        s = jnp.where(col >= min_col, s, NEG_BIG)
    m = jnp.max(s, axis=-1, keepdims=True)
    p = jnp.exp(s - m)
    l = jnp.sum(p, axis=-1, keepdims=True)
    pv = jnp.dot(p.astype(BF16), vw, preferred_element_type=F32) * (1.0 / l)
    return jnp.where(lo, pv[:mq], pv[mq:])


def _attn_prompt_kernel(q_ref, kp_ref, kc_ref, vp_ref, vc_ref, bias_ref, o_ref, kbuf, vbuf):
    i = pl.program_id(0)
    lead = jnp.zeros((CHUNK, ATTN_WIDTH), BF16)
    kbuf[0:CHUNK, :] = lead
    vbuf[0:CHUNK, :] = lead
    kbuf[CHUNK:CHUNK + ATTN_QB, :] = kp_ref[...].astype(BF16)
    vbuf[CHUNK:CHUNK + ATTN_QB, :] = vp_ref[...].astype(BF16)
    kbuf[CHUNK + ATTN_QB:CHUNK + 2 * ATTN_QB, :] = kc_ref[...].astype(BF16)
    vbuf[CHUNK + ATTN_QB:CHUNK + 2 * ATTN_QB, :] = vc_ref[...].astype(BF16)

    def chunk_body(j, carry):
        r0 = pl.multiple_of(j * CHUNK, CHUNK)
        min_col = (N_PAST_CHUNKS + 1 - (i * (ATTN_QB // CHUNK) + j)) * CHUNK
        for p in range(N_PAIRS):
            sl = slice(p * PAIR_W, (p + 1) * PAIR_W)
            bias2 = jnp.concatenate([bias_ref[2 * p], bias_ref[2 * p + 1]], axis=0)
            o = _attend_pair(q_ref[pl.ds(r0, CHUNK), sl], kbuf[pl.ds(r0, KEY_WIN), sl],
                             vbuf[pl.ds(r0, KEY_WIN), sl], bias2, min_col)
            o_ref[pl.ds(r0, CHUNK), sl] = o.astype(BF16)
        return carry

    lax.fori_loop(0, ATTN_QB // CHUNK, chunk_body, 0)


def _attn_prompt(z_attn, bias):
    t = z_attn.shape[0]
    assert t % ATTN_QB == 0
    blk = (ATTN_QB, ATTN_WIDTH)
    buf_rows = CHUNK + 2 * ATTN_QB
    blocks = 5 * _nbytes(blk, F32) + _nbytes(bias.shape, F32) + _nbytes(blk, BF16)
    scratch = 2 * _nbytes((buf_rows, ATTN_WIDTH), BF16)
    return pl.pallas_call(
        _attn_prompt_kernel,
        out_shape=jax.ShapeDtypeStruct((t, ATTN_WIDTH), BF16),
        grid=(t // ATTN_QB,),
        in_specs=[pl.BlockSpec(blk, lambda i: (i, 0)),
                  pl.BlockSpec(blk, lambda i: (jnp.maximum(i - 1, 0), 1)),
                  pl.BlockSpec(blk, lambda i: (i, 1)),
                  pl.BlockSpec(blk, lambda i: (jnp.maximum(i - 1, 0), 2)),
                  pl.BlockSpec(blk, lambda i: (i, 2)),
                  pl.BlockSpec(bias.shape, lambda i: (0, 0, 0))],
        out_specs=pl.BlockSpec(blk, lambda i: (i, 0)),
        scratch_shapes=[pltpu.VMEM((buf_rows, ATTN_WIDTH), BF16), pltpu.VMEM((buf_rows, ATTN_WIDTH), BF16)],
        compiler_params=_params(1, _vmem_limit(blocks, scratch, 8 * _nbytes((2 * CHUNK, KEY_WIN), F32))),
        name="attn_prompt",
    )(z_attn, z_attn, z_attn, z_attn, z_attn, bias)


def _attn_sample_kernel(q_ref, kn_ref, vn_ref, kc_ref, vc_ref, bias_ref, o_ref, kbuf, vbuf, *, s, past):
    tail = jnp.zeros((KEY_WIN - past - s, ATTN_WIDTH), BF16)
    kbuf[0:past, :] = kc_ref[...].astype(BF16)
    vbuf[0:past, :] = vc_ref[...].astype(BF16)
    kbuf[past:past + s, :] = kn_ref[...].astype(BF16)
    vbuf[past:past + s, :] = vn_ref[...].astype(BF16)
    kbuf[past + s:KEY_WIN, :] = tail
    vbuf[past + s:KEY_WIN, :] = tail
    for p in range(N_PAIRS):
        sl = slice(p * PAIR_W, (p + 1) * PAIR_W)
        bias2 = jnp.concatenate([bias_ref[2 * p], bias_ref[2 * p + 1]], axis=0)
        o = _attend_pair(q_ref[:, sl], kbuf[:, sl], vbuf[:, sl], bias2, None)
        o_ref[:, sl] = o.astype(BF16)


def _attn_sample(z_attn, k_cache, v_cache, bias, *, nb, s):
    past = k_cache.shape[1]
    assert past + s <= KEY_WIN and z_attn.shape[0] == nb * s
    blk = (s, ATTN_WIDTH)
    cblk = (None, past, ATTN_WIDTH)
    blocks = (3 * _nbytes(blk, F32) + 2 * _nbytes((past, ATTN_WIDTH), F32) + _nbytes(bias.shape, F32)
              + _nbytes(blk, BF16))
    scratch = 2 * _nbytes((KEY_WIN, ATTN_WIDTH), BF16)
    return pl.pallas_call(
        functools.partial(_attn_sample_kernel, s=s, past=past),
        out_shape=jax.ShapeDtypeStruct((nb * s, ATTN_WIDTH), BF16),
        grid=(nb,),
        in_specs=[pl.BlockSpec(blk, lambda b: (b, 0)),
                  pl.BlockSpec(blk, lambda b: (b, 1)),
                  pl.BlockSpec(blk, lambda b: (b, 2)),
                  pl.BlockSpec(cblk, lambda b: (b, 0, 0)),
                  pl.BlockSpec(cblk, lambda b: (b, 0, 0)),
                  pl.BlockSpec(bias.shape, lambda b: (0, 0, 0))],
        out_specs=pl.BlockSpec(blk, lambda b: (b, 0)),
        scratch_shapes=[pltpu.VMEM((KEY_WIN, ATTN_WIDTH), BF16), pltpu.VMEM((KEY_WIN, ATTN_WIDTH), BF16)],
        compiler_params=_params(1, _vmem_limit(blocks, scratch, 8 * _nbytes((2 * s, KEY_WIN), F32))),
        name="attn_sample",
    )(z_attn, z_attn, z_attn, k_cache, v_cache, bias)


def _split2(x):
    hi = x.astype(BF16)
    return hi, (x - hi.astype(F32)).astype(BF16)


def _dot3(a, b, dims=(((1,), (0,)), ((), ()))):
    ah, al = _split2(a)
    bh, bl = _split2(b)
    d = lambda x, y: lax.dot_general(x, y, dims, preferred_element_type=F32)
    return d(ah, bh) + d(ah, bl) + d(al, bh)


_NT = (((1,), (1,)), ((), ()))


def _bd(y, lo):
    return jnp.concatenate([jnp.where(lo, y, 0.0), jnp.where(lo, 0.0, y)], axis=0)


def _seg_sum(x, ones_bd):
    n = x.shape[1] // PAIR_W
    stacked = jnp.concatenate([x[:, p * PAIR_W:(p + 1) * PAIR_W] for p in range(n)], axis=0)
    hi, lo = _split2(stacked)
    s = jnp.dot(hi, ones_bd, preferred_element_type=F32) + jnp.dot(lo, ones_bd, preferred_element_type=F32)
    r = x.shape[0]
    return jnp.concatenate([s[p * r:(p + 1) * r] for p in range(n)], axis=1)


def _sigmoid(x):
    return 1.0 / (1.0 + jnp.exp(-x))


def _softplus(x):
    return jnp.maximum(x, 0.0) + jnp.log(1.0 + jnp.exp(-jnp.abs(x)))


def _rwkv_kernel(z_ref, sh_ref, s0_ref, mu_ref, w0_ref, ww2_ref, a0_ref, aw2_ref, gw2_ref, kk_ref, ka_ref,
                 rk_ref, lnw_ref, lnb_ref, o_ref, sfin_ref, s_scr, carry_scr, y_scr, *, rows):
    ct = CHUNK_T
    c = pl.program_id(1)

    @pl.when(c == 0)
    def _():
        s_scr[...] = s0_ref[...]
        carry_scr[0:1, :] = sh_ref[...]

    z = z_ref[...]
    if rows < ct:
        z = jnp.concatenate([z, jnp.zeros((ct - rows, SHIFT_W), F32)], axis=0)
    row = lax.broadcasted_iota(jnp.int32, (ct, 1), 0)
    zprev = jnp.where(row == 0, carry_scr[0:1, :], pltpu.roll(z, 1, axis=0))
    carry_scr[0:1, :] = z[rows - 1:rows, :]
    zs = z + (zprev - z) * mu_ref[...]

    r = zs[:, 0:RWKV_WIDTH]
    k = zs[:, RWKV_WIDTH:2 * RWKV_WIDTH]
    v = zs[:, 2 * RWKV_WIDTH:3 * RWKV_WIDTH]
    zwa = zs[:, 3 * RWKV_WIDTH:3 * RWKV_WIDTH + DECAY_RANK + AAA_RANK]
    zg = zs[:, 3 * RWKV_WIDTH + DECAY_RANK + AAA_RANK:SHIFT_W]

    lo_t = _lane_lo(ct)
    w2cat = jnp.concatenate([ww2_ref[...], aw2_ref[...]], axis=0)
    wl = _dot3(jnp.where(lo_t, jnp.tanh(zwa), 0.0), w2cat)
    al = _dot3(jnp.where(lo_t, 0.0, zwa), w2cat)
    wlog = -_softplus(-(w0_ref[...] + wl)) - 0.5
    lw = -jnp.exp(wlog)
    a = _sigmoid(a0_ref[...] + al)
    g = _dot3(_sigmoid(zg), gw2_ref[...])

    i128 = lax.broadcasted_iota(jnp.int32, (PAIR_W, PAIR_W), 0) // HEAD_DIM
    j128 = lax.broadcasted_iota(jnp.int32, (PAIR_W, PAIR_W), 1) // HEAD_DIM
    ones_bd = (i128 == j128).astype(BF16)

    kkr = k * kk_ref[...]
    kn = kkr / jnp.maximum(jnp.sqrt(_seg_sum(kkr * kkr, ones_bd)), 1e-12)
    k2 = k * (1.0 + (a - 1.0) * ka_ref[...])
    bonus = _seg_sum(r * k2 * rk_ref[...], ones_bd) * v

    if rows < ct:
        live = row < rows
        lw = jnp.where(live, lw, 0.0)
    tri = (lax.broadcasted_iota(jnp.int32, (ct, ct), 1) <= lax.broadcasted_iota(jnp.int32, (ct, ct), 0)).astype(BF16)
    l1 = lw.astype(BF16)
    rem = lw - l1.astype(F32)
    l2 = rem.astype(BF16)
    l3 = (rem - l2.astype(F32)).astype(BF16)
    cl = (jnp.dot(tri, l1, preferred_element_type=F32) + jnp.dot(tri, l2, preferred_element_type=F32)
          + jnp.dot(tri, l3, preferred_element_type=F32))
    e_in = jnp.exp(cl)
    e_ex = jnp.exp(cl - lw)
    e_inv = jnp.exp(-cl)
    at = -kn * e_ex
    bt = kn * a * e_inv
    kt = k2 * e_inv
    if rows < ct:
        bt = jnp.where(live, bt, 0.0)
        kt = jnp.where(live, kt, 0.0)
    rt = r * e_in
    pc = e_in[ct - 1:ct, :]
    bp = bt * pc
    kp = kt * pc

    t_lane = lax.broadcasted_iota(jnp.int32, (ct, 2 * ct), 1) % ct
    strict = t_lane < row
    incl = t_lane <= row
    lo_p = lax.broadcasted_iota(jnp.int32, (ct, 2 * ct), 1) < ct
    diag = (lax.broadcasted_iota(jnp.int32, (HEAD_DIM, PAIR_W), 1) % HEAD_DIM
            == lax.broadcasted_iota(jnp.int32, (HEAD_DIM, PAIR_W), 0))
    lo_s = _lane_lo(HEAD_DIM)
    n_dbl = int(math.log2(ct))
    assert 1 << n_dbl == ct

    def tn(x, y):
        zz = _dot3(x.T, y)
        return jnp.where(lo_s, zz[:HEAD_DIM], zz[HEAD_DIM:])

    for p in range(N_PAIRS):
        sl = slice(p * PAIR_W, (p + 1) * PAIR_W)
        a_, b_, k_, r_, v_ = at[:, sl], bt[:, sl], kt[:, sl], rt[:, sl], v[:, sl]
        b_bd, k_bd, v_bd = _bd(b_, lo_t), _bd(k_, lo_t), _bd(v_, lo_t)
        a_ab = jnp.where(strict, _dot3(a_, b_bd, _NT), 0.0)
        a_ak = jnp.where(strict, _dot3(a_, k_bd, _NT), 0.0)
        a_rb = jnp.where(incl, _dot3(r_, b_bd, _NT), 0.0)
        a_rk = jnp.where(incl, _dot3(r_, k_bd, _NT), 0.0)
        x1 = a_
        x2 = _dot3(a_ak, v_bd)
        npow = a_ab
        for it in range(n_dbl):
            x1 = x1 + _dot3(npow, _bd(x1, lo_t))
            x2 = x2 + _dot3(npow, _bd(x2, lo_t))
            if it + 1 < n_dbl:
                npow = _dot3(npow, _bd(npow, lo_p))
        rp = r_ + _dot3(a_rb, _bd(x1, lo_t))
        y0 = _dot3(a_rb, _bd(x2, lo_t)) + _dot3(a_rk, v_bd)
        m_mat = jnp.where(diag, pc[:, sl], 0.0) + tn(x1, bp[:, sl])
        n_mat = tn(x2, bp[:, sl]) + tn(v_, kp[:, sl])
        s_old = s_scr[:, sl]
        y_scr[:, sl] = _dot3(rp, _bd(s_old, lo_s), _NT) + y0
        s_scr[:, sl] = _dot3(s_old, _bd(m_mat, lo_s)) + n_mat

    y = y_scr[...]
    mean = _seg_sum(y, ones_bd) * (1.0 / HEAD_DIM)
    dv = y - mean
    var = _seg_sum(dv * dv, ones_bd) * (1.0 / HEAD_DIM)
    yn = dv * lax.rsqrt(var + GN_EPS) * lnw_ref[...] + lnb_ref[...]
    out = (yn + bonus) * g
    o_ref[...] = out[:rows].astype(BF16)

    @pl.when(c == pl.num_programs(1) - 1)
    def _():
        sfin_ref[...] = s_scr[...]


def _rwkv(z_rwkv, shift0, s0_packed, lp, *, nb, nck, rows, name):
    assert z_rwkv.shape == (nb * nck * rows, SHIFT_W) and rows <= CHUNK_T
    row1 = lambda n: pl.BlockSpec((1, n), lambda b, c: (0, 0))
    full = lambda r, n: pl.BlockSpec((r, n), lambda b, c: (0, 0))
    st_blk = pl.BlockSpec((None, HEAD_DIM, RWKV_WIDTH), lambda b, c: (b, 0, 0))
    blocks = (_nbytes((rows, SHIFT_W), F32) + 2 * _nbytes((HEAD_DIM, RWKV_WIDTH), F32)
              + _nbytes((2 * DECAY_RANK + GATE_RANK + 8, RWKV_WIDTH), F32) + _nbytes((rows, RWKV_WIDTH), BF16))
    scratch = (_nbytes((HEAD_DIM, RWKV_WIDTH), F32) + _nbytes((8, SHIFT_W), F32)
               + _nbytes((CHUNK_T, RWKV_WIDTH), F32))
    out, s_fin = pl.pallas_call(
        functools.partial(_rwkv_kernel, rows=rows),
        out_shape=(jax.ShapeDtypeStruct((nb * nck * rows, RWKV_WIDTH), BF16),
                   jax.ShapeDtypeStruct((nb, HEAD_DIM, RWKV_WIDTH), F32)),
        grid=(nb, nck),
        in_specs=[pl.BlockSpec((rows, SHIFT_W), lambda b, c: (b * nck + c, 0)),
                  pl.BlockSpec((None, 1, SHIFT_W), lambda b, c: (b, 0, 0)),
                  st_blk,
                  row1(SHIFT_W), row1(RWKV_WIDTH), full(DECAY_RANK, RWKV_WIDTH), row1(RWKV_WIDTH),
                  full(AAA_RANK, RWKV_WIDTH), full(GATE_RANK, RWKV_WIDTH), row1(RWKV_WIDTH),
                  row1(RWKV_WIDTH), row1(RWKV_WIDTH), row1(RWKV_WIDTH), row1(RWKV_WIDTH)],
        out_specs=(pl.BlockSpec((rows, RWKV_WIDTH), lambda b, c: (b * nck + c, 0)), st_blk),
        scratch_shapes=[pltpu.VMEM((HEAD_DIM, RWKV_WIDTH), F32), pltpu.VMEM((8, SHIFT_W), F32),
                        pltpu.VMEM((CHUNK_T, RWKV_WIDTH), F32)],
        compiler_params=_params(2, _vmem_limit(blocks, scratch, 64 * _nbytes((CHUNK_T, RWKV_WIDTH), F32))),
        name=name,
    )(z_rwkv, shift0, s0_packed, lp["mu"], lp["w0"], lp["w_w2"], lp["a0"], lp["a_w2"], lp["g_w2"],
      lp["k_k"], lp["k_a"], lp["r_k"], lp["lnx_w"], lp["lnx_b"])
    return out, s_fin


def _oproj_kernel(a_ref, r_ref, wa_ref, wr_ref, x_ref, g_ref, h_ref, n_ref):
    h = (x_ref[...] + jnp.dot(a_ref[...], wa_ref[...], preferred_element_type=F32)
         + jnp.dot(r_ref[...], wr_ref[...], preferred_element_type=F32))
    h_ref[...] = h
    n_ref[...] = _rms(h, g_ref[...]).astype(BF16)


def _oproj(attn, rwkv, w_o, x, g, *, tm, name):
    t = x.shape[0]
    assert t % tm == 0
    half = (ATTN_WIDTH, D_MODEL)
    blocks = (2 * _nbytes((tm, ATTN_WIDTH), BF16) + 2 * _nbytes(half, BF16) + 2 * _nbytes((tm, D_MODEL), F32)
              + _nbytes((tm, D_MODEL), BF16))
    return pl.pallas_call(
        _oproj_kernel,
        out_shape=(jax.ShapeDtypeStruct((t, D_MODEL), F32), jax.ShapeDtypeStruct((t, D_MODEL), BF16)),
        grid=(t // tm,),
        in_specs=[pl.BlockSpec((tm, ATTN_WIDTH), lambda i: (i, 0)),
                  pl.BlockSpec((tm, RWKV_WIDTH), lambda i: (i, 0)),
                  pl.BlockSpec(half, lambda i: (0, 0)),
                  pl.BlockSpec(half, lambda i: (1, 0)),
                  pl.BlockSpec((tm, D_MODEL), lambda i: (i, 0)),
                  pl.BlockSpec((1, D_MODEL), lambda i: (0, 0))],
        out_specs=(pl.BlockSpec((tm, D_MODEL), lambda i: (i, 0)), pl.BlockSpec((tm, D_MODEL), lambda i: (i, 0))),
        compiler_params=_params(1, _vmem_limit(blocks, 0, 2 * _nbytes((tm, D_MODEL), F32))),
        name=name,
    )(attn, rwkv, w_o, w_o, x, g)


def _ffn_kernel(n_ref, h_ref, wg_ref, wu_ref, wd_ref, g_ref, o_ref, acc_ref):
    j = pl.program_id(1)

    @pl.when(j == 0)
    def _():
        acc_ref[...] = h_ref[...]

    n = n_ref[...]
    gate = jnp.dot(n, wg_ref[...], preferred_element_type=F32)
    up = jnp.dot(n, wu_ref[...], preferred_element_type=F32)
    act = (gate * _sigmoid(gate) * up).astype(BF16)
    acc_ref[...] += jnp.dot(act, wd_ref[...], preferred_element_type=F32)

    @pl.when(j == pl.num_programs(1) - 1)
    def _():
        o_ref[...] = _rms(acc_ref[...], g_ref[...])


def _ffn(n2, h, wg, wu, wd, g, *, tm, tf, name):
    t = h.shape[0]
    assert t % tm == 0 and D_FF % tf == 0
    blocks = (_nbytes((tm, D_MODEL), BF16) + 2 * _nbytes((tm, D_MODEL), F32) + 3 * _nbytes((D_MODEL, tf), BF16))
    return pl.pallas_call(
        _ffn_kernel,
        out_shape=jax.ShapeDtypeStruct((t, D_MODEL), F32),
        grid=(t // tm, D_FF // tf),
        in_specs=[pl.BlockSpec((tm, D_MODEL), lambda i, j: (i, 0)),
                  pl.BlockSpec((tm, D_MODEL), lambda i, j: (i, 0)),
                  pl.BlockSpec((D_MODEL, tf), lambda i, j: (0, j)),
                  pl.BlockSpec((D_MODEL, tf), lambda i, j: (0, j)),
                  pl.BlockSpec((tf, D_MODEL), lambda i, j: (j, 0)),
                  pl.BlockSpec((1, D_MODEL), lambda i, j: (0, 0))],
        out_specs=pl.BlockSpec((tm, D_MODEL), lambda i, j: (i, 0)),
        scratch_shapes=[pltpu.VMEM((tm, D_MODEL), F32)],
        compiler_params=_params(2, _vmem_limit(blocks, _nbytes((tm, D_MODEL), F32), 4 * _nbytes((tm, tf), F32))),
        name=name,
    )(n2, h, wg, wu, wd, g)


def _pack_state(s):
    nb = s.shape[0]
    return jnp.transpose(s, (0, 2, 1, 3)).reshape(nb, HEAD_DIM, RWKV_WIDTH)


def _unpack_state(s):
    nb = s.shape[0]
    return jnp.transpose(s.reshape(nb, HEAD_DIM, N_HEADS, HEAD_DIM), (0, 2, 1, 3))


def _trunk(x, lp, wts, attn_fn, shift0, s0, *, nb, seq, tag):
    t = nb * seq
    tm = min(t, 1024)
    z_attn = _inproj(x, lp["ln1"], wts["w_qkv"], tm=tm, tn=1024, name=f"inproj_attn_{tag}")
    z_rwkv = _inproj(x, lp["ln1"], wts["w_rwkv"], tm=tm, tn=SHIFT_W // 2, name=f"inproj_rwkv_{tag}")
    attn = attn_fn(z_attn)
    rows = min(seq, CHUNK_T)
    rwkv, s_fin = _rwkv(z_rwkv, shift0, _pack_state(s0), lp, nb=nb, nck=seq // rows, rows=rows,
                        name=f"rwkv_{tag}")
    tm2 = min(t, 512)
    h, n2 = _oproj(attn, rwkv, wts["w_o"], x, lp["ln2"], tm=tm2, name=f"oproj_{tag}")
    y = _ffn(n2, h, wts["w_gate"], wts["w_up"], wts["w_down"], lp["final_norm"], tm=tm2, tf=512, name=f"ffn_{tag}")
    return y, z_attn, z_rwkv, _unpack_state(s_fin)


def kernel(x_prompt, x_sample, cache_attn_k, cache_attn_v, state_rwkv_wkv, state_rwkv_shift, ln1, w_in, rel_table,
           mu, w0, w_w2, a0, a_w2, g_w2, k_k, k_a, r_k, lnx_w, lnx_b, w_o, ln2, w_gate, w_up, w_down, final_norm):
    depth = w_in.shape[0]
    assert depth == 1, "final RMSNorm is fused into the (single) layer's FFN call"
    b, t, _ = x_prompt.shape
    bd, s, _ = x_sample.shape
    past = cache_attn_k.shape[2]
    n_keep = min(BAND_PAST, t)
    l = 0
    row = lambda p: p[l].reshape(1, -1)
    lp = dict(ln1=row(ln1), mu=row(mu), w0=row(w0), w_w2=w_w2[l], a0=row(a0), a_w2=a_w2[l], g_w2=g_w2[l],
              k_k=row(k_k), k_a=row(k_a), r_k=row(r_k), lnx_w=row(lnx_w), lnx_b=row(lnx_b), ln2=row(ln2),
              final_norm=final_norm.reshape(1, -1))
    wts = dict(w_qkv=w_in[l][:, :3 * ATTN_WIDTH].astype(BF16), w_rwkv=w_in[l][:, 3 * ATTN_WIDTH:].astype(BF16),
               w_o=w_o[l].astype(BF16), w_gate=w_gate[l].astype(BF16), w_up=w_up[l].astype(BF16),
               w_down=w_down[l].astype(BF16))
    table = rel_table[l]

    bias_p = _bias_tiles(table, mq=CHUNK, off=BAND_PAST + CHUNK, col_lo=CHUNK, col_hi=KEY_WIN, name="bias_prompt")
    bias_s = _bias_tiles(table, mq=s, off=past, col_lo=0, col_hi=past + s, name="bias_sample")

    assert b == 1
    yp, zap, zrp, wkv_p = _trunk(
        x_prompt.reshape(b * t, D_MODEL), lp, wts, lambda z: _attn_prompt(z, bias_p),
        jnp.zeros((b, 1, SHIFT_W), F32), jnp.zeros((b, N_HEADS, HEAD_DIM, HEAD_DIM), F32), nb=b, seq=t, tag="p")
    kc = cache_attn_k[l].reshape(bd, past, ATTN_WIDTH)
    vc = cache_attn_v[l].reshape(bd, past, ATTN_WIDTH)
    ys, zas, zrs, wkv_s = _trunk(
        x_sample.reshape(bd * s, D_MODEL), lp, wts, lambda z: _attn_sample(z, kc, vc, bias_s, nb=bd, s=s),
        state_rwkv_shift[l], state_rwkv_wkv[l], nb=bd, seq=s, tag="s")

    heads = lambda z, n, tt: z.reshape(n, tt, N_HEADS, HEAD_DIM)
    zap = zap.reshape(b, t, 3 * ATTN_WIDTH)
    zas = zas.reshape(bd, s, 3 * ATTN_WIDTH)
    k_p = heads(zap[:, t - n_keep:, ATTN_WIDTH:2 * ATTN_WIDTH], b, n_keep)
    v_p = heads(zap[:, t - n_keep:, 2 * ATTN_WIDTH:], b, n_keep)
    k_s = heads(zas[:, :, ATTN_WIDTH:2 * ATTN_WIDTH], bd, s)
    v_s = heads(zas[:, :, 2 * ATTN_WIDTH:], bd, s)
    sh_p = zrp.reshape(b, t, SHIFT_W)[:, t - 1:]
    sh_s = zrs.reshape(bd, s, SHIFT_W)[:, s - 1:]
    return (yp.reshape(b, t, D_MODEL), ys.reshape(bd, s, D_MODEL),
            k_p[None], v_p[None], wkv_p[None], sh_p[None],
            k_s[None], v_s[None], wkv_s[None], sh_s[None])
```

```python
import functools
import math

import jax
import jax.numpy as jnp
from jax import lax
from jax.experimental import pallas as pl
from jax.experimental.pallas import tpu as pltpu

F32 = jnp.float32
BF16 = jnp.bfloat16

D_MODEL = 2048
CHUNK = 64
N_PAST_CHUNKS = 8
BAND_PAST = N_PAST_CHUNKS * CHUNK
ATTN_WIDTH = 1024
HEAD_DIM = 64
N_HEADS = 16
REL_CLIP = 128
N_REL = 2 * REL_CLIP + 1
RWKV_WIDTH = 1024
DECAY_RANK = 64
AAA_RANK = 64
GATE_RANK = 128
SHIFT_W = 3 * RWKV_WIDTH + DECAY_RANK + AAA_RANK + GATE_RANK
D_FF = 5632
RMS_EPS = 1e-6
GN_EPS = 64e-5

V7X_LANES = 128
V7X_VMEM_BYTES = 64 * 1024 * 1024
V7X_SCOPED_VMEM_CAP_BYTES = 60000 * 1024

PAIR_W = 2 * HEAD_DIM
N_PAIRS = N_HEADS // 2
CHUNK_T = 64
KEY_WIN = (N_PAST_CHUNKS + 2) * CHUNK
NEG_BIG = -1e30
ATTN_QB = 512

assert PAIR_W == V7X_LANES


def _vmem_limit(pipelined_bytes, scratch_bytes, temp_bytes):
    need = 2 * pipelined_bytes + scratch_bytes + temp_bytes
    return int(min(max(need, 16 * 1024 * 1024), V7X_SCOPED_VMEM_CAP_BYTES))


def _nbytes(shape, dtype):
    return math.prod(shape) * jnp.dtype(dtype).itemsize


def _params(n_grid, vmem):
    return pltpu.CompilerParams(dimension_semantics=("arbitrary",) * n_grid, vmem_limit_bytes=vmem)


def _rms(x, g):
    return x * lax.rsqrt(jnp.mean(x * x, axis=-1, keepdims=True) + RMS_EPS) * g


def _inproj_kernel(x_ref, g_ref, w_ref, o_ref, n_ref):
    @pl.when(pl.program_id(1) == 0)
    def _():
        n_ref[...] = _rms(x_ref[...], g_ref[...]).astype(BF16)

    o_ref[...] = jnp.dot(n_ref[...], w_ref[...], preferred_element_type=F32)


def _inproj(x, g, w, *, tm, tn, name):
    t, d = x.shape
    n = w.shape[1]
    assert t % tm == 0 and n % tn == 0
    blocks = _nbytes((tm, d), F32) + _nbytes((d, tn), BF16) + _nbytes((tm, tn), F32)
    return pl.pallas_call(
        _inproj_kernel,
        out_shape=jax.ShapeDtypeStruct((t, n), F32),
        grid=(t // tm, n // tn),
        in_specs=[pl.BlockSpec((tm, d), lambda i, j: (i, 0)),
                  pl.BlockSpec((1, d), lambda i, j: (0, 0)),
                  pl.BlockSpec((d, tn), lambda i, j: (0, j))],
        out_specs=pl.BlockSpec((tm, tn), lambda i, j: (i, j)),
        scratch_shapes=[pltpu.VMEM((tm, d), BF16)],
        compiler_params=_params(2, _vmem_limit(blocks, _nbytes((tm, d), BF16), 2 * _nbytes((tm, d), F32))),
        name=name,
    )(x, g, w)


BIAS_VAR_COL0 = KEY_WIN - 2 * V7X_LANES


def _bias_kernel(tab_ref, o_ref, *, mq, off, col_lo, col_hi):
    nvar = KEY_WIN - BIAS_VAR_COL0
    q = lax.broadcasted_iota(jnp.int32, (mq, nvar), 0)
    c = lax.broadcasted_iota(jnp.int32, (mq, nvar), 1) + BIAS_VAR_COL0
    idx = jnp.clip(off + q - c, -REL_CLIP, REL_CLIP) + REL_CLIP
    valid_var = (c >= col_lo) & (c < col_hi)
    c_far = lax.broadcasted_iota(jnp.int32, (mq, BIAS_VAR_COL0), 1)
    valid_far = (c_far >= col_lo) & (c_far < col_hi)

    def body(j, accs):
        hit = idx == j
        return tuple(jnp.where(hit, tab_ref[h, j], accs[h]) for h in range(N_HEADS))

    init = tuple(jnp.zeros((mq, nvar), F32) for _ in range(N_HEADS))
    accs = lax.fori_loop(0, N_REL, body, init)
    for h in range(N_HEADS):
        far = jnp.full((mq, BIAS_VAR_COL0), tab_ref[h, N_REL - 1], F32)
        o_ref[h, :, 0:BIAS_VAR_COL0] = jnp.where(valid_far, far, NEG_BIG)
        o_ref[h, :, BIAS_VAR_COL0:KEY_WIN] = jnp.where(valid_var, accs[h], NEG_BIG)


def _bias_tiles(table, *, mq, off, col_lo, col_hi, name):
    assert off - (BIAS_VAR_COL0 - 1) >= REL_CLIP
    return pl.pallas_call(
        functools.partial(_bias_kernel, mq=mq, off=off, col_lo=col_lo, col_hi=col_hi),
        out_shape=jax.ShapeDtypeStruct((N_HEADS, mq, KEY_WIN), F32),
        in_specs=[pl.BlockSpec(memory_space=pltpu.SMEM)],
        out_specs=pl.BlockSpec(memory_space=pltpu.VMEM),
        name=name,
    )(table)


def _lane_lo(rows):
    return lax.broadcasted_iota(jnp.int32, (rows, PAIR_W), 1) < HEAD_DIM


def _attend_pair(q, kw, vw, bias2, min_col):
    mq = q.shape[0]
    lo = _lane_lo(mq)
    qs = q * (HEAD_DIM ** -0.5)
    q2 = jnp.concatenate([jnp.where(lo, qs, 0.0), jnp.where(lo, 0.0, qs)], axis=0).astype(BF16)
    s = lax.dot_general(q2, kw, (((1,), (1,)), ((), ())), preferred_element_type=F32) + bias2
    if min_col is not None:
        col = lax.broadcasted_iota(jnp.int32, (1, KEY_WIN), 1)
        s = jnp.where(col >= min_col, s, NEG_BIG)
    m = jnp.max(s, axis=-1, keepdims=True)
    p = jnp.exp(s - m)
    l = jnp.sum(p, axis=-1, keepdims=True)
    pv = jnp.dot(p.astype(BF16), vw, preferred_element_type=F32) * (1.0 / l)
    return jnp.where(lo, pv[:mq], pv[mq:])


def _attn_prompt_kernel(q_ref, kp_ref, kc_ref, vp_ref, vc_ref, bias_ref, o_ref, kbuf, vbuf):
    i = pl.program_id(0)
    lead = jnp.zeros((CHUNK, ATTN_WIDTH), BF16)
    kbuf[0:CHUNK, :] = lead
    vbuf[0:CHUNK, :] = lead
    kbuf[CHUNK:CHUNK + ATTN_QB, :] = kp_ref[...].astype(BF16)
    vbuf[CHUNK:CHUNK + ATTN_QB, :] = vp_ref[...].astype(BF16)
    kbuf[CHUNK + ATTN_QB:CHUNK + 2 * ATTN_QB, :] = kc_ref[...].astype(BF16)
    vbuf[CHUNK + ATTN_QB:CHUNK + 2 * ATTN_QB, :] = vc_ref[...].astype(BF16)

    def chunk_body(j, carry):
        r0 = pl.multiple_of(j * CHUNK, CHUNK)
        min_col = (N_PAST_CHUNKS + 1 - (i * (ATTN_QB // CHUNK) + j)) * CHUNK
        for p in range(N_PAIRS):
            sl = slice(p * PAIR_W, (p + 1) * PAIR_W)
            bias2 = jnp.concatenate([bias_ref[2 * p], bias_ref[2 * p + 1]], axis=0)
            o = _attend_pair(q_ref[pl.ds(r0, CHUNK), sl], kbuf[pl.ds(r0, KEY_WIN), sl],
                             vbuf[pl.ds(r0, KEY_WIN), sl], bias2, min_col)
            o_ref[pl.ds(r0, CHUNK), sl] = o.astype(BF16)
        return carry

    lax.fori_loop(0, ATTN_QB // CHUNK, chunk_body, 0)


def _attn_prompt(z_attn, bias):
    t = z_attn.shape[0]
    assert t % ATTN_QB == 0
    blk = (ATTN_QB, ATTN_WIDTH)
    buf_rows = CHUNK + 2 * ATTN_QB
    blocks = 5 * _nbytes(blk, F32) + _nbytes(bias.shape, F32) + _nbytes(blk, BF16)
    scratch = 2 * _nbytes((buf_rows, ATTN_WIDTH), BF16)
    return pl.pallas_call(
        _attn_prompt_kernel,
        out_shape=jax.ShapeDtypeStruct((t, ATTN_WIDTH), BF16),
        grid=(t // ATTN_QB,),
        in_specs=[pl.BlockSpec(blk, lambda i: (i, 0)),
                  pl.BlockSpec(blk, lambda i: (jnp.maximum(i - 1, 0), 1)),
                  pl.BlockSpec(blk, lambda i: (i, 1)),
                  pl.BlockSpec(blk, lambda i: (jnp.maximum(i - 1, 0), 2)),
                  pl.BlockSpec(blk, lambda i: (i, 2)),
                  pl.BlockSpec(bias.shape, lambda i: (0, 0, 0))],
        out_specs=pl.BlockSpec(blk, lambda i: (i, 0)),
        scratch_shapes=[pltpu.VMEM((buf_rows, ATTN_WIDTH), BF16), pltpu.VMEM((buf_rows, ATTN_WIDTH), BF16)],
        compiler_params=_params(1, _vmem_limit(blocks, scratch, 8 * _nbytes((2 * CHUNK, KEY_WIN), F32))),
        name="attn_prompt",
    )(z_attn, z_attn, z_attn, z_attn, z_attn, bias)


def _attn_sample_kernel(q_ref, kn_ref, vn_ref, kc_ref, vc_ref, bias_ref, o_ref, kbuf, vbuf, *, s, past):
    tail = jnp.zeros((KEY_WIN - past - s, ATTN_WIDTH), BF16)
    kbuf[0:past, :] = kc_ref[...].astype(BF16)
    vbuf[0:past, :] = vc_ref[...].astype(BF16)
    kbuf[past:past + s, :] = kn_ref[...].astype(BF16)
    vbuf[past:past + s, :] = vn_ref[...].astype(BF16)
    kbuf[past + s:KEY_WIN, :] = tail
    vbuf[past + s:KEY_WIN, :] = tail
    for p in range(N_PAIRS):
        sl = slice(p * PAIR_W, (p + 1) * PAIR_W)
        bias2 = jnp.concatenate([bias_ref[2 * p], bias_ref[2 * p + 1]], axis=0)
        o = _attend_pair(q_ref[:, sl], kbuf[:, sl], vbuf[:, sl], bias2, None)
        o_ref[:, sl] = o.astype(BF16)


def _attn_sample(z_attn, k_cache, v_cache, bias, *, nb, s):
    past = k_cache.shape[1]
    assert past + s <= KEY_WIN and z_attn.shape[0] == nb * s
    blk = (s, ATTN_WIDTH)
    cblk = (None, past, ATTN_WIDTH)
    blocks = (3 * _nbytes(blk, F32) + 2 * _nbytes((past, ATTN_WIDTH), F32) + _nbytes(bias.shape, F32)
              + _nbytes(blk, BF16))
    scratch = 2 * _nbytes((KEY_WIN, ATTN_WIDTH), BF16)
    return pl.pallas_call(
        functools.partial(_attn_sample_kernel, s=s, past=past),
        out_shape=jax.ShapeDtypeStruct((nb * s, ATTN_WIDTH), BF16),
        grid=(nb,),
        in_specs=[pl.BlockSpec(blk, lambda b: (b, 0)),
                  pl.BlockSpec(blk, lambda b: (b, 1)),
                  pl.BlockSpec(blk, lambda b: (b, 2)),
                  pl.BlockSpec(cblk, lambda b: (b, 0, 0)),
                  pl.BlockSpec(cblk, lambda b: (b, 0, 0)),
                  pl.BlockSpec(bias.shape, lambda b: (0, 0, 0))],
        out_specs=pl.BlockSpec(blk, lambda b: (b, 0)),
        scratch_shapes=[pltpu.VMEM((KEY_WIN, ATTN_WIDTH), BF16), pltpu.VMEM((KEY_WIN, ATTN_WIDTH), BF16)],
        compiler_params=_params(1, _vmem_limit(blocks, scratch, 8 * _nbytes((2 * s, KEY_WIN), F32))),
        name="attn_sample",
    )(z_attn, z_attn, z_attn, k_cache, v_cache, bias)


def _split2(x):
    hi = x.astype(BF16)
    return hi, (x - hi.astype(F32)).astype(BF16)


def _dot3(a, b, dims=(((1,), (0,)), ((), ()))):
    ah, al = _split2(a)
    bh, bl = _split2(b)
    d = lambda x, y: lax.dot_general(x, y, dims, preferred_element_type=F32)
    return d(ah, bh) + d(ah, bl) + d(al, bh)


_NT = (((1,), (1,)), ((), ()))


def _bd(y, lo):
    return jnp.concatenate([jnp.where(lo, y, 0.0), jnp.where(lo, 0.0, y)], axis=0)


def _seg_sum(x, ones_bd):
    n = x.shape[1] // PAIR_W
    stacked = jnp.concatenate([x[:, p * PAIR_W:(p + 1) * PAIR_W] for p in range(n)], axis=0)
    hi, lo = _split2(stacked)
    s = jnp.dot(hi, ones_bd, preferred_element_type=F32) + jnp.dot(lo, ones_bd, preferred_element_type=F32)
    r = x.shape[0]
    return jnp.concatenate([s[p * r:(p + 1) * r] for p in range(n)], axis=1)


def _sigmoid(x):
    return 1.0 / (1.0 + jnp.exp(-x))


def _softplus(x):
    return jnp.maximum(x, 0.0) + jnp.log(1.0 + jnp.exp(-jnp.abs(x)))


def _rwkv_kernel(z_ref, sh_ref, s0_ref, mu_ref, w0_ref, ww2_ref, a0_ref, aw2_ref, gw2_ref, kk_ref, ka_ref,
                 rk_ref, lnw_ref, lnb_ref, o_ref, sfin_ref, s_scr, carry_scr, y_scr, *, rows):
    ct = CHUNK_T
    c = pl.program_id(1)

    @pl.when(c == 0)
    def _():
        s_scr[...] = s0_ref[...]
        carry_scr[0:1, :] = sh_ref[...]

    z = z_ref[...]
    if rows < ct:
        z = jnp.concatenate([z, jnp.zeros((ct - rows, SHIFT_W), F32)], axis=0)
    row = lax.broadcasted_iota(jnp.int32, (ct, 1), 0)
    zprev = jnp.where(row == 0, carry_scr[0:1, :], pltpu.roll(z, 1, axis=0))
    carry_scr[0:1, :] = z[rows - 1:rows, :]
    zs = z + (zprev - z) * mu_ref[...]

    r = zs[:, 0:RWKV_WIDTH]
    k = zs[:, RWKV_WIDTH:2 * RWKV_WIDTH]
    v = zs[:, 2 * RWKV_WIDTH:3 * RWKV_WIDTH]
    zwa = zs[:, 3 * RWKV_WIDTH:3 * RWKV_WIDTH + DECAY_RANK + AAA_RANK]
    zg = zs[:, 3 * RWKV_WIDTH + DECAY_RANK + AAA_RANK:SHIFT_W]

    lo_t = _lane_lo(ct)
    w2cat = jnp.concatenate([ww2_ref[...], aw2_ref[...]], axis=0)
    wl = _dot3(jnp.where(lo_t, jnp.tanh(zwa), 0.0), w2cat)
    al = _dot3(jnp.where(lo_t, 0.0, zwa), w2cat)
    wlog = -_softplus(-(w0_ref[...] + wl)) - 0.5
    lw = -jnp.exp(wlog)
    a = _sigmoid(a0_ref[...] + al)
    g = jnp.dot(_sigmoid(zg).astype(BF16), gw2_ref[...].astype(BF16), preferred_element_type=F32)

    i128 = lax.broadcasted_iota(jnp.int32, (PAIR_W, PAIR_W), 0) // HEAD_DIM
    j128 = lax.broadcasted_iota(jnp.int32, (PAIR_W, PAIR_W), 1) // HEAD_DIM
    ones_bd = (i128 == j128).astype(BF16)

    kkr = k * kk_ref[...]
    kn = kkr / jnp.maximum(jnp.sqrt(_seg_sum(kkr * kkr, ones_bd)), 1e-12)
    k2 = k * (1.0 + (a - 1.0) * ka_ref[...])
    bonus = _seg_sum(r * k2 * rk_ref[...], ones_bd) * v

    if rows < ct:
        live = row < rows
        lw = jnp.where(live, lw, 0.0)
    tri = (lax.broadcasted_iota(jnp.int32, (ct, ct), 1) <= lax.broadcasted_iota(jnp.int32, (ct, ct), 0)).astype(BF16)
    l1 = lw.astype(BF16)
    rem = lw - l1.astype(F32)
    l2 = rem.astype(BF16)
    l3 = (rem - l2.astype(F32)).astype(BF16)
    cl = (jnp.dot(tri, l1, preferred_element_type=F32) + jnp.dot(tri, l2, preferred_element_type=F32)
          + jnp.dot(tri, l3, preferred_element_type=F32))
    e_in = jnp.exp(cl)
    e_ex = jnp.exp(cl - lw)
    e_inv = jnp.exp(-cl)
    at = -kn * e_ex
    bt = kn * a * e_inv
    kt = k2 * e_inv
    if rows < ct:
        bt = jnp.where(live, bt, 0.0)
        kt = jnp.where(live, kt, 0.0)
    rt = r * e_in
    pc = e_in[ct - 1:ct, :]
    bp = bt * pc
    kp = kt * pc

    pw = PAIR_W
    t_lane = lax.broadcasted_iota(jnp.int32, (ct, 2 * ct), 1) % ct
    strict = t_lane < row
    incl = t_lane <= row
    lo_p = lax.broadcasted_iota(jnp.int32, (ct, 2 * ct), 1) < ct
    lo_t2 = lax.broadcasted_iota(jnp.int32, (ct, 2 * pw), 1) % pw < HEAD_DIM
    diag = (lax.broadcasted_iota(jnp.int32, (HEAD_DIM, pw), 1) % HEAD_DIM
            == lax.broadcasted_iota(jnp.int32, (HEAD_DIM, pw), 0))
    lo_s = _lane_lo(HEAD_DIM)
    n_dbl = int(math.log2(ct))
    assert 1 << n_dbl == ct and 2 * ct == pw
    pairs = range(N_PAIRS)
    cat0 = lambda xs: jnp.concatenate(xs, axis=0)
    cat1 = lambda xs: jnp.concatenate(xs, axis=1)
    tok = lambda x: [x[:, p * pw:(p + 1) * pw] for p in pairs]

    def mm(x, y, dims=(((1,), (0,)), ((), ()))):
        return lax.dot_general(x.astype(BF16), y.astype(BF16), dims, preferred_element_type=F32)

    def tn(x, y):
        zz = mm(x.T, y)
        return jnp.where(lo_s, zz[:HEAD_DIM], zz[HEAD_DIM:])

    a_, b_, k_, r_, v_, bp_, kp_ = tok(at), tok(bt), tok(kt), tok(rt), tok(v), tok(bp), tok(kp)
    v_bd = [_bd(v_[p], lo_t) for p in pairs]
    amat = [mm(cat0([a_[p], r_[p]]), cat0([_bd(b_[p], lo_t), _bd(k_[p], lo_t)]), _NT) for p in pairs]
    a_ab = [jnp.where(strict, amat[p][:ct, :pw], 0.0) for p in pairs]
    a_ak = [jnp.where(strict, amat[p][:ct, pw:], 0.0) for p in pairs]
    a_rb = [jnp.where(incl, amat[p][ct:, :pw], 0.0) for p in pairs]
    a_rk = [jnp.where(incl, amat[p][ct:, pw:], 0.0) for p in pairs]
    x = [cat1([a_[p], mm(a_ak[p], v_bd[p])]) for p in pairs]
    npow = a_ab
    for it in range(n_dbl):
        if it + 1 < n_dbl:
            res = [mm(npow[p], cat1([_bd(x[p], lo_t2), _bd(npow[p], lo_p)])) for p in pairs]
            npow = [res[p][:, 2 * pw:] for p in pairs]
        else:
            res = [mm(npow[p], _bd(x[p], lo_t2)) for p in pairs]
        x = [x[p] + res[p][:, :2 * pw] for p in pairs]
    ry = [mm(a_rb[p], _bd(x[p], lo_t2)) for p in pairs]
    rp = [r_[p] + ry[p][:, :pw] for p in pairs]
    y0 = [ry[p][:, pw:] + mm(a_rk[p], v_bd[p]) for p in pairs]
    m_mat = [jnp.where(diag, pc[:, p * pw:(p + 1) * pw], 0.0) + tn(x[p][:, :pw], bp_[p]) for p in pairs]
    n_mat = [tn(cat0([x[p][:, pw:], v_[p]]), cat0([bp_[p], kp_[p]])) for p in pairs]
    for p in pairs:
        sl = slice(p * pw, (p + 1) * pw)
        s_old = s_scr[:, sl]
        y_scr[:, sl] = mm(rp[p], _bd(s_old, lo_s), _NT) + y0[p]
        s_scr[:, sl] = mm(s_old, _bd(m_mat[p], lo_s)) + n_mat[p]

    y = y_scr[...]
    mean = _seg_sum(y, ones_bd) * (1.0 / HEAD_DIM)
    dv = y - mean
    var = _seg_sum(dv * dv, ones_bd) * (1.0 / HEAD_DIM)
    yn = dv * lax.rsqrt(var + GN_EPS) * lnw_ref[...] + lnb_ref[...]
    out = (yn + bonus) * g
    o_ref[...] = out[:rows].astype(BF16)

    @pl.when(c == pl.num_programs(1) - 1)
    def _():
        sfin_ref[...] = s_scr[...]


def _rwkv(z_rwkv, shift0, s0_packed, lp, *, nb, nck, rows, name):
    assert z_rwkv.shape == (nb * nck * rows, SHIFT_W) and rows <= CHUNK_T
    row1 = lambda n: pl.BlockSpec((1, n), lambda b, c: (0, 0))
    full = lambda r, n: pl.BlockSpec((r, n), lambda b, c: (0, 0))
    st_blk = pl.BlockSpec((None, HEAD_DIM, RWKV_WIDTH), lambda b, c: (b, 0, 0))
    blocks = (_nbytes((rows, SHIFT_W), F32) + 2 * _nbytes((HEAD_DIM, RWKV_WIDTH), F32)
              + _nbytes((2 * DECAY_RANK + GATE_RANK + 8, RWKV_WIDTH), F32) + _nbytes((rows, RWKV_WIDTH), BF16))
    scratch = (_nbytes((HEAD_DIM, RWKV_WIDTH), F32) + _nbytes((8, SHIFT_W), F32)
               + _nbytes((CHUNK_T, RWKV_WIDTH), F32))
    out, s_fin = pl.pallas_call(
        functools.partial(_rwkv_kernel, rows=rows),
        out_shape=(jax.ShapeDtypeStruct((nb * nck * rows, RWKV_WIDTH), BF16),
                   jax.ShapeDtypeStruct((nb, HEAD_DIM, RWKV_WIDTH), F32)),
        grid=(nb, nck),
        in_specs=[pl.BlockSpec((rows, SHIFT_W), lambda b, c: (b * nck + c, 0)),
                  pl.BlockSpec((None, 1, SHIFT_W), lambda b, c: (b, 0, 0)),
                  st_blk,
                  row1(SHIFT_W), row1(RWKV_WIDTH), full(DECAY_RANK, RWKV_WIDTH), row1(RWKV_WIDTH),
                  full(AAA_RANK, RWKV_WIDTH), full(GATE_RANK, RWKV_WIDTH), row1(RWKV_WIDTH),
                  row1(RWKV_WIDTH), row1(RWKV_WIDTH), row1(RWKV_WIDTH), row1(RWKV_WIDTH)],
        out_specs=(pl.BlockSpec((rows, RWKV_WIDTH), lambda b, c: (b * nck + c, 0)), st_blk),
        scratch_shapes=[pltpu.VMEM((HEAD_DIM, RWKV_WIDTH), F32), pltpu.VMEM((8, SHIFT_W), F32),
                        pltpu.VMEM((CHUNK_T, RWKV_WIDTH), F32)],
        compiler_params=_params(2, _vmem_limit(blocks, scratch, 64 * _nbytes((CHUNK_T, RWKV_WIDTH), F32))),
        name=name,
    )(z_rwkv, shift0, s0_packed, lp["mu"], lp["w0"], lp["w_w2"], lp["a0"], lp["a_w2"], lp["g_w2"],
      lp["k_k"], lp["k_a"], lp["r_k"], lp["lnx_w"], lp["lnx_b"])
    return out, s_fin


def _oproj_kernel(a_ref, r_ref, wa_ref, wr_ref, x_ref, g_ref, h_ref, n_ref):
    h = (x_ref[...] + jnp.dot(a_ref[...], wa_ref[...], preferred_element_type=F32)
         + jnp.dot(r_ref[...], wr_ref[...], preferred_element_type=F32))
    h_ref[...] = h
    n_ref[...] = _rms(h, g_ref[...]).astype(BF16)


def _oproj(attn, rwkv, w_o, x, g, *, tm, name):
    t = x.shape[0]
    assert t % tm == 0
    half = (ATTN_WIDTH, D_MODEL)
    blocks = (2 * _nbytes((tm, ATTN_WIDTH), BF16) + 2 * _nbytes(half, BF16) + 2 * _nbytes((tm, D_MODEL), F32)
              + _nbytes((tm, D_MODEL), BF16))
    return pl.pallas_call(
        _oproj_kernel,
        out_shape=(jax.ShapeDtypeStruct((t, D_MODEL), F32), jax.ShapeDtypeStruct((t, D_MODEL), BF16)),
        grid=(t // tm,),
        in_specs=[pl.BlockSpec((tm, ATTN_WIDTH), lambda i: (i, 0)),
                  pl.BlockSpec((tm, RWKV_WIDTH), lambda i: (i, 0)),
                  pl.BlockSpec(half, lambda i: (0, 0)),
                  pl.BlockSpec(half, lambda i: (1, 0)),
                  pl.BlockSpec((tm, D_MODEL), lambda i: (i, 0)),
                  pl.BlockSpec((1, D_MODEL), lambda i: (0, 0))],
        out_specs=(pl.BlockSpec((tm, D_MODEL), lambda i: (i, 0)), pl.BlockSpec((tm, D_MODEL), lambda i: (i, 0))),
        compiler_params=_params(1, _vmem_limit(blocks, 0, 2 * _nbytes((tm, D_MODEL), F32))),
        name=name,
    )(attn, rwkv, w_o, w_o, x, g)


def _ffn_kernel(n_ref, h_ref, wg_ref, wu_ref, wd_ref, g_ref, o_ref, acc_ref):
    j = pl.program_id(1)

    @pl.when(j == 0)
    def _():
        acc_ref[...] = h_ref[...]

    n = n_ref[...]
    gate = jnp.dot(n, wg_ref[...], preferred_element_type=F32)
    up = jnp.dot(n, wu_ref[...], preferred_element_type=F32)
    act = (gate * _sigmoid(gate) * up).astype(BF16)
    acc_ref[...] += jnp.dot(act, wd_ref[...], preferred_element_type=F32)

    @pl.when(j == pl.num_programs(1) - 1)
    def _():
        o_ref[...] = _rms(acc_ref[...], g_ref[...])


def _ffn(n2, h, wg, wu, wd, g, *, tm, tf, name):
    t = h.shape[0]
    assert t % tm == 0 and D_FF % tf == 0
    blocks = (_nbytes((tm, D_MODEL), BF16) + 2 * _nbytes((tm, D_MODEL), F32) + 3 * _nbytes((D_MODEL, tf), BF16))
    return pl.pallas_call(
        _ffn_kernel,
        out_shape=jax.ShapeDtypeStruct((t, D_MODEL), F32),
        grid=(t // tm, D_FF // tf),
        in_specs=[pl.BlockSpec((tm, D_MODEL), lambda i, j: (i, 0)),
                  pl.BlockSpec((tm, D_MODEL), lambda i, j: (i, 0)),
                  pl.BlockSpec((D_MODEL, tf), lambda i, j: (0, j)),
                  pl.BlockSpec((D_MODEL, tf), lambda i, j: (0, j)),
                  pl.BlockSpec((tf, D_MODEL), lambda i, j: (j, 0)),
                  pl.BlockSpec((1, D_MODEL), lambda i, j: (0, 0))],
        out_specs=pl.BlockSpec((tm, D_MODEL), lambda i, j: (i, 0)),
        scratch_shapes=[pltpu.VMEM((tm, D_MODEL), F32)],
        compiler_params=_params(2, _vmem_limit(blocks, _nbytes((tm, D_MODEL), F32), 4 * _nbytes((tm, tf), F32))),
        name=name,
    )(n2, h, wg, wu, wd, g)


def _pack_state(s):
    nb = s.shape[0]
    return jnp.transpose(s, (0, 2, 1, 3)).reshape(nb, HEAD_DIM, RWKV_WIDTH)


def _unpack_state(s):
    nb = s.shape[0]
    return jnp.transpose(s.reshape(nb, HEAD_DIM, N_HEADS, HEAD_DIM), (0, 2, 1, 3))


def _trunk(x, lp, wts, attn_fn, shift0, s0, *, nb, seq, tag):
    t = nb * seq
    tm = min(t, 1024)
    z_attn = _inproj(x, lp["ln1"], wts["w_qkv"], tm=tm, tn=1024, name=f"inproj_attn_{tag}")
    z_rwkv = _inproj(x, lp["ln1"], wts["w_rwkv"], tm=tm, tn=SHIFT_W // 2, name=f"inproj_rwkv_{tag}")
    attn = attn_fn(z_attn)
    rows = min(seq, CHUNK_T)
    rwkv, s_fin = _rwkv(z_rwkv, shift0, _pack_state(s0), lp, nb=nb, nck=seq // rows, rows=rows,
                        name=f"rwkv_{tag}")
    tm2 = min(t, 512)
    h, n2 = _oproj(attn, rwkv, wts["w_o"], x, lp["ln2"], tm=tm2, name=f"oproj_{tag}")
    y = _ffn(n2, h, wts["w_gate"], wts["w_up"], wts["w_down"], lp["final_norm"], tm=tm2, tf=512, name=f"ffn_{tag}")
    return y, z_attn, z_rwkv, _unpack_state(s_fin)


def kernel(x_prompt, x_sample, cache_attn_k, cache_attn_v, state_rwkv_wkv, state_rwkv_shift, ln1, w_in, rel_table,
           mu, w0, w_w2, a0, a_w2, g_w2, k_k, k_a, r_k, lnx_w, lnx_b, w_o, ln2, w_gate, w_up, w_down, final_norm):
    depth = w_in.shape[0]
    assert depth == 1, "final RMSNorm is fused into the (single) layer's FFN call"
    b, t, _ = x_prompt.shape
    bd, s, _ = x_sample.shape
    past = cache_attn_k.shape[2]
    n_keep = min(BAND_PAST, t)
    l = 0
    row = lambda p: p[l].reshape(1, -1)
    lp = dict(ln1=row(ln1), mu=row(mu), w0=row(w0), w_w2=w_w2[l], a0=row(a0), a_w2=a_w2[l], g_w2=g_w2[l],
              k_k=row(k_k), k_a=row(k_a), r_k=row(r_k), lnx_w=row(lnx_w), lnx_b=row(lnx_b), ln2=row(ln2),
              final_norm=final_norm.reshape(1, -1))
    wts = dict(w_qkv=w_in[l][:, :3 * ATTN_WIDTH].astype(BF16), w_rwkv=w_in[l][:, 3 * ATTN_WIDTH:].astype(BF16),
               w_o=w_o[l].astype(BF16), w_gate=w_gate[l].astype(BF16), w_up=w_up[l].astype(BF16),
               w_down=w_down[l].astype(BF16))
    table = rel_table[l]

    bias_p = _bias_tiles(table, mq=CHUNK, off=BAND_PAST + CHUNK, col_lo=CHUNK, col_hi=KEY_WIN, name="bias_prompt")
    bias_s = _bias_tiles(table, mq=s, off=past, col_lo=0, col_hi=past + s, name="bias_sample")

    assert b == 1
    yp, zap, zrp, wkv_p = _trunk(
        x_prompt.reshape(b * t, D_MODEL), lp, wts, lambda z: _attn_prompt(z, bias_p),
        jnp.zeros((b, 1, SHIFT_W), F32), jnp.zeros((b, N_HEADS, HEAD_DIM, HEAD_DIM), F32), nb=b, seq=t, tag="p")
    kc = cache_attn_k[l].reshape(bd, past, ATTN_WIDTH)
    vc = cache_attn_v[l].reshape(bd, past, ATTN_WIDTH)
    ys, zas, zrs, wkv_s = _trunk(
        x_sample.reshape(bd * s, D_MODEL), lp, wts, lambda z: _attn_sample(z, kc, vc, bias_s, nb=bd, s=s),
        state_rwkv_shift[l], state_rwkv_wkv[l], nb=bd, seq=s, tag="s")

    heads = lambda z, n, tt: z.reshape(n, tt, N_HEADS, HEAD_DIM)
    zap = zap.reshape(b, t, 3 * ATTN_WIDTH)
    zas = zas.reshape(bd, s, 3 * ATTN_WIDTH)
    k_p = heads(zap[:, t - n_keep:, ATTN_WIDTH:2 * ATTN_WIDTH], b, n_keep)
    v_p = heads(zap[:, t - n_keep:, 2 * ATTN_WIDTH:], b, n_keep)
    k_s = heads(zas[:, :, ATTN_WIDTH:2 * ATTN_WIDTH], bd, s)
    v_s = heads(zas[:, :, 2 * ATTN_WIDTH:], bd, s)
    sh_p = zrp.reshape(b, t, SHIFT_W)[:, t - 1:]
    sh_s = zrs.reshape(bd, s, SHIFT_W)[:, s - 1:]
    return (yp.reshape(b, t, D_MODEL), ys.reshape(bd, s, D_MODEL),
            k_p[None], v_p[None], wkv_p[None], sh_p[None],
            k_s[None], v_s[None], wkv_s[None], sh_s[None])
```

```python
import functools
import math

import jax
import jax.numpy as jnp
from jax import lax
from jax.experimental import pallas as pl
from jax.experimental.pallas import tpu as pltpu

F32 = jnp.float32
BF16 = jnp.bfloat16

D_MODEL = 2048
CHUNK = 64
N_PAST_CHUNKS = 8
BAND_PAST = N_PAST_CHUNKS * CHUNK
ATTN_WIDTH = 1024
HEAD_DIM = 64
N_HEADS = 16
REL_CLIP = 128
N_REL = 2 * REL_CLIP + 1
RWKV_WIDTH = 1024
DECAY_RANK = 64
AAA_RANK = 64
GATE_RANK = 128
SHIFT_W = 3 * RWKV_WIDTH + DECAY_RANK + AAA_RANK + GATE_RANK
D_FF = 5632
RMS_EPS = 1e-6
GN_EPS = 64e-5

V7X_LANES = 128
V7X_VMEM_BYTES = 64 * 1024 * 1024
V7X_SCOPED_VMEM_CAP_BYTES = 60000 * 1024

PAIR_W = 2 * HEAD_DIM
N_PAIRS = N_HEADS // 2
CHUNK_T = 64
KEY_WIN = (N_PAST_CHUNKS + 2) * CHUNK
NEG_BIG = -1e30
ATTN_QB = 512
RWKV_CHUNKS_PER_STEP = 2

assert PAIR_W == V7X_LANES


def _vmem_limit(pipelined_bytes, scratch_bytes, temp_bytes):
    need = 2 * pipelined_bytes + scratch_bytes + temp_bytes
    return int(min(max(need, 16 * 1024 * 1024), V7X_SCOPED_VMEM_CAP_BYTES))


def _nbytes(shape, dtype):
    return math.prod(shape) * jnp.dtype(dtype).itemsize


def _params(n_grid, vmem):
    return pltpu.CompilerParams(dimension_semantics=("arbitrary",) * n_grid, vmem_limit_bytes=vmem)


def _rms(x, g):
    return x * lax.rsqrt(jnp.mean(x * x, axis=-1, keepdims=True) + RMS_EPS) * g


def _inproj_kernel(x_ref, g_ref, w_ref, o_ref, n_ref):
    @pl.when(pl.program_id(1) == 0)
    def _():
        n_ref[...] = _rms(x_ref[...], g_ref[...]).astype(BF16)

    o_ref[...] = jnp.dot(n_ref[...], w_ref[...], preferred_element_type=F32)


def _inproj(x, g, w, *, tm, tn, name):
    t, d = x.shape
    n = w.shape[1]
    assert t % tm == 0 and n % tn == 0
    blocks = _nbytes((tm, d), F32) + _nbytes((d, tn), BF16) + _nbytes((tm, tn), F32)
    return pl.pallas_call(
        _inproj_kernel,
        out_shape=jax.ShapeDtypeStruct((t, n), F32),
        grid=(t // tm, n // tn),
        in_specs=[pl.BlockSpec((tm, d), lambda i, j: (i, 0)),
                  pl.BlockSpec((1, d), lambda i, j: (0, 0)),
                  pl.BlockSpec((d, tn), lambda i, j: (0, j))],
        out_specs=pl.BlockSpec((tm, tn), lambda i, j: (i, j)),
        scratch_shapes=[pltpu.VMEM((tm, d), BF16)],
        compiler_params=_params(2, _vmem_limit(blocks, _nbytes((tm, d), BF16), 2 * _nbytes((tm, d), F32))),
        name=name,
    )(x, g, w)


BIAS_VAR_COL0 = KEY_WIN - 2 * V7X_LANES
BIAS_HEAD_GROUP = 4


def _bias_kernel(tab_ref, o_ref, *, mq, off, col_lo, col_hi):
    nvar = KEY_WIN - BIAS_VAR_COL0
    q = lax.broadcasted_iota(jnp.int32, (mq, nvar), 0)
    c = lax.broadcasted_iota(jnp.int32, (mq, nvar), 1) + BIAS_VAR_COL0
    idx = jnp.clip(off + q - c, -REL_CLIP, REL_CLIP) + REL_CLIP
    valid_var = (c >= col_lo) & (c < col_hi)
    c_far = lax.broadcasted_iota(jnp.int32, (mq, BIAS_VAR_COL0), 1)
    valid_far = (c_far >= col_lo) & (c_far < col_hi)
    idx_min = max(0, min(REL_CLIP, off - (KEY_WIN - 1)) + REL_CLIP)

    for h0 in range(0, N_HEADS, BIAS_HEAD_GROUP):
        heads = range(h0, h0 + BIAS_HEAD_GROUP)

        def body(j, accs):
            hit = idx == j
            return tuple(jnp.where(hit, tab_ref[h, j], acc) for h, acc in zip(heads, accs))

        init = tuple(jnp.full((mq, nvar), tab_ref[h, N_REL - 1], F32) for h in heads)
        accs = lax.fori_loop(idx_min, N_REL - 1, body, init)
        for h, acc in zip(heads, accs):
            rows = slice((h % 2) * mq, (h % 2 + 1) * mq)
            far = jnp.full((mq, BIAS_VAR_COL0), tab_ref[h, N_REL - 1], F32)
            o_ref[h // 2, rows, 0:BIAS_VAR_COL0] = jnp.where(valid_far, far, NEG_BIG)
            o_ref[h // 2, rows, BIAS_VAR_COL0:KEY_WIN] = jnp.where(valid_var, acc, NEG_BIG)


def _bias_tiles(table, *, mq, off, col_lo, col_hi, name):
    assert off - (BIAS_VAR_COL0 - 1) >= REL_CLIP
    return pl.pallas_call(
        functools.partial(_bias_kernel, mq=mq, off=off, col_lo=col_lo, col_hi=col_hi),
        out_shape=jax.ShapeDtypeStruct((N_PAIRS, 2 * mq, KEY_WIN), F32),
        in_specs=[pl.BlockSpec(memory_space=pltpu.SMEM)],
        out_specs=pl.BlockSpec(memory_space=pltpu.VMEM),
        name=name,
    )(table)


def _lane_lo(rows):
    return lax.broadcasted_iota(jnp.int32, (rows, PAIR_W), 1) < HEAD_DIM


def _attend(q_pair, k_pair, v_pair, bias_ref, o_put, s_scr, p_scr, *, mq, min_col):
    lo = _lane_lo(mq)
    for p in range(N_PAIRS):
        qs = q_pair(p) * (HEAD_DIM ** -0.5)
        q2 = jnp.concatenate([jnp.where(lo, qs, 0.0), jnp.where(lo, 0.0, qs)], axis=0).astype(BF16)
        s = lax.dot_general(q2, k_pair(p), (((1,), (1,)), ((), ())), preferred_element_type=F32) + bias_ref[p]
        if min_col is not None:
            col = lax.broadcasted_iota(jnp.int32, (1, KEY_WIN), 1)
            s = jnp.where(col >= min_col, s, NEG_BIG)
        s_scr[p] = s
    inv_l = []
    for p in range(N_PAIRS):
        s = s_scr[p]
        e = jnp.exp(s - jnp.max(s, axis=-1, keepdims=True))
        inv_l.append(1.0 / jnp.sum(e, axis=-1, keepdims=True))
        p_scr[p] = e.astype(BF16)
    for p in range(N_PAIRS):
        pv = jnp.dot(p_scr[p], v_pair(p), preferred_element_type=F32) * inv_l[p]
        o_put(p, jnp.where(lo, pv[:mq], pv[mq:]))


def _attn_prompt_kernel(q_ref, kp_ref, kc_ref, vp_ref, vc_ref, bias_ref, o_ref, kbuf, vbuf, s_scr, p_scr):
    i = pl.program_id(0)
    lead = jnp.zeros((CHUNK, ATTN_WIDTH), BF16)
    kbuf[0:CHUNK, :] = lead
    vbuf[0:CHUNK, :] = lead
    kbuf[CHUNK:CHUNK + ATTN_QB, :] = kp_ref[...].astype(BF16)
    vbuf[CHUNK:CHUNK + ATTN_QB, :] = vp_ref[...].astype(BF16)
    kbuf[CHUNK + ATTN_QB:CHUNK + 2 * ATTN_QB, :] = kc_ref[...].astype(BF16)
    vbuf[CHUNK + ATTN_QB:CHUNK + 2 * ATTN_QB, :] = vc_ref[...].astype(BF16)
    chunks_per_step = ATTN_QB // CHUNK
    lead_chunks = N_PAST_CHUNKS + 1

    def chunk_body(j, carry, *, masked):
        r0 = pl.multiple_of(j * CHUNK, CHUNK)
        sl = lambda p: slice(p * PAIR_W, (p + 1) * PAIR_W)
        min_col = (lead_chunks - (i * chunks_per_step + j)) * CHUNK if masked else None

        def o_put(p, o):
            o_ref[pl.ds(r0, CHUNK), sl(p)] = o.astype(BF16)

        _attend(lambda p: q_ref[pl.ds(r0, CHUNK), sl(p)], lambda p: kbuf[pl.ds(r0, KEY_WIN), sl(p)],
                lambda p: vbuf[pl.ds(r0, KEY_WIN), sl(p)], bias_ref, o_put, s_scr, p_scr, mq=CHUNK,
                min_col=min_col)
        return carry

    masked_steps = -(-lead_chunks // chunks_per_step)

    @pl.when(i < masked_steps)
    def _():
        lax.fori_loop(0, chunks_per_step, functools.partial(chunk_body, masked=True), 0)

    @pl.when(i >= masked_steps)
    def _():
        lax.fori_loop(0, chunks_per_step, functools.partial(chunk_body, masked=False), 0)


def _attn_prompt(z_attn, bias):
    t = z_attn.shape[0]
    assert t % ATTN_QB == 0
    blk = (ATTN_QB, ATTN_WIDTH)
    buf_rows = CHUNK + 2 * ATTN_QB
    s_shape = (N_PAIRS, 2 * CHUNK, KEY_WIN)
    blocks = 5 * _nbytes(blk, F32) + _nbytes(bias.shape, F32) + _nbytes(blk, BF16)
    scratch = 2 * _nbytes((buf_rows, ATTN_WIDTH), BF16) + _nbytes(s_shape, F32) + _nbytes(s_shape, BF16)
    return pl.pallas_call(
        _attn_prompt_kernel,
        out_shape=jax.ShapeDtypeStruct((t, ATTN_WIDTH), BF16),
        grid=(t // ATTN_QB,),
        in_specs=[pl.BlockSpec(blk, lambda i: (i, 0)),
                  pl.BlockSpec(blk, lambda i: (jnp.maximum(i - 1, 0), 1)),
                  pl.BlockSpec(blk, lambda i: (i, 1)),
                  pl.BlockSpec(blk, lambda i: (jnp.maximum(i - 1, 0), 2)),
                  pl.BlockSpec(blk, lambda i: (i, 2)),
                  pl.BlockSpec(bias.shape, lambda i: (0, 0, 0))],
        out_specs=pl.BlockSpec(blk, lambda i: (i, 0)),
        scratch_shapes=[pltpu.VMEM((buf_rows, ATTN_WIDTH), BF16), pltpu.VMEM((buf_rows, ATTN_WIDTH), BF16),
                        pltpu.VMEM(s_shape, F32), pltpu.VMEM(s_shape, BF16)],
        compiler_params=_params(1, _vmem_limit(blocks, scratch, 4 * _nbytes((2 * CHUNK, KEY_WIN), F32))),
        name="attn_prompt",
    )(z_attn, z_attn, z_attn, z_attn, z_attn, bias)


def _attn_sample_kernel(q_ref, kn_ref, vn_ref, kc_ref, vc_ref, bias_ref, o_ref, kbuf, vbuf, s_scr, p_scr, *,
                        s, past):
    tail = jnp.zeros((KEY_WIN - past - s, ATTN_WIDTH), BF16)
    kbuf[0:past, :] = kc_ref[...].astype(BF16)
    vbuf[0:past, :] = vc_ref[...].astype(BF16)
    kbuf[past:past + s, :] = kn_ref[...].astype(BF16)
    vbuf[past:past + s, :] = vn_ref[...].astype(BF16)
    kbuf[past + s:KEY_WIN, :] = tail
    vbuf[past + s:KEY_WIN, :] = tail
    sl = lambda p: slice(p * PAIR_W, (p + 1) * PAIR_W)

    def o_put(p, o):
        o_ref[:, sl(p)] = o.astype(BF16)

    _attend(lambda p: q_ref[:, sl(p)], lambda p: kbuf[:, sl(p)], lambda p: vbuf[:, sl(p)], bias_ref, o_put,
            s_scr, p_scr, mq=s, min_col=None)


def _attn_sample(z_attn, k_cache, v_cache, bias, *, nb, s):
    past = k_cache.shape[1]
    assert past + s <= KEY_WIN and z_attn.shape[0] == nb * s
    blk = (s, ATTN_WIDTH)
    cblk = (None, past, ATTN_WIDTH)
    s_shape = (N_PAIRS, 2 * s, KEY_WIN)
    blocks = (3 * _nbytes(blk, F32) + 2 * _nbytes((past, ATTN_WIDTH), F32) + _nbytes(bias.shape, F32)
              + _nbytes(blk, BF16))
    scratch = 2 * _nbytes((KEY_WIN, ATTN_WIDTH), BF16) + _nbytes(s_shape, F32) + _nbytes(s_shape, BF16)
    return pl.pallas_call(
        functools.partial(_attn_sample_kernel, s=s, past=past),
        out_shape=jax.ShapeDtypeStruct((nb * s, ATTN_WIDTH), BF16),
        grid=(nb,),
        in_specs=[pl.BlockSpec(blk, lambda b: (b, 0)),
                  pl.BlockSpec(blk, lambda b: (b, 1)),
                  pl.BlockSpec(blk, lambda b: (b, 2)),
                  pl.BlockSpec(cblk, lambda b: (b, 0, 0)),
                  pl.BlockSpec(cblk, lambda b: (b, 0, 0)),
                  pl.BlockSpec(bias.shape, lambda b: (0, 0, 0))],
        out_specs=pl.BlockSpec(blk, lambda b: (b, 0)),
        scratch_shapes=[pltpu.VMEM((KEY_WIN, ATTN_WIDTH), BF16), pltpu.VMEM((KEY_WIN, ATTN_WIDTH), BF16),
                        pltpu.VMEM(s_shape, F32), pltpu.VMEM(s_shape, BF16)],
        compiler_params=_params(1, _vmem_limit(blocks, scratch, 4 * _nbytes((2 * s, KEY_WIN), F32))),
        name="attn_sample",
    )(z_attn, z_attn, z_attn, k_cache, v_cache, bias)


def _bf(x):
    return x.astype(BF16)


def _split2(x):
    hi = _bf(x)
    return hi, _bf(x - hi.astype(F32))


_NN = (((1,), (0,)), ((), ()))
_NT = (((1,), (1,)), ((), ()))


def _mm(x, y, dims=_NN):
    return lax.dot_general(x, y, dims, preferred_element_type=F32)


def _bd(y, lo):
    zero = jnp.zeros_like(y)
    return jnp.concatenate([jnp.where(lo, y, zero), jnp.where(lo, zero, y)], axis=0)


def _seg_sum(x, ones_seg, *, passes):
    r = x.shape[0]
    half = x.shape[1] // (2 * PAIR_W)
    slab = lambda p: x[:, p * PAIR_W:(p + 1) * PAIR_W]
    xx = jnp.concatenate([jnp.concatenate([slab(p) for p in range(half)], axis=0),
                          jnp.concatenate([slab(half + p) for p in range(half)], axis=0)], axis=1)
    hi, lo = _split2(xx)
    s = _mm(hi, ones_seg)
    if passes == 2:
        s = s + _mm(lo, ones_seg)
    return jnp.concatenate([s[p * r:(p + 1) * r, 0:PAIR_W] for p in range(half)]
                           + [s[p * r:(p + 1) * r, PAIR_W:2 * PAIR_W] for p in range(half)], axis=1)


def _sigmoid(x):
    return 1.0 / (1.0 + jnp.exp(-x))


def _rwkv_kernel(z_ref, sh_ref, s0_ref, mu_ref, w0_ref, w2cat_ref, a0_ref, gw2_ref, kk_ref, ka_ref,
                 rk_ref, lnw_ref, lnb_ref, o_ref, sfin_ref, s_scr, carry_scr, y_scr, *, rows, cps):
    ct = CHUNK_T
    pw = PAIR_W
    tt = cps * ct
    assert rows == ct or cps == 1
    c = pl.program_id(1)

    @pl.when(c == 0)
    def _():
        s_scr[...] = s0_ref[...]
        carry_scr[0:1, :] = sh_ref[...]

    z = z_ref[...]
    if rows < ct:
        z = jnp.concatenate([z, jnp.zeros((ct - rows, SHIFT_W), F32)], axis=0)
    row_all = lax.broadcasted_iota(jnp.int32, (tt, 1), 0)
    zprev = jnp.where(row_all == 0, carry_scr[0:1, :], pltpu.roll(z, 1, axis=0))
    last = (cps - 1) * ct + rows - 1
    carry_scr[0:1, :] = z[last:last + 1, :]
    zs = z + (zprev - z) * mu_ref[...]

    r = zs[:, 0:RWKV_WIDTH]
    k = zs[:, RWKV_WIDTH:2 * RWKV_WIDTH]
    v = zs[:, 2 * RWKV_WIDTH:3 * RWKV_WIDTH]
    zwa = zs[:, 3 * RWKV_WIDTH:3 * RWKV_WIDTH + DECAY_RANK + AAA_RANK]
    zg = zs[:, 3 * RWKV_WIDTH + DECAY_RANK + AAA_RANK:SHIFT_W]

    lo_tt = _lane_lo(tt)
    lora_in = jnp.concatenate([jnp.where(lo_tt, jnp.tanh(zwa), 0.0), jnp.where(lo_tt, 0.0, zwa)], axis=0)
    lora = _mm(_bf(lora_in), w2cat_ref[...])
    lw = -math.exp(-0.5) * _sigmoid(w0_ref[...] + lora[:tt])
    a = _sigmoid(a0_ref[...] + lora[tt:])
    g = _mm(_bf(_sigmoid(zg)), gw2_ref[...])

    seg = 2 * pw
    ones_seg = _bf(lax.broadcasted_iota(jnp.int32, (seg, seg), 0) // HEAD_DIM
                   == lax.broadcasted_iota(jnp.int32, (seg, seg), 1) // HEAD_DIM)

    kkr = k * kk_ref[...]
    kn = kkr * lax.rsqrt(jnp.maximum(_seg_sum(kkr * kkr, ones_seg, passes=1), 1e-24))
    k2 = k * (1.0 + (a - 1.0) * ka_ref[...])
    bonus = _seg_sum(r * k2 * rk_ref[...], ones_seg, passes=1) * v

    row = lax.broadcasted_iota(jnp.int32, (ct, 1), 0)
    if rows < ct:
        live = row < rows
        lw = jnp.where(live, lw, 0.0)
    tri = _bf(lax.broadcasted_iota(jnp.int32, (ct, ct), 1) <= lax.broadcasted_iota(jnp.int32, (ct, ct), 0))
    l_hi, l_lo = _split2(lw)
    rows_of = lambda xv, gi: xv[gi * ct:(gi + 1) * ct]
    cl = jnp.concatenate([_mm(tri, rows_of(l_hi, gi)) + _mm(tri, rows_of(l_lo, gi)) for gi in range(cps)], axis=0)
    e_in = jnp.exp(cl)
    e_ex = jnp.exp(cl - lw)
    e_inv = jnp.exp(-cl)
    at = -kn * e_ex
    bt = kn * a * e_inv
    kt = k2 * e_inv
    if rows < ct:
        bt = jnp.where(live, bt, 0.0)
        kt = jnp.where(live, kt, 0.0)
    rt = r * e_in
    pcs = [e_in[gi * ct + ct - 1:(gi + 1) * ct, :] for gi in range(cps)]
    pc_rows = jnp.concatenate([jnp.broadcast_to(pcs[gi], (ct, RWKV_WIDTH)) for gi in range(cps)], axis=0)
    bp = bt * pc_rows
    kp = kt * pc_rows

    t_lane = lax.broadcasted_iota(jnp.int32, (ct, 2 * ct), 1) % ct
    strict = t_lane < row
    incl = t_lane <= row
    lo_t = _lane_lo(ct)
    lo_p = lax.broadcasted_iota(jnp.int32, (ct, 2 * ct), 1) < ct
    lo_t2 = lax.broadcasted_iota(jnp.int32, (ct, 2 * pw), 1) % pw < HEAD_DIM
    diag = (lax.broadcasted_iota(jnp.int32, (HEAD_DIM, pw), 1) % HEAD_DIM
            == lax.broadcasted_iota(jnp.int32, (HEAD_DIM, pw), 0))
    lo_s = _lane_lo(HEAD_DIM)
    n_dbl = int(math.log2(ct))
    assert 1 << n_dbl == ct and 2 * ct == pw
    pairs = range(N_PAIRS)
    units = [(gi, p) for gi in range(cps) for p in pairs]
    uid = range(len(units))
    cat0 = lambda xs: jnp.concatenate(xs, axis=0)
    cat1 = lambda xs: jnp.concatenate(xs, axis=1)
    tok = lambda xv: [xv[gi * ct:(gi + 1) * ct, p * pw:(p + 1) * pw] for gi, p in units]

    def tn(xv, yv):
        zz = _mm(_bf(xv.T), yv)
        return jnp.where(lo_s, zz[:HEAD_DIM], zz[HEAD_DIM:])

    a_, r_, v_ = tok(at), tok(rt), tok(v)
    ab_, bb_, kb_, rb_, vb_ = tok(_bf(at)), tok(_bf(bt)), tok(_bf(kt)), tok(_bf(rt)), tok(_bf(v))
    bpb_, kpb_ = tok(_bf(bp)), tok(_bf(kp))
    v_bd = [_bd(vb_[u], lo_t) for u in uid]
    amat = [_mm(cat0([ab_[u], rb_[u]]), cat0([_bd(bb_[u], lo_t), _bd(kb_[u], lo_t)]), _NT) for u in uid]
    a_ab = [jnp.where(strict, amat[u][:ct, :pw], 0.0) for u in uid]
    a_ak = [_bf(jnp.where(strict, amat[u][:ct, pw:], 0.0)) for u in uid]
    a_rb = [_bf(jnp.where(incl, amat[u][ct:, :pw], 0.0)) for u in uid]
    a_rk = [_bf(jnp.where(incl, amat[u][ct:, pw:], 0.0)) for u in uid]
    x = [cat1([a_[u], _mm(a_ak[u], v_bd[u])]) for u in uid]
    npow = [_bf(a_ab[u]) for u in uid]
    for it in range(n_dbl):
        if it + 1 < n_dbl:
            res = [_mm(npow[u], cat1([_bd(_bf(x[u]), lo_t2), _bd(npow[u], lo_p)])) for u in uid]
            npow = [_bf(res[u][:, 2 * pw:]) for u in uid]
        else:
            res = [_mm(npow[u], _bd(_bf(x[u]), lo_t2)) for u in uid]
        x = [x[u] + res[u][:, :2 * pw] for u in uid]
    ry = [_mm(a_rb[u], _bd(_bf(x[u]), lo_t2)) for u in uid]
    rp = [_bf(r_[u] + ry[u][:, :pw]) for u in uid]
    y0 = [ry[u][:, pw:] + _mm(a_rk[u], v_bd[u]) for u in uid]
    m_mat = [_bf(jnp.where(diag, pcs[gi][:, p * pw:(p + 1) * pw], 0.0) + tn(x[u][:, :pw], bpb_[u]))
             for u, (gi, p) in enumerate(units)]
    n_mat = [tn(cat0([x[u][:, pw:], v_[u]]), cat0([bpb_[u], kpb_[u]])) for u in uid]
    state = [s_scr[:, p * pw:(p + 1) * pw] for p in pairs]
    for u, (gi, p) in enumerate(units):
        s_old = _bf(state[p])
        y_scr[gi * ct:(gi + 1) * ct, p * pw:(p + 1) * pw] = _mm(rp[u], _bd(s_old, lo_s), _NT) + y0[u]
        state[p] = _mm(s_old, _bd(m_mat[u], lo_s)) + n_mat[u]
    for p in pairs:
        s_scr[:, p * pw:(p + 1) * pw] = state[p]

    y = y_scr[...]
    mean = _seg_sum(y, ones_seg, passes=2) * (1.0 / HEAD_DIM)
    dv = y - mean
    var = _seg_sum(dv * dv, ones_seg, passes=1) * (1.0 / HEAD_DIM)
    yn = dv * lax.rsqrt(var + GN_EPS) * lnw_ref[...] + lnb_ref[...]
    out = (yn + bonus) * g
    o_ref[...] = out[:(cps - 1) * ct + rows].astype(BF16)

    @pl.when(c == pl.num_programs(1) - 1)
    def _():
        sfin_ref[...] = s_scr[...]


def _rwkv(z_rwkv, shift0, s0_packed, lp, *, nb, seq, name):
    rows = min(seq, CHUNK_T)
    cps = min(seq // rows, RWKV_CHUNKS_PER_STEP)
    blk_rows = cps * rows
    assert seq % blk_rows == 0 and z_rwkv.shape == (nb * seq, SHIFT_W)
    nck = seq // blk_rows
    tt = cps * CHUNK_T
    row1 = lambda n: pl.BlockSpec((1, n), lambda b, c: (0, 0))
    full = lambda r, n: pl.BlockSpec((r, n), lambda b, c: (0, 0))
    st_blk = pl.BlockSpec((None, HEAD_DIM, RWKV_WIDTH), lambda b, c: (b, 0, 0))
    lora_rows = DECAY_RANK + AAA_RANK
    blocks = (_nbytes((blk_rows, SHIFT_W), F32) + 2 * _nbytes((HEAD_DIM, RWKV_WIDTH), F32)
              + _nbytes((lora_rows + GATE_RANK, RWKV_WIDTH), BF16) + _nbytes((16, SHIFT_W), F32)
              + _nbytes((blk_rows, RWKV_WIDTH), BF16))
    scratch = (_nbytes((HEAD_DIM, RWKV_WIDTH), F32) + _nbytes((8, SHIFT_W), F32)
               + _nbytes((tt, RWKV_WIDTH), F32))
    out, s_fin = pl.pallas_call(
        functools.partial(_rwkv_kernel, rows=rows, cps=cps),
        out_shape=(jax.ShapeDtypeStruct((nb * seq, RWKV_WIDTH), BF16),
                   jax.ShapeDtypeStruct((nb, HEAD_DIM, RWKV_WIDTH), F32)),
        grid=(nb, nck),
        in_specs=[pl.BlockSpec((blk_rows, SHIFT_W), lambda b, c: (b * nck + c, 0)),
                  pl.BlockSpec((None, 1, SHIFT_W), lambda b, c: (b, 0, 0)),
                  st_blk,
                  row1(SHIFT_W), row1(RWKV_WIDTH), full(lora_rows, RWKV_WIDTH), row1(RWKV_WIDTH),
                  full(GATE_RANK, RWKV_WIDTH), row1(RWKV_WIDTH),
                  row1(RWKV_WIDTH), row1(RWKV_WIDTH), row1(RWKV_WIDTH), row1(RWKV_WIDTH)],
        out_specs=(pl.BlockSpec((blk_rows, RWKV_WIDTH), lambda b, c: (b * nck + c, 0)), st_blk),
        scratch_shapes=[pltpu.VMEM((HEAD_DIM, RWKV_WIDTH), F32), pltpu.VMEM((8, SHIFT_W), F32),
                        pltpu.VMEM((tt, RWKV_WIDTH), F32)],
        compiler_params=_params(2, _vmem_limit(blocks, scratch, 64 * _nbytes((tt, RWKV_WIDTH), F32))),
        name=name,
    )(z_rwkv, shift0, s0_packed, lp["mu"], lp["w0"], lp["w2cat"], lp["a0"], lp["g_w2"],
      lp["k_k"], lp["k_a"], lp["r_k"], lp["lnx_w"], lp["lnx_b"])
    return out, s_fin


def _oproj_kernel(a_ref, r_ref, wa_ref, wr_ref, x_ref, g_ref, h_ref, n_ref):
    h = (x_ref[...] + jnp.dot(a_ref[...], wa_ref[...], preferred_element_type=F32)
         + jnp.dot(r_ref[...], wr_ref[...], preferred_element_type=F32))
    h_ref[...] = h
    n_ref[...] = _rms(h, g_ref[...]).astype(BF16)


def _oproj(attn, rwkv, w_o, x, g, *, tm, name):
    t = x.shape[0]
    assert t % tm == 0
    half = (ATTN_WIDTH, D_MODEL)
    blocks = (2 * _nbytes((tm, ATTN_WIDTH), BF16) + 2 * _nbytes(half, BF16) + 2 * _nbytes((tm, D_MODEL), F32)
              + _nbytes((tm, D_MODEL), BF16))
    return pl.pallas_call(
        _oproj_kernel,
        out_shape=(jax.ShapeDtypeStruct((t, D_MODEL), F32), jax.ShapeDtypeStruct((t, D_MODEL), BF16)),
        grid=(t // tm,),
        in_specs=[pl.BlockSpec((tm, ATTN_WIDTH), lambda i: (i, 0)),
                  pl.BlockSpec((tm, RWKV_WIDTH), lambda i: (i, 0)),
                  pl.BlockSpec(half, lambda i: (0, 0)),
                  pl.BlockSpec(half, lambda i: (1, 0)),
                  pl.BlockSpec((tm, D_MODEL), lambda i: (i, 0)),
                  pl.BlockSpec((1, D_MODEL), lambda i: (0, 0))],
        out_specs=(pl.BlockSpec((tm, D_MODEL), lambda i: (i, 0)), pl.BlockSpec((tm, D_MODEL), lambda i: (i, 0))),
        compiler_params=_params(1, _vmem_limit(blocks, 0, 2 * _nbytes((tm, D_MODEL), F32))),
        name=name,
    )(attn, rwkv, w_o, w_o, x, g)


def _ffn_kernel(n_ref, h_ref, wg_ref, wu_ref, wd_ref, g_ref, o_ref, acc_ref):
    j = pl.program_id(1)

    @pl.when(j == 0)
    def _():
        acc_ref[...] = h_ref[...]

    n = n_ref[...]
    gate = jnp.dot(n, wg_ref[...], preferred_element_type=F32)
    up = jnp.dot(n, wu_ref[...], preferred_element_type=F32)
    act = (gate * _sigmoid(gate) * up).astype(BF16)
    acc_ref[...] += jnp.dot(act, wd_ref[...], preferred_element_type=F32)

    @pl.when(j == pl.num_programs(1) - 1)
    def _():
        o_ref[...] = _rms(acc_ref[...], g_ref[...])


def _ffn(n2, h, wg, wu, wd, g, *, tm, tf, name):
    t = h.shape[0]
    assert t % tm == 0 and D_FF % tf == 0
    blocks = (_nbytes((tm, D_MODEL), BF16) + 2 * _nbytes((tm, D_MODEL), F32) + 3 * _nbytes((D_MODEL, tf), BF16))
    return pl.pallas_call(
        _ffn_kernel,
        out_shape=jax.ShapeDtypeStruct((t, D_MODEL), F32),
        grid=(t // tm, D_FF // tf),
        in_specs=[pl.BlockSpec((tm, D_MODEL), lambda i, j: (i, 0)),
                  pl.BlockSpec((tm, D_MODEL), lambda i, j: (i, 0)),
                  pl.BlockSpec((D_MODEL, tf), lambda i, j: (0, j)),
                  pl.BlockSpec((D_MODEL, tf), lambda i, j: (0, j)),
                  pl.BlockSpec((tf, D_MODEL), lambda i, j: (j, 0)),
                  pl.BlockSpec((1, D_MODEL), lambda i, j: (0, 0))],
        out_specs=pl.BlockSpec((tm, D_MODEL), lambda i, j: (i, 0)),
        scratch_shapes=[pltpu.VMEM((tm, D_MODEL), F32)],
        compiler_params=_params(2, _vmem_limit(blocks, _nbytes((tm, D_MODEL), F32), 4 * _nbytes((tm, tf), F32))),
        name=name,
    )(n2, h, wg, wu, wd, g)


def _pack_state(s):
    nb = s.shape[0]
    return jnp.transpose(s, (0, 2, 1, 3)).reshape(nb, HEAD_DIM, RWKV_WIDTH)


def _unpack_state(s):
    nb = s.shape[0]
    return jnp.transpose(s.reshape(nb, HEAD_DIM, N_HEADS, HEAD_DIM), (0, 2, 1, 3))


def _trunk(x, lp, wts, attn_fn, shift0, s0, *, nb, seq, tag):
    t = nb * seq
    tm = min(t, 1024)
    z_attn = _inproj(x, lp["ln1"], wts["w_qkv"], tm=tm, tn=1024, name=f"inproj_attn_{tag}")
    z_rwkv = _inproj(x, lp["ln1"], wts["w_rwkv"], tm=tm, tn=SHIFT_W // 2, name=f"inproj_rwkv_{tag}")
    attn = attn_fn(z_attn)
    rwkv, s_fin = _rwkv(z_rwkv, shift0, _pack_state(s0), lp, nb=nb, seq=seq, name=f"rwkv_{tag}")
    tm2 = min(t, 512)
    h, n2 = _oproj(attn, rwkv, wts["w_o"], x, lp["ln2"], tm=tm2, name=f"oproj_{tag}")
    y = _ffn(n2, h, wts["w_gate"], wts["w_up"], wts["w_down"], lp["final_norm"], tm=tm2, tf=512, name=f"ffn_{tag}")
    return y, z_attn, z_rwkv, _unpack_state(s_fin)


def kernel(x_prompt, x_sample, cache_attn_k, cache_attn_v, state_rwkv_wkv, state_rwkv_shift, ln1, w_in, rel_table,
           mu, w0, w_w2, a0, a_w2, g_w2, k_k, k_a, r_k, lnx_w, lnx_b, w_o, ln2, w_gate, w_up, w_down, final_norm):
    depth = w_in.shape[0]
    assert depth == 1, "final RMSNorm is fused into the (single) layer's FFN call"
    b, t, _ = x_prompt.shape
    bd, s, _ = x_sample.shape
    past = cache_attn_k.shape[2]
    n_keep = min(BAND_PAST, t)
    l = 0
    row = lambda p: p[l].reshape(1, -1)
    lp = dict(ln1=row(ln1), mu=row(mu), w0=row(w0), a0=row(a0),
              w2cat=jnp.concatenate([w_w2[l], a_w2[l]], axis=0).astype(BF16), g_w2=g_w2[l].astype(BF16),
              k_k=row(k_k), k_a=row(k_a), r_k=row(r_k), lnx_w=row(lnx_w), lnx_b=row(lnx_b), ln2=row(ln2),
              final_norm=final_norm.reshape(1, -1))
    wts = dict(w_qkv=w_in[l][:, :3 * ATTN_WIDTH].astype(BF16), w_rwkv=w_in[l][:, 3 * ATTN_WIDTH:].astype(BF16),
               w_o=w_o[l].astype(BF16), w_gate=w_gate[l].astype(BF16), w_up=w_up[l].astype(BF16),
               w_down=w_down[l].astype(BF16))
    table = rel_table[l]

    bias_p = _bias_tiles(table, mq=CHUNK, off=BAND_PAST + CHUNK, col_lo=CHUNK, col_hi=KEY_WIN, name="bias_prompt")
    bias_s = _bias_tiles(table, mq=s, off=past, col_lo=0, col_hi=past + s, name="bias_sample")

    assert b == 1
    yp, zap, zrp, wkv_p = _trunk(
        x_prompt.reshape(b * t, D_MODEL), lp, wts, lambda z: _attn_prompt(z, bias_p),
        jnp.zeros((b, 1, SHIFT_W), F32), jnp.zeros((b, N_HEADS, HEAD_DIM, HEAD_DIM), F32), nb=b, seq=t, tag="p")
    kc = cache_attn_k[l].reshape(bd, past, ATTN_WIDTH)
    vc = cache_attn_v[l].reshape(bd, past, ATTN_WIDTH)
    ys, zas, zrs, wkv_s = _trunk(
        x_sample.reshape(bd * s, D_MODEL), lp, wts, lambda z: _attn_sample(z, kc, vc, bias_s, nb=bd, s=s),
        state_rwkv_shift[l], state_rwkv_wkv[l], nb=bd, seq=s, tag="s")

    heads = lambda z, n, tt: z.reshape(n, tt, N_HEADS, HEAD_DIM)
    zap = zap.reshape(b, t, 3 * ATTN_WIDTH)
    zas = zas.reshape(bd, s, 3 * ATTN_WIDTH)
    k_p = heads(zap[:, t - n_keep:, ATTN_WIDTH:2 * ATTN_WIDTH], b, n_keep)
    v_p = heads(zap[:, t - n_keep:, 2 * ATTN_WIDTH:], b, n_keep)
    k_s = heads(zas[:, :, ATTN_WIDTH:2 * ATTN_WIDTH], bd, s)
    v_s = heads(zas[:, :, 2 * ATTN_WIDTH:], bd, s)
    sh_p = zrp.reshape(b, t, SHIFT_W)[:, t - 1:]
    sh_s = zrs.reshape(bd, s, SHIFT_W)[:, s - 1:]
    return (yp.reshape(b, t, D_MODEL), ys.reshape(bd, s, D_MODEL),
            k_p[None], v_p[None], wkv_p[None], sh_p[None],
            k_s[None], v_s[None], wkv_s[None], sh_s[None])
```

```python
import functools
import math

import jax
import jax.numpy as jnp
from jax import lax
from jax.experimental import pallas as pl
from jax.experimental.pallas import tpu as pltpu

F32 = jnp.float32
BF16 = jnp.bfloat16

D_MODEL = 2048
CHUNK = 64
N_PAST_CHUNKS = 8
BAND_PAST = N_PAST_CHUNKS * CHUNK
ATTN_WIDTH = 1024
HEAD_DIM = 64
N_HEADS = 16
REL_CLIP = 128
N_REL = 2 * REL_CLIP + 1
RWKV_WIDTH = 1024
DECAY_RANK = 64
AAA_RANK = 64
GATE_RANK = 128
SHIFT_W = 3 * RWKV_WIDTH + DECAY_RANK + AAA_RANK + GATE_RANK
D_FF = 5632
RMS_EPS = 1e-6
GN_EPS = 64e-5

V7X_LANES = 128
V7X_VMEM_BYTES = 64 * 1024 * 1024
V7X_SCOPED_VMEM_CAP_BYTES = 60000 * 1024

PAIR_W = 2 * HEAD_DIM
N_PAIRS = N_HEADS // 2
CHUNK_T = 64
KEY_WIN = (N_PAST_CHUNKS + 2) * CHUNK
NEG_BIG = -1e30
ATTN_QB = 512
ATTN_CHUNKS_PER_ITER = 2
ATTN_SOFTMAX_ROWS = 128
RWKV_CHUNKS_PER_STEP = 2

assert PAIR_W == V7X_LANES


def _vmem_limit(pipelined_bytes, scratch_bytes, temp_bytes):
    need = 2 * pipelined_bytes + scratch_bytes + temp_bytes
    return int(min(max(need, 16 * 1024 * 1024), V7X_SCOPED_VMEM_CAP_BYTES))


def _nbytes(shape, dtype):
    return math.prod(shape) * jnp.dtype(dtype).itemsize


def _params(n_grid, vmem):
    return pltpu.CompilerParams(dimension_semantics=("arbitrary",) * n_grid, vmem_limit_bytes=vmem)


def _rms(x, g):
    return x * lax.rsqrt(jnp.mean(x * x, axis=-1, keepdims=True) + RMS_EPS) * g


def _inproj_kernel(x_ref, g_ref, w_ref, o_ref, n_ref):
    @pl.when(pl.program_id(1) == 0)
    def _():
        n_ref[...] = _rms(x_ref[...], g_ref[...]).astype(BF16)

    o_ref[...] = jnp.dot(n_ref[...], w_ref[...], preferred_element_type=F32)


def _inproj(x, g, w, *, col0, n, tm, tn, name):
    t, d = x.shape
    assert t % tm == 0 and n % tn == 0 and col0 % V7X_LANES == 0 and col0 + n <= w.shape[1]
    blocks = _nbytes((tm, d), F32) + _nbytes((d, tn), BF16) + _nbytes((tm, tn), F32)
    return pl.pallas_call(
        _inproj_kernel,
        out_shape=jax.ShapeDtypeStruct((t, n), F32),
        grid=(t // tm, n // tn),
        in_specs=[pl.BlockSpec((tm, d), lambda i, j: (i, 0)),
                  pl.BlockSpec((1, d), lambda i, j: (0, 0)),
                  pl.BlockSpec((pl.Element(d), pl.Element(tn)),
                               lambda i, j: (0, (col0 // V7X_LANES + j * (tn // V7X_LANES)) * V7X_LANES))],
        out_specs=pl.BlockSpec((tm, tn), lambda i, j: (i, j)),
        scratch_shapes=[pltpu.VMEM((tm, d), BF16)],
        compiler_params=_params(2, _vmem_limit(blocks, _nbytes((tm, d), BF16), 2 * _nbytes((tm, d), F32))),
        name=name,
    )(x, g, w)


BIAS_VAR_COL0 = KEY_WIN - 2 * V7X_LANES
BIAS_HEAD_GROUP = 4


def _bias_kernel(tab_ref, o_ref, *, mq, off, col_lo, col_hi):
    nvar = KEY_WIN - BIAS_VAR_COL0
    q = lax.broadcasted_iota(jnp.int32, (mq, nvar), 0)
    c = lax.broadcasted_iota(jnp.int32, (mq, nvar), 1) + BIAS_VAR_COL0
    idx = jnp.clip(off + q - c, -REL_CLIP, REL_CLIP) + REL_CLIP
    valid_var = (c >= col_lo) & (c < col_hi)
    c_far = lax.broadcasted_iota(jnp.int32, (mq, BIAS_VAR_COL0), 1)
    valid_far = (c_far >= col_lo) & (c_far < col_hi)
    idx_min = max(0, min(REL_CLIP, off - (KEY_WIN - 1)) + REL_CLIP)

    for h0 in range(0, N_HEADS, BIAS_HEAD_GROUP):
        heads = range(h0, h0 + BIAS_HEAD_GROUP)

        def body(j, accs):
            hit = idx == j
            return tuple(jnp.where(hit, tab_ref[h, j], acc) for h, acc in zip(heads, accs))

        init = tuple(jnp.full((mq, nvar), tab_ref[h, N_REL - 1], F32) for h in heads)
        accs = lax.fori_loop(idx_min, N_REL - 1, body, init)
        for h, acc in zip(heads, accs):
            rows = slice((h % 2) * mq, (h % 2 + 1) * mq)
            far = jnp.full((mq, BIAS_VAR_COL0), tab_ref[h, N_REL - 1], F32)
            o_ref[h // 2, rows, 0:BIAS_VAR_COL0] = jnp.where(valid_far, far, NEG_BIG)
            o_ref[h // 2, rows, BIAS_VAR_COL0:KEY_WIN] = jnp.where(valid_var, acc, NEG_BIG)


def _bias_tiles(table, *, mq, off, col_lo, col_hi, name):
    assert off - (BIAS_VAR_COL0 - 1) >= REL_CLIP
    return pl.pallas_call(
        functools.partial(_bias_kernel, mq=mq, off=off, col_lo=col_lo, col_hi=col_hi),
        out_shape=jax.ShapeDtypeStruct((N_PAIRS, 2 * mq, KEY_WIN), F32),
        in_specs=[pl.BlockSpec(memory_space=pltpu.SMEM)],
        out_specs=pl.BlockSpec(memory_space=pltpu.VMEM),
        name=name,
    )(table)


def _lane_lo(rows):
    return lax.broadcasted_iota(jnp.int32, (rows, PAIR_W), 1) < HEAD_DIM


def _attend(q_pair, k_pair, v_pair, bias_ref, o_put, s_scr, p_scr, *, mq, n_win, min_col):
    lo = _lane_lo(mq)
    units = [(w, p) for w in range(n_win) for p in range(N_PAIRS)]
    for u, (w, p) in enumerate(units):
        qs = q_pair(w, p) * (HEAD_DIM ** -0.5)
        q2 = jnp.concatenate([jnp.where(lo, qs, 0.0), jnp.where(lo, 0.0, qs)], axis=0).astype(BF16)
        s = lax.dot_general(q2, k_pair(w, p), (((1,), (1,)), ((), ())), preferred_element_type=F32) + bias_ref[p]
        first = min_col(w)
        if first is not None:
            col = lax.broadcasted_iota(jnp.int32, (1, KEY_WIN), 1)
            s = jnp.where(col >= first, s, NEG_BIG)
        s_scr[u] = s
    inv_l = []
    rb = min(2 * mq, ATTN_SOFTMAX_ROWS)
    for u in range(len(units)):
        parts = []
        for r0 in range(0, 2 * mq, rb):
            s = s_scr[u, r0:r0 + rb, :]
            e = jnp.exp(s - jnp.max(s, axis=-1, keepdims=True))
            parts.append(1.0 / jnp.sum(e, axis=-1, keepdims=True))
            p_scr[u, r0:r0 + rb, :] = e.astype(BF16)
        inv_l.append(jnp.concatenate(parts, axis=0))
    for u, (w, p) in enumerate(units):
        pv = jnp.dot(p_scr[u], v_pair(w, p), preferred_element_type=F32) * inv_l[u]
        o_put(w, p, jnp.where(lo, pv[:mq], pv[mq:]))


def _attn_prompt_kernel(q_ref, kc_ref, vc_ref, bias_ref, o_ref, kbuf, vbuf, s_scr, p_scr):
    i = pl.program_id(0)
    prev0, cur0, end = CHUNK, CHUNK + ATTN_QB, CHUNK + 2 * ATTN_QB

    @pl.when(i == 0)
    def _():
        zeros = jnp.zeros((cur0, ATTN_WIDTH), BF16)
        kbuf[0:cur0, :] = zeros
        vbuf[0:cur0, :] = zeros

    kbuf[cur0:end, :] = kc_ref[...].astype(BF16)
    vbuf[cur0:end, :] = vc_ref[...].astype(BF16)
    chunks_per_step = ATTN_QB // CHUNK
    lead_chunks = N_PAST_CHUNKS + 1
    n_win = ATTN_CHUNKS_PER_ITER

    def iter_body(j, carry, *, masked):
        sl = lambda p: slice(p * PAIR_W, (p + 1) * PAIR_W)
        r0 = lambda w: pl.multiple_of((j * n_win + w) * CHUNK, CHUNK)
        min_col = lambda w: (lead_chunks - (i * chunks_per_step + j * n_win + w)) * CHUNK if masked else None

        def o_put(w, p, o):
            o_ref[pl.ds(r0(w), CHUNK), sl(p)] = o.astype(BF16)

        _attend(lambda w, p: q_ref[pl.ds(r0(w), CHUNK), sl(p)], lambda w, p: kbuf[pl.ds(r0(w), KEY_WIN), sl(p)],
                lambda w, p: vbuf[pl.ds(r0(w), KEY_WIN), sl(p)], bias_ref, o_put, s_scr, p_scr, mq=CHUNK,
                n_win=n_win, min_col=min_col)
        return carry

    masked_steps = -(-lead_chunks // chunks_per_step)

    @pl.when(i < masked_steps)
    def _():
        lax.fori_loop(0, chunks_per_step // n_win, functools.partial(iter_body, masked=True), 0)

    @pl.when(i >= masked_steps)
    def _():
        lax.fori_loop(0, chunks_per_step // n_win, functools.partial(iter_body, masked=False), 0)

    kbuf[prev0:cur0, :] = kbuf[cur0:end, :]
    vbuf[prev0:cur0, :] = vbuf[cur0:end, :]


def _attn_prompt(z_attn, bias):
    t = z_attn.shape[0]
    assert t % ATTN_QB == 0 and (ATTN_QB // CHUNK) % ATTN_CHUNKS_PER_ITER == 0 and ATTN_QB >= BAND_PAST
    blk = (ATTN_QB, ATTN_WIDTH)
    buf_rows = CHUNK + 2 * ATTN_QB
    s_shape = (ATTN_CHUNKS_PER_ITER * N_PAIRS, 2 * CHUNK, KEY_WIN)
    blocks = 3 * _nbytes(blk, F32) + _nbytes(bias.shape, F32) + _nbytes(blk, BF16)
    scratch = 2 * _nbytes((buf_rows, ATTN_WIDTH), BF16) + _nbytes(s_shape, F32) + _nbytes(s_shape, BF16)
    return pl.pallas_call(
        _attn_prompt_kernel,
        out_shape=jax.ShapeDtypeStruct((t, ATTN_WIDTH), BF16),
        grid=(t // ATTN_QB,),
        in_specs=[pl.BlockSpec(blk, lambda i: (i, 0)),
                  pl.BlockSpec(blk, lambda i: (i, 1)),
                  pl.BlockSpec(blk, lambda i: (i, 2)),
                  pl.BlockSpec(bias.shape, lambda i: (0, 0, 0))],
        out_specs=pl.BlockSpec(blk, lambda i: (i, 0)),
        scratch_shapes=[pltpu.VMEM((buf_rows, ATTN_WIDTH), BF16), pltpu.VMEM((buf_rows, ATTN_WIDTH), BF16),
                        pltpu.VMEM(s_shape, F32), pltpu.VMEM(s_shape, BF16)],
        compiler_params=_params(1, _vmem_limit(blocks, scratch, 4 * _nbytes((2 * CHUNK, KEY_WIN), F32))),
        name="attn_prompt",
    )(z_attn, z_attn, z_attn, bias)


def _attn_sample_kernel(q_ref, kn_ref, vn_ref, kc_ref, vc_ref, bias_ref, o_ref, kbuf, vbuf, s_scr, p_scr, *,
                        s, past):
    tail = jnp.zeros((KEY_WIN - past - s, ATTN_WIDTH), BF16)
    kbuf[0:past, :] = kc_ref[...].astype(BF16)
    vbuf[0:past, :] = vc_ref[...].astype(BF16)
    kbuf[past:past + s, :] = kn_ref[...].astype(BF16)
    vbuf[past:past + s, :] = vn_ref[...].astype(BF16)
    kbuf[past + s:KEY_WIN, :] = tail
    vbuf[past + s:KEY_WIN, :] = tail
    sl = lambda p: slice(p * PAIR_W, (p + 1) * PAIR_W)

    def o_put(w, p, o):
        o_ref[:, sl(p)] = o.astype(BF16)

    _attend(lambda w, p: q_ref[:, sl(p)], lambda w, p: kbuf[:, sl(p)], lambda w, p: vbuf[:, sl(p)], bias_ref,
            o_put, s_scr, p_scr, mq=s, n_win=1, min_col=lambda w: None)


def _attn_sample(z_attn, k_cache, v_cache, bias, *, nb, s):
    past = k_cache.shape[1]
    assert past + s <= KEY_WIN and z_attn.shape[0] == nb * s
    blk = (s, ATTN_WIDTH)
    cblk = (None, past, ATTN_WIDTH)
    s_shape = (N_PAIRS, 2 * s, KEY_WIN)
    blocks = (3 * _nbytes(blk, F32) + 2 * _nbytes((past, ATTN_WIDTH), F32) + _nbytes(bias.shape, F32)
              + _nbytes(blk, BF16))
    scratch = 2 * _nbytes((KEY_WIN, ATTN_WIDTH), BF16) + _nbytes(s_shape, F32) + _nbytes(s_shape, BF16)
    return pl.pallas_call(
        functools.partial(_attn_sample_kernel, s=s, past=past),
        out_shape=jax.ShapeDtypeStruct((nb * s, ATTN_WIDTH), BF16),
        grid=(nb,),
        in_specs=[pl.BlockSpec(blk, lambda b: (b, 0)),
                  pl.BlockSpec(blk, lambda b: (b, 1)),
                  pl.BlockSpec(blk, lambda b: (b, 2)),
                  pl.BlockSpec(cblk, lambda b: (b, 0, 0)),
                  pl.BlockSpec(cblk, lambda b: (b, 0, 0)),
                  pl.BlockSpec(bias.shape, lambda b: (0, 0, 0))],
        out_specs=pl.BlockSpec(blk, lambda b: (b, 0)),
        scratch_shapes=[pltpu.VMEM((KEY_WIN, ATTN_WIDTH), BF16), pltpu.VMEM((KEY_WIN, ATTN_WIDTH), BF16),
                        pltpu.VMEM(s_shape, F32), pltpu.VMEM(s_shape, BF16)],
        compiler_params=_params(1, _vmem_limit(blocks, scratch, 4 * _nbytes((2 * s, KEY_WIN), F32))),
        name="attn_sample",
    )(z_attn, z_attn, z_attn, k_cache, v_cache, bias)


def _bf(x):
    return x.astype(BF16)


def _split2(x):
    hi = _bf(x)
    return hi, _bf(x - hi.astype(F32))


_NN = (((1,), (0,)), ((), ()))
_NT = (((1,), (1,)), ((), ()))


def _mm(x, y, dims=_NN):
    return lax.dot_general(x, y, dims, preferred_element_type=F32)


def _bd(y, lo):
    zero = jnp.zeros_like(y)
    return jnp.concatenate([jnp.where(lo, y, zero), jnp.where(lo, zero, y)], axis=0)


def _seg_sum(x, ones_seg, *, passes):
    r = x.shape[0]
    half = x.shape[1] // (2 * PAIR_W)
    slab = lambda p: x[:, p * PAIR_W:(p + 1) * PAIR_W]
    xx = jnp.concatenate([jnp.concatenate([slab(p) for p in range(half)], axis=0),
                          jnp.concatenate([slab(half + p) for p in range(half)], axis=0)], axis=1)
    hi, lo = _split2(xx)
    s = _mm(hi, ones_seg)
    if passes == 2:
        s = s + _mm(lo, ones_seg)
    return jnp.concatenate([s[p * r:(p + 1) * r, 0:PAIR_W] for p in range(half)]
                           + [s[p * r:(p + 1) * r, PAIR_W:2 * PAIR_W] for p in range(half)], axis=1)


def _sigmoid(x):
    return 1.0 / (1.0 + jnp.exp(-x))


def _sigmoid_tanh(x):
    return 0.5 + 0.5 * jnp.tanh(0.5 * x)


def _rwkv_kernel(z_ref, sh_ref, s0_ref, mu_ref, w0_ref, w2cat_ref, a0_ref, gw2_ref, kk_ref, ka_ref,
                 rk_ref, lnw_ref, lnb_ref, o_ref, sfin_ref, s_scr, carry_scr, y_scr, *, rows, cps):
    ct = CHUNK_T
    pw = PAIR_W
    tt = cps * ct
    assert rows == ct or cps == 1
    c = pl.program_id(1)

    @pl.when(c == 0)
    def _():
        for p in range(N_PAIRS):
            s_scr[:, p * pw:(p + 1) * pw] = jnp.concatenate([s0_ref[2 * p], s0_ref[2 * p + 1]], axis=1)
        carry_scr[0:1, :] = sh_ref[...]

    z = z_ref[...]
    if rows < ct:
        z = jnp.concatenate([z, jnp.zeros((ct - rows, SHIFT_W), F32)], axis=0)
    row_all = lax.broadcasted_iota(jnp.int32, (tt, 1), 0)
    zprev = jnp.where(row_all == 0, carry_scr[0:1, :], pltpu.roll(z, 1, axis=0))
    last = (cps - 1) * ct + rows - 1
    carry_scr[0:1, :] = z[last:last + 1, :]
    zs = z + (zprev - z) * mu_ref[...]

    seg = 2 * pw
    ones_seg = _bf(lax.broadcasted_iota(jnp.int32, (seg, seg), 0) // HEAD_DIM
                   == lax.broadcasted_iota(jnp.int32, (seg, seg), 1) // HEAD_DIM)
    row = lax.broadcasted_iota(jnp.int32, (ct, 1), 0)
    tri = _bf(lax.broadcasted_iota(jnp.int32, (ct, ct), 1) <= lax.broadcasted_iota(jnp.int32, (ct, ct), 0))
    t_lane = lax.broadcasted_iota(jnp.int32, (ct, 2 * ct), 1) % ct
    strict = t_lane < row
    incl = t_lane <= row
    lo_t = _lane_lo(ct)
    lo_p = lax.broadcasted_iota(jnp.int32, (ct, 2 * ct), 1) < ct
    lo_t2 = lax.broadcasted_iota(jnp.int32, (ct, 2 * pw), 1) % pw < HEAD_DIM
    diag = (lax.broadcasted_iota(jnp.int32, (HEAD_DIM, pw), 1) % HEAD_DIM
            == lax.broadcasted_iota(jnp.int32, (HEAD_DIM, pw), 0))
    lo_s = _lane_lo(HEAD_DIM)
    n_dbl = int(math.log2(ct))
    assert 1 << n_dbl == ct and 2 * ct == pw
    pairs = range(N_PAIRS)
    cat0 = lambda xs: jnp.concatenate(xs, axis=0)
    cat1 = lambda xs: jnp.concatenate(xs, axis=1)
    tok = lambda xv: [xv[:, p * pw:(p + 1) * pw] for p in pairs]

    def tn(xv, yv):
        zz = _mm(_bf(xv.T), yv)
        return jnp.where(lo_s, zz[:HEAD_DIM], zz[HEAD_DIM:])

    def chunk_pre(gi):
        zc = zs[gi * ct:(gi + 1) * ct]
        r = zc[:, 0:RWKV_WIDTH]
        k = zc[:, RWKV_WIDTH:2 * RWKV_WIDTH]
        v = zc[:, 2 * RWKV_WIDTH:3 * RWKV_WIDTH]
        zwa = zc[:, 3 * RWKV_WIDTH:3 * RWKV_WIDTH + DECAY_RANK + AAA_RANK]
        zg = zc[:, 3 * RWKV_WIDTH + DECAY_RANK + AAA_RANK:SHIFT_W]
        lora_in = cat0([jnp.where(lo_t, jnp.tanh(zwa), 0.0), jnp.where(lo_t, 0.0, zwa)])
        lora = _mm(_bf(lora_in), w2cat_ref[...])
        yield
        lw = -math.exp(-0.5) * _sigmoid_tanh(w0_ref[...] + lora[:ct])
        if rows < ct:
            live = row < rows
            lw = jnp.where(live, lw, 0.0)
        l_hi, l_lo = _split2(lw)
        cl = _mm(tri, l_hi) + _mm(tri, l_lo)
        yield
        a = _sigmoid_tanh(a0_ref[...] + lora[ct:])
        g = _mm(_bf(_sigmoid_tanh(zg)), gw2_ref[...])
        yield
        kkr = k * kk_ref[...]
        kn = kkr * lax.rsqrt(jnp.maximum(_seg_sum(kkr * kkr, ones_seg, passes=1), 1e-24))
        yield
        k2 = k * (1.0 + (a - 1.0) * ka_ref[...])
        bonus = _seg_sum(r * k2 * rk_ref[...], ones_seg, passes=1) * v
        yield
        e_in = jnp.exp(cl)
        rt = r * e_in
        pc = e_in[ct - 1:ct, :]
        yield
        at = -kn * jnp.exp(cl - lw)
        yield
        e_inv = jnp.exp(-cl)
        bt = kn * a * e_inv
        kt = k2 * e_inv
        if rows < ct:
            bt = jnp.where(live, bt, 0.0)
            kt = jnp.where(live, kt, 0.0)
        yield
        bp = bt * pc
        kp = kt * pc
        yield
        return dict(a_=tok(at), r_=tok(rt), v_=tok(v), ab_=tok(_bf(at)), bb_=tok(_bf(bt)), kb_=tok(_bf(kt)),
                    rb_=tok(_bf(rt)), vb_=tok(_bf(v)), bpb_=tok(_bf(bp)), kpb_=tok(_bf(kp)), pc=pc,
                    bonus=bonus, g=g)

    def chunk_stages(t):
        v_bd = [_bd(t["vb_"][p], lo_t) for p in pairs]
        amat = [_mm(cat0([t["ab_"][p], t["rb_"][p]]), cat0([_bd(t["bb_"][p], lo_t), _bd(t["kb_"][p], lo_t)]), _NT)
                for p in pairs]
        yield
        a_ab = [jnp.where(strict, amat[p][:ct, :pw], 0.0) for p in pairs]
        a_ak = [_bf(jnp.where(strict, amat[p][:ct, pw:], 0.0)) for p in pairs]
        a_rb = [_bf(jnp.where(incl, amat[p][ct:, :pw], 0.0)) for p in pairs]
        a_rk = [_bf(jnp.where(incl, amat[p][ct:, pw:], 0.0)) for p in pairs]
        x = [cat1([t["a_"][p], _mm(a_ak[p], v_bd[p])]) for p in pairs]
        npow = [_bf(a_ab[p]) for p in pairs]
        yield
        for it in range(n_dbl):
            if it + 1 < n_dbl:
                res = [_mm(npow[p], cat1([_bd(_bf(x[p]), lo_t2), _bd(npow[p], lo_p)])) for p in pairs]
                npow = [_bf(res[p][:, 2 * pw:]) for p in pairs]
            else:
                res = [_mm(npow[p], _bd(_bf(x[p]), lo_t2)) for p in pairs]
            x = [x[p] + res[p][:, :2 * pw] for p in pairs]
            yield
        ry = [_mm(a_rb[p], _bd(_bf(x[p]), lo_t2)) for p in pairs]
        rp = [_bf(t["r_"][p] + ry[p][:, :pw]) for p in pairs]
        y0 = [ry[p][:, pw:] + _mm(a_rk[p], v_bd[p]) for p in pairs]
        yield
        m_mat = [_bf(jnp.where(diag, t["pc"][:, p * pw:(p + 1) * pw], 0.0) + tn(x[p][:, :pw], t["bpb_"][p]))
                 for p in pairs]
        yield
        n_mat = [tn(cat0([x[p][:, pw:], t["v_"][p]]), cat0([t["bpb_"][p], t["kpb_"][p]])) for p in pairs]
        return rp, y0, m_mat, n_mat

    def drive(*gens):
        done = [None] * len(gens)
        live = list(range(len(gens)))
        while live:
            for i in list(live):
                try:
                    next(gens[i])
                except StopIteration as stop:
                    done[i] = stop.value
                    live.remove(i)
        return done

    tokens = [drive(chunk_pre(gi))[0] for gi in range(cps)]
    stage_outs = drive(*[chunk_stages(t) for t in tokens])
    terms = [stage_outs[gi] + (tokens[gi]["bonus"], tokens[gi]["g"]) for gi in range(cps)]
    state = [s_scr[:, p * pw:(p + 1) * pw] for p in pairs]
    for gi, (rp, y0, m_mat, n_mat, _, _) in enumerate(terms):
        for p in pairs:
            s_old = _bf(state[p])
            y_scr[gi * ct:(gi + 1) * ct, p * pw:(p + 1) * pw] = _mm(rp[p], _bd(s_old, lo_s), _NT) + y0[p]
            state[p] = _mm(s_old, _bd(m_mat[p], lo_s)) + n_mat[p]
    for p in pairs:
        s_scr[:, p * pw:(p + 1) * pw] = state[p]
    bonus = cat0([t[4] for t in terms])
    g = cat0([t[5] for t in terms])

    y = y_scr[...]
    mean = _seg_sum(y, ones_seg, passes=2) * (1.0 / HEAD_DIM)
    dv = y - mean
    var = _seg_sum(dv * dv, ones_seg, passes=1) * (1.0 / HEAD_DIM)
    yn = dv * lax.rsqrt(var + GN_EPS) * lnw_ref[...] + lnb_ref[...]
    out = (yn + bonus) * g
    o_ref[...] = out[:(cps - 1) * ct + rows].astype(BF16)

    @pl.when(c == pl.num_programs(1) - 1)
    def _():
        for h in range(N_HEADS):
            sfin_ref[h] = s_scr[:, h * HEAD_DIM:(h + 1) * HEAD_DIM]


def _rwkv(z_rwkv, shift0, s0, lp, *, nb, seq, name):
    rows = min(seq, CHUNK_T)
    cps = min(seq // rows, RWKV_CHUNKS_PER_STEP)
    blk_rows = cps * rows
    assert seq % blk_rows == 0 and z_rwkv.shape == (nb * seq, SHIFT_W)
    nck = seq // blk_rows
    tt = cps * CHUNK_T
    row1 = lambda n: pl.BlockSpec((1, n), lambda b, c: (0, 0))
    full = lambda r, n: pl.BlockSpec((r, n), lambda b, c: (0, 0))
    st_blk = pl.BlockSpec((None, N_HEADS, HEAD_DIM, HEAD_DIM), lambda b, c: (b, 0, 0, 0))
    lora_rows = DECAY_RANK + AAA_RANK
    blocks = (_nbytes((blk_rows, SHIFT_W), F32) + 2 * _nbytes((HEAD_DIM, RWKV_WIDTH), F32)
              + _nbytes((lora_rows + GATE_RANK, RWKV_WIDTH), BF16) + _nbytes((16, SHIFT_W), F32)
              + _nbytes((blk_rows, RWKV_WIDTH), BF16))
    scratch = (_nbytes((HEAD_DIM, RWKV_WIDTH), F32) + _nbytes((8, SHIFT_W), F32)
               + _nbytes((tt, RWKV_WIDTH), F32))
    out, s_fin = pl.pallas_call(
        functools.partial(_rwkv_kernel, rows=rows, cps=cps),
        out_shape=(jax.ShapeDtypeStruct((nb * seq, RWKV_WIDTH), BF16),
                   jax.ShapeDtypeStruct((nb, N_HEADS, HEAD_DIM, HEAD_DIM), F32)),
        grid=(nb, nck),
        in_specs=[pl.BlockSpec((blk_rows, SHIFT_W), lambda b, c: (b * nck + c, 0)),
                  pl.BlockSpec((None, 1, SHIFT_W), lambda b, c: (b, 0, 0)),
                  st_blk,
                  row1(SHIFT_W), row1(RWKV_WIDTH), full(lora_rows, RWKV_WIDTH), row1(RWKV_WIDTH),
                  full(GATE_RANK, RWKV_WIDTH), row1(RWKV_WIDTH),
                  row1(RWKV_WIDTH), row1(RWKV_WIDTH), row1(RWKV_WIDTH), row1(RWKV_WIDTH)],
        out_specs=(pl.BlockSpec((blk_rows, RWKV_WIDTH), lambda b, c: (b * nck + c, 0)), st_blk),
        scratch_shapes=[pltpu.VMEM((HEAD_DIM, RWKV_WIDTH), F32), pltpu.VMEM((8, SHIFT_W), F32),
                        pltpu.VMEM((tt, RWKV_WIDTH), F32)],
        compiler_params=_params(2, _vmem_limit(blocks, scratch, 64 * _nbytes((tt, RWKV_WIDTH), F32))),
        name=name,
    )(z_rwkv, shift0, s0, lp["mu"], lp["w0"], lp["w2cat"], lp["a0"], lp["g_w2"],
      lp["k_k"], lp["k_a"], lp["r_k"], lp["lnx_w"], lp["lnx_b"])
    return out, s_fin


def _oproj_kernel(a_ref, r_ref, wa_ref, wr_ref, x_ref, g_ref, h_ref, n_ref):
    h = (x_ref[...] + jnp.dot(a_ref[...], wa_ref[...], preferred_element_type=F32)
         + jnp.dot(r_ref[...], wr_ref[...], preferred_element_type=F32))
    h_ref[...] = h
    n_ref[...] = _rms(h, g_ref[...]).astype(BF16)


def _oproj(attn, rwkv, w_o, x, g, *, tm, name):
    t = x.shape[0]
    assert t % tm == 0
    half = (ATTN_WIDTH, D_MODEL)
    blocks = (2 * _nbytes((tm, ATTN_WIDTH), BF16) + 2 * _nbytes(half, BF16) + 2 * _nbytes((tm, D_MODEL), F32)
              + _nbytes((tm, D_MODEL), BF16))
    return pl.pallas_call(
        _oproj_kernel,
        out_shape=(jax.ShapeDtypeStruct((t, D_MODEL), F32), jax.ShapeDtypeStruct((t, D_MODEL), BF16)),
        grid=(t // tm,),
        in_specs=[pl.BlockSpec((tm, ATTN_WIDTH), lambda i: (i, 0)),
                  pl.BlockSpec((tm, RWKV_WIDTH), lambda i: (i, 0)),
                  pl.BlockSpec(half, lambda i: (0, 0)),
                  pl.BlockSpec(half, lambda i: (1, 0)),
                  pl.BlockSpec((tm, D_MODEL), lambda i: (i, 0)),
                  pl.BlockSpec((1, D_MODEL), lambda i: (0, 0))],
        out_specs=(pl.BlockSpec((tm, D_MODEL), lambda i: (i, 0)), pl.BlockSpec((tm, D_MODEL), lambda i: (i, 0))),
        compiler_params=_params(1, _vmem_limit(blocks, 0, 2 * _nbytes((tm, D_MODEL), F32))),
        name=name,
    )(attn, rwkv, w_o, w_o, x, g)


def _ffn_kernel(n_ref, h_ref, wg_ref, wu_ref, wd_ref, g_ref, o_ref, acc_ref):
    j = pl.program_id(1)

    @pl.when(j == 0)
    def _():
        acc_ref[...] = h_ref[...]

    n = n_ref[...]
    gate = jnp.dot(n, wg_ref[...], preferred_element_type=F32)
    up = jnp.dot(n, wu_ref[...], preferred_element_type=F32)
    act = (gate * _sigmoid(gate) * up).astype(BF16)
    acc_ref[...] += jnp.dot(act, wd_ref[...], preferred_element_type=F32)

    @pl.when(j == pl.num_programs(1) - 1)
    def _():
        o_ref[...] = _rms(acc_ref[...], g_ref[...])


def _ffn(n2, h, wg, wu, wd, g, *, tm, tf, name):
    t = h.shape[0]
    assert t % tm == 0 and D_FF % tf == 0
    blocks = (_nbytes((tm, D_MODEL), BF16) + 2 * _nbytes((tm, D_MODEL), F32) + 3 * _nbytes((D_MODEL, tf), BF16))
    return pl.pallas_call(
        _ffn_kernel,
        out_shape=jax.ShapeDtypeStruct((t, D_MODEL), F32),
        grid=(t // tm, D_FF // tf),
        in_specs=[pl.BlockSpec((tm, D_MODEL), lambda i, j: (i, 0)),
                  pl.BlockSpec((tm, D_MODEL), lambda i, j: (i, 0)),
                  pl.BlockSpec((D_MODEL, tf), lambda i, j: (0, j)),
                  pl.BlockSpec((D_MODEL, tf), lambda i, j: (0, j)),
                  pl.BlockSpec((tf, D_MODEL), lambda i, j: (j, 0)),
                  pl.BlockSpec((1, D_MODEL), lambda i, j: (0, 0))],
        out_specs=pl.BlockSpec((tm, D_MODEL), lambda i, j: (i, 0)),
        scratch_shapes=[pltpu.VMEM((tm, D_MODEL), F32)],
        compiler_params=_params(2, _vmem_limit(blocks, _nbytes((tm, D_MODEL), F32), 4 * _nbytes((tm, tf), F32))),
        name=name,
    )(n2, h, wg, wu, wd, g)


def _trunk(x, lp, wts, attn_fn, shift0, s0, *, nb, seq, tag):
    t = nb * seq
    tm = min(t, 1024)
    z_attn = _inproj(x, lp["ln1"], wts["w_in"], col0=0, n=3 * ATTN_WIDTH, tm=tm, tn=1024,
                     name=f"inproj_attn_{tag}")
    z_rwkv = _inproj(x, lp["ln1"], wts["w_in"], col0=3 * ATTN_WIDTH, n=SHIFT_W, tm=tm, tn=SHIFT_W // 2,
                     name=f"inproj_rwkv_{tag}")
    attn = attn_fn(z_attn)
    rwkv, s_fin = _rwkv(z_rwkv, shift0, s0, lp, nb=nb, seq=seq, name=f"rwkv_{tag}")
    tm2 = min(t, 512)
    h, n2 = _oproj(attn, rwkv, wts["w_o"], x, lp["ln2"], tm=tm2, name=f"oproj_{tag}")
    y = _ffn(n2, h, wts["w_gate"], wts["w_up"], wts["w_down"], lp["final_norm"], tm=tm2, tf=512, name=f"ffn_{tag}")
    return y, z_attn, z_rwkv, s_fin


def kernel(x_prompt, x_sample, cache_attn_k, cache_attn_v, state_rwkv_wkv, state_rwkv_shift, ln1, w_in, rel_table,
           mu, w0, w_w2, a0, a_w2, g_w2, k_k, k_a, r_k, lnx_w, lnx_b, w_o, ln2, w_gate, w_up, w_down, final_norm):
    depth = w_in.shape[0]
    assert depth == 1, "final RMSNorm is fused into the (single) layer's FFN call"
    b, t, _ = x_prompt.shape
    bd, s, _ = x_sample.shape
    past = cache_attn_k.shape[2]
    n_keep = min(BAND_PAST, t)
    l = 0
    row = lambda p: p[l].reshape(1, -1)
    lp = dict(ln1=row(ln1), mu=row(mu), w0=row(w0), a0=row(a0),
              w2cat=jnp.concatenate([w_w2[l], a_w2[l]], axis=0).astype(BF16), g_w2=g_w2[l].astype(BF16),
              k_k=row(k_k), k_a=row(k_a), r_k=row(r_k), lnx_w=row(lnx_w), lnx_b=row(lnx_b), ln2=row(ln2),
              final_norm=final_norm.reshape(1, -1))
    wts = dict(w_in=w_in[l].astype(BF16),
               w_o=w_o[l].astype(BF16), w_gate=w_gate[l].astype(BF16), w_up=w_up[l].astype(BF16),
               w_down=w_down[l].astype(BF16))
    table = rel_table[l]

    bias_p = _bias_tiles(table, mq=CHUNK, off=BAND_PAST + CHUNK, col_lo=CHUNK, col_hi=KEY_WIN, name="bias_prompt")
    bias_s = _bias_tiles(table, mq=s, off=past, col_lo=0, col_hi=past + s, name="bias_sample")

    assert b == 1
    yp, zap, zrp, wkv_p = _trunk(
        x_prompt.reshape(b * t, D_MODEL), lp, wts, lambda z: _attn_prompt(z, bias_p),
        jnp.zeros((b, 1, SHIFT_W), F32), jnp.zeros((b, N_HEADS, HEAD_DIM, HEAD_DIM), F32), nb=b, seq=t, tag="p")
    kc = cache_attn_k[l].reshape(bd, past, ATTN_WIDTH)
    vc = cache_attn_v[l].reshape(bd, past, ATTN_WIDTH)
    ys, zas, zrs, wkv_s = _trunk(
        x_sample.reshape(bd * s, D_MODEL), lp, wts, lambda z: _attn_sample(z, kc, vc, bias_s, nb=bd, s=s),
        state_rwkv_shift[l], state_rwkv_wkv[l], nb=bd, seq=s, tag="s")

    heads = lambda z, n, tt: z.reshape(n, tt, N_HEADS, HEAD_DIM)
    zap = zap.reshape(b, t, 3 * ATTN_WIDTH)
    zas = zas.reshape(bd, s, 3 * ATTN_WIDTH)
    k_p = heads(zap[:, t - n_keep:, ATTN_WIDTH:2 * ATTN_WIDTH], b, n_keep)
    v_p = heads(zap[:, t - n_keep:, 2 * ATTN_WIDTH:], b, n_keep)
    k_s = heads(zas[:, :, ATTN_WIDTH:2 * ATTN_WIDTH], bd, s)
    v_s = heads(zas[:, :, 2 * ATTN_WIDTH:], bd, s)
    sh_p = zrp.reshape(b, t, SHIFT_W)[:, t - 1:]
    sh_s = zrs.reshape(bd, s, SHIFT_W)[:, s - 1:]
    return (yp.reshape(b, t, D_MODEL), ys.reshape(bd, s, D_MODEL),
            k_p[None], v_p[None], wkv_p[None], sh_p[None],
            k_s[None], v_s[None], wkv_s[None], sh_s[None])
```

```python
import functools
import math

import jax
import jax.numpy as jnp
from jax import lax
from jax.experimental import pallas as pl
from jax.experimental.pallas import tpu as pltpu

F32 = jnp.float32
BF16 = jnp.bfloat16

D_MODEL = 2048
CHUNK = 64
N_PAST_CHUNKS = 8
BAND_PAST = N_PAST_CHUNKS * CHUNK
ATTN_WIDTH = 1024
HEAD_DIM = 64
N_HEADS = 16
REL_CLIP = 128
N_REL = 2 * REL_CLIP + 1
RWKV_WIDTH = 1024
DECAY_RANK = 64
AAA_RANK = 64
GATE_RANK = 128
SHIFT_W = 3 * RWKV_WIDTH + DECAY_RANK + AAA_RANK + GATE_RANK
D_FF = 5632
RMS_EPS = 1e-6
GN_EPS = 64e-5

V7X_LANES = 128
V7X_VMEM_BYTES = 64 * 1024 * 1024
V7X_SCOPED_VMEM_CAP_BYTES = 60000 * 1024

PAIR_W = 2 * HEAD_DIM
N_PAIRS = N_HEADS // 2
CHUNK_T = 64
KEY_WIN = (N_PAST_CHUNKS + 2) * CHUNK
NEG_BIG = -1e30
ATTN_QB = 512
ATTN_CHUNKS_PER_ITER = 2
ATTN_SOFTMAX_ROWS = 128
RWKV_CHUNKS_PER_STEP = 2

assert PAIR_W == V7X_LANES


def _vmem_limit(pipelined_bytes, scratch_bytes, temp_bytes):
    need = 2 * pipelined_bytes + scratch_bytes + temp_bytes
    return int(min(max(need, 16 * 1024 * 1024), V7X_SCOPED_VMEM_CAP_BYTES))


def _nbytes(shape, dtype):
    return math.prod(shape) * jnp.dtype(dtype).itemsize


def _params(n_grid, vmem):
    return pltpu.CompilerParams(dimension_semantics=("arbitrary",) * n_grid, vmem_limit_bytes=vmem)


def _rms(x, g):
    return x * lax.rsqrt(jnp.mean(x * x, axis=-1, keepdims=True) + RMS_EPS) * g


def _inproj_kernel(x_ref, g_ref, w_ref, o_ref, n_ref):
    @pl.when(pl.program_id(1) == 0)
    def _():
        n_ref[...] = _rms(x_ref[...], g_ref[...]).astype(BF16)

    o_ref[...] = jnp.dot(n_ref[...], w_ref[...], preferred_element_type=F32)


def _inproj(x, g, w, *, col0, n, tm, tn, name):
    t, d = x.shape
    assert t % tm == 0 and n % tn == 0 and col0 % V7X_LANES == 0 and col0 + n <= w.shape[1]
    blocks = _nbytes((tm, d), F32) + _nbytes((d, tn), BF16) + _nbytes((tm, tn), F32)
    return pl.pallas_call(
        _inproj_kernel,
        out_shape=jax.ShapeDtypeStruct((t, n), F32),
        grid=(t // tm, n // tn),
        in_specs=[pl.BlockSpec((tm, d), lambda i, j: (i, 0)),
                  pl.BlockSpec((1, d), lambda i, j: (0, 0)),
                  pl.BlockSpec((pl.Element(d), pl.Element(tn)),
                               lambda i, j: (0, (col0 // V7X_LANES + j * (tn // V7X_LANES)) * V7X_LANES))],
        out_specs=pl.BlockSpec((tm, tn), lambda i, j: (i, j)),
        scratch_shapes=[pltpu.VMEM((tm, d), BF16)],
        compiler_params=_params(2, _vmem_limit(blocks, _nbytes((tm, d), BF16), 2 * _nbytes((tm, d), F32))),
        name=name,
    )(x, g, w)


BIAS_VAR_COL0 = KEY_WIN - 2 * V7X_LANES
BIAS_DIAG_W = 512


def _bias_kernel(tab_ref, o_ref, *, mq, off, col_lo, col_hi):
    nvar = KEY_WIN - BIAS_VAR_COL0
    w = BIAS_DIAG_W
    assert nvar + mq <= w
    lane = lax.broadcasted_iota(jnp.int32, (1, w), 1)
    idx = jnp.clip(off - (BIAS_VAR_COL0 - mq + lane), -REL_CLIP, REL_CLIP) + REL_CLIP
    idx_min = max(0, min(REL_CLIP, off - (BIAS_VAR_COL0 - mq + w - 1)) + REL_CLIP)
    heads = range(N_HEADS)

    def body(j, profs):
        hit = idx == j
        return tuple(jnp.where(hit, tab_ref[h, j], prof) for h, prof in zip(heads, profs))

    init = tuple(jnp.full((1, w), tab_ref[h, N_REL - 1], F32) for h in heads)
    profs = lax.fori_loop(idx_min, N_REL - 1, body, init)

    c = lax.broadcasted_iota(jnp.int32, (mq, nvar), 1) + BIAS_VAR_COL0
    valid_var = (c >= col_lo) & (c < col_hi)
    c_far = lax.broadcasted_iota(jnp.int32, (mq, BIAS_VAR_COL0), 1)
    valid_far = (c_far >= col_lo) & (c_far < col_hi)
    for h in heads:
        tile = pltpu.roll(jnp.broadcast_to(profs[h], (mq, w)), w - mq, axis=1, stride=1, stride_axis=0)
        rows = slice((h % 2) * mq, (h % 2 + 1) * mq)
        far = jnp.full((mq, BIAS_VAR_COL0), tab_ref[h, N_REL - 1], F32)
        o_ref[h // 2, rows, 0:BIAS_VAR_COL0] = jnp.where(valid_far, far, NEG_BIG)
        o_ref[h // 2, rows, BIAS_VAR_COL0:KEY_WIN] = jnp.where(valid_var, tile[:, 0:nvar], NEG_BIG)


def _bias_tiles(table, *, mq, off, col_lo, col_hi, name):
    assert off - (BIAS_VAR_COL0 - 1) >= REL_CLIP
    return pl.pallas_call(
        functools.partial(_bias_kernel, mq=mq, off=off, col_lo=col_lo, col_hi=col_hi),
        out_shape=jax.ShapeDtypeStruct((N_PAIRS, 2 * mq, KEY_WIN), F32),
        in_specs=[pl.BlockSpec(memory_space=pltpu.SMEM)],
        out_specs=pl.BlockSpec(memory_space=pltpu.VMEM),
        name=name,
    )(table)


def _lane_lo(rows):
    return lax.broadcasted_iota(jnp.int32, (rows, PAIR_W), 1) < HEAD_DIM


def _attend(q_pair, k_pair, v_pair, bias_ref, o_put, s_scr, p_scr, *, mq, n_win, min_col):
    lo = _lane_lo(mq)
    units = [(w, p) for w in range(n_win) for p in range(N_PAIRS)]
    for u, (w, p) in enumerate(units):
        qs = q_pair(w, p) * (HEAD_DIM ** -0.5)
        q2 = jnp.concatenate([jnp.where(lo, qs, 0.0), jnp.where(lo, 0.0, qs)], axis=0).astype(BF16)
        s = lax.dot_general(q2, k_pair(w, p), (((1,), (1,)), ((), ())), preferred_element_type=F32) + bias_ref[p]
        first = min_col(w)
        if first is not None:
            col = lax.broadcasted_iota(jnp.int32, (1, KEY_WIN), 1)
            s = jnp.where(col >= first, s, NEG_BIG)
        s_scr[u] = s
    inv_l = []
    rb = min(2 * mq, ATTN_SOFTMAX_ROWS)
    for u in range(len(units)):
        parts = []
        for r0 in range(0, 2 * mq, rb):
            s = s_scr[u, r0:r0 + rb, :]
            e = jnp.exp(s - jnp.max(s, axis=-1, keepdims=True))
            parts.append(1.0 / jnp.sum(e, axis=-1, keepdims=True))
            p_scr[u, r0:r0 + rb, :] = e.astype(BF16)
        inv_l.append(jnp.concatenate(parts, axis=0))
    for u, (w, p) in enumerate(units):
        pv = jnp.dot(p_scr[u], v_pair(w, p), preferred_element_type=F32) * inv_l[u]
        o_put(w, p, jnp.where(lo, pv[:mq], pv[mq:]))


def _attn_prompt_kernel(q_ref, kc_ref, vc_ref, bias_ref, o_ref, kbuf, vbuf, s_scr, p_scr):
    i = pl.program_id(0)
    prev0, cur0, end = CHUNK, CHUNK + ATTN_QB, CHUNK + 2 * ATTN_QB

    @pl.when(i == 0)
    def _():
        zeros = jnp.zeros((cur0, ATTN_WIDTH), BF16)
        kbuf[0:cur0, :] = zeros
        vbuf[0:cur0, :] = zeros

    kbuf[cur0:end, :] = kc_ref[...].astype(BF16)
    vbuf[cur0:end, :] = vc_ref[...].astype(BF16)
    chunks_per_step = ATTN_QB // CHUNK
    lead_chunks = N_PAST_CHUNKS + 1
    n_win = ATTN_CHUNKS_PER_ITER

    def iter_body(j, carry, *, masked):
        sl = lambda p: slice(p * PAIR_W, (p + 1) * PAIR_W)
        r0 = lambda w: pl.multiple_of((j * n_win + w) * CHUNK, CHUNK)
        min_col = lambda w: (lead_chunks - (i * chunks_per_step + j * n_win + w)) * CHUNK if masked else None

        def o_put(w, p, o):
            o_ref[pl.ds(r0(w), CHUNK), sl(p)] = o.astype(BF16)

        _attend(lambda w, p: q_ref[pl.ds(r0(w), CHUNK), sl(p)], lambda w, p: kbuf[pl.ds(r0(w), KEY_WIN), sl(p)],
                lambda w, p: vbuf[pl.ds(r0(w), KEY_WIN), sl(p)], bias_ref, o_put, s_scr, p_scr, mq=CHUNK,
                n_win=n_win, min_col=min_col)
        return carry

    masked_steps = -(-lead_chunks // chunks_per_step)

    @pl.when(i < masked_steps)
    def _():
        lax.fori_loop(0, chunks_per_step // n_win, functools.partial(iter_body, masked=True), 0)

    @pl.when(i >= masked_steps)
    def _():
        lax.fori_loop(0, chunks_per_step // n_win, functools.partial(iter_body, masked=False), 0)

    kbuf[prev0:cur0, :] = kbuf[cur0:end, :]
    vbuf[prev0:cur0, :] = vbuf[cur0:end, :]


def _attn_prompt(z_attn, bias):
    t = z_attn.shape[0]
    assert t % ATTN_QB == 0 and (ATTN_QB // CHUNK) % ATTN_CHUNKS_PER_ITER == 0 and ATTN_QB >= BAND_PAST
    blk = (ATTN_QB, ATTN_WIDTH)
    buf_rows = CHUNK + 2 * ATTN_QB
    s_shape = (ATTN_CHUNKS_PER_ITER * N_PAIRS, 2 * CHUNK, KEY_WIN)
    blocks = 3 * _nbytes(blk, F32) + _nbytes(bias.shape, F32) + _nbytes(blk, BF16)
    scratch = 2 * _nbytes((buf_rows, ATTN_WIDTH), BF16) + _nbytes(s_shape, F32) + _nbytes(s_shape, BF16)
    return pl.pallas_call(
        _attn_prompt_kernel,
        out_shape=jax.ShapeDtypeStruct((t, ATTN_WIDTH), BF16),
        grid=(t // ATTN_QB,),
        in_specs=[pl.BlockSpec(blk, lambda i: (i, 0)),
                  pl.BlockSpec(blk, lambda i: (i, 1)),
                  pl.BlockSpec(blk, lambda i: (i, 2)),
                  pl.BlockSpec(bias.shape, lambda i: (0, 0, 0))],
        out_specs=pl.BlockSpec(blk, lambda i: (i, 0)),
        scratch_shapes=[pltpu.VMEM((buf_rows, ATTN_WIDTH), BF16), pltpu.VMEM((buf_rows, ATTN_WIDTH), BF16),
                        pltpu.VMEM(s_shape, F32), pltpu.VMEM(s_shape, BF16)],
        compiler_params=_params(1, _vmem_limit(blocks, scratch, 4 * _nbytes((2 * CHUNK, KEY_WIN), F32))),
        name="attn_prompt",
    )(z_attn, z_attn, z_attn, bias)


def _attn_sample_kernel(q_ref, kn_ref, vn_ref, kc_ref, vc_ref, bias_ref, o_ref, kbuf, vbuf, s_scr, p_scr, *,
                        s, past):
    tail = jnp.zeros((KEY_WIN - past - s, ATTN_WIDTH), BF16)
    kbuf[0:past, :] = kc_ref[...].astype(BF16)
    vbuf[0:past, :] = vc_ref[...].astype(BF16)
    kbuf[past:past + s, :] = kn_ref[...].astype(BF16)
    vbuf[past:past + s, :] = vn_ref[...].astype(BF16)
    kbuf[past + s:KEY_WIN, :] = tail
    vbuf[past + s:KEY_WIN, :] = tail
    sl = lambda p: slice(p * PAIR_W, (p + 1) * PAIR_W)

    def o_put(w, p, o):
        o_ref[:, sl(p)] = o.astype(BF16)

    _attend(lambda w, p: q_ref[:, sl(p)], lambda w, p: kbuf[:, sl(p)], lambda w, p: vbuf[:, sl(p)], bias_ref,
            o_put, s_scr, p_scr, mq=s, n_win=1, min_col=lambda w: None)


def _attn_sample(z_attn, k_cache, v_cache, bias, *, nb, s):
    past = k_cache.shape[1]
    assert past + s <= KEY_WIN and z_attn.shape[0] == nb * s
    blk = (s, ATTN_WIDTH)
    cblk = (None, past, ATTN_WIDTH)
    s_shape = (N_PAIRS, 2 * s, KEY_WIN)
    blocks = (3 * _nbytes(blk, F32) + 2 * _nbytes((past, ATTN_WIDTH), F32) + _nbytes(bias.shape, F32)
              + _nbytes(blk, BF16))
    scratch = 2 * _nbytes((KEY_WIN, ATTN_WIDTH), BF16) + _nbytes(s_shape, F32) + _nbytes(s_shape, BF16)
    return pl.pallas_call(
        functools.partial(_attn_sample_kernel, s=s, past=past),
        out_shape=jax.ShapeDtypeStruct((nb * s, ATTN_WIDTH), BF16),
        grid=(nb,),
        in_specs=[pl.BlockSpec(blk, lambda b: (b, 0)),
                  pl.BlockSpec(blk, lambda b: (b, 1)),
                  pl.BlockSpec(blk, lambda b: (b, 2)),
                  pl.BlockSpec(cblk, lambda b: (b, 0, 0)),
                  pl.BlockSpec(cblk, lambda b: (b, 0, 0)),
                  pl.BlockSpec(bias.shape, lambda b: (0, 0, 0))],
        out_specs=pl.BlockSpec(blk, lambda b: (b, 0)),
        scratch_shapes=[pltpu.VMEM((KEY_WIN, ATTN_WIDTH), BF16), pltpu.VMEM((KEY_WIN, ATTN_WIDTH), BF16),
                        pltpu.VMEM(s_shape, F32), pltpu.VMEM(s_shape, BF16)],
        compiler_params=_params(1, _vmem_limit(blocks, scratch, 4 * _nbytes((2 * s, KEY_WIN), F32))),
        name="attn_sample",
    )(z_attn, z_attn, z_attn, k_cache, v_cache, bias)


def _bf(x):
    return x.astype(BF16)


def _split2(x):
    hi = _bf(x)
    return hi, _bf(x - hi.astype(F32))


_NN = (((1,), (0,)), ((), ()))
_NT = (((1,), (1,)), ((), ()))


def _mm(x, y, dims=_NN):
    return lax.dot_general(x, y, dims, preferred_element_type=F32)


def _bd(y, lo):
    zero = jnp.zeros_like(y)
    return jnp.concatenate([jnp.where(lo, y, zero), jnp.where(lo, zero, y)], axis=0)


def _seg_sum(x, ones_seg, *, passes):
    r = x.shape[0]
    half = x.shape[1] // (2 * PAIR_W)
    slab = lambda p: x[:, p * PAIR_W:(p + 1) * PAIR_W]
    xx = jnp.concatenate([jnp.concatenate([slab(p) for p in range(half)], axis=0),
                          jnp.concatenate([slab(half + p) for p in range(half)], axis=0)], axis=1)
    hi, lo = _split2(xx)
    s = _mm(hi, ones_seg)
    if passes == 2:
        s = s + _mm(lo, ones_seg)
    return jnp.concatenate([s[p * r:(p + 1) * r, 0:PAIR_W] for p in range(half)]
                           + [s[p * r:(p + 1) * r, PAIR_W:2 * PAIR_W] for p in range(half)], axis=1)


def _sigmoid(x):
    return 1.0 / (1.0 + jnp.exp(-x))


def _sigmoid_tanh(x):
    return 0.5 + 0.5 * jnp.tanh(0.5 * x)


def _rwkv_kernel(z_ref, sh_ref, s0_ref, mu_ref, w0_ref, w2cat_ref, a0_ref, gw2_ref, kk_ref, ka_ref,
                 rk_ref, lnw_ref, lnb_ref, o_ref, sfin_ref, s_scr, carry_scr, y_scr, *, rows, cps):
    ct = CHUNK_T
    pw = PAIR_W
    tt = cps * ct
    assert rows == ct or cps == 1
    c = pl.program_id(1)

    @pl.when(c == 0)
    def _():
        for p in range(N_PAIRS):
            s_scr[:, p * pw:(p + 1) * pw] = jnp.concatenate([s0_ref[2 * p], s0_ref[2 * p + 1]], axis=1)
        carry_scr[0:1, :] = sh_ref[...]

    z = z_ref[...]
    if rows < ct:
        z = jnp.concatenate([z, jnp.zeros((ct - rows, SHIFT_W), F32)], axis=0)
    row_all = lax.broadcasted_iota(jnp.int32, (tt, 1), 0)
    zprev = jnp.where(row_all == 0, carry_scr[0:1, :], pltpu.roll(z, 1, axis=0))
    last = (cps - 1) * ct + rows - 1
    carry_scr[0:1, :] = z[last:last + 1, :]
    zs = z + (zprev - z) * mu_ref[...]

    seg = 2 * pw
    ones_seg = _bf(lax.broadcasted_iota(jnp.int32, (seg, seg), 0) // HEAD_DIM
                   == lax.broadcasted_iota(jnp.int32, (seg, seg), 1) // HEAD_DIM)
    row = lax.broadcasted_iota(jnp.int32, (ct, 1), 0)
    tri = _bf(lax.broadcasted_iota(jnp.int32, (ct, ct), 1) <= lax.broadcasted_iota(jnp.int32, (ct, ct), 0))
    t_lane = lax.broadcasted_iota(jnp.int32, (ct, 2 * ct), 1) % ct
    strict = t_lane < row
    incl = t_lane <= row
    lo_t = _lane_lo(ct)
    lo_p = lax.broadcasted_iota(jnp.int32, (ct, 2 * ct), 1) < ct
    lo_t2 = lax.broadcasted_iota(jnp.int32, (ct, 2 * pw), 1) % pw < HEAD_DIM
    diag = (lax.broadcasted_iota(jnp.int32, (HEAD_DIM, pw), 1) % HEAD_DIM
            == lax.broadcasted_iota(jnp.int32, (HEAD_DIM, pw), 0))
    lo_s = _lane_lo(HEAD_DIM)
    n_dbl = int(math.log2(ct))
    assert 1 << n_dbl == ct and 2 * ct == pw
    pairs = range(N_PAIRS)
    cat0 = lambda xs: jnp.concatenate(xs, axis=0)
    cat1 = lambda xs: jnp.concatenate(xs, axis=1)
    tok = lambda xv: [xv[:, p * pw:(p + 1) * pw] for p in pairs]

    def tn(xv, yv):
        zz = _mm(_bf(xv.T), yv)
        return jnp.where(lo_s, zz[:HEAD_DIM], zz[HEAD_DIM:])

    def chunk_pre(gi):
        zc = zs[gi * ct:(gi + 1) * ct]
        r = zc[:, 0:RWKV_WIDTH]
        k = zc[:, RWKV_WIDTH:2 * RWKV_WIDTH]
        v = zc[:, 2 * RWKV_WIDTH:3 * RWKV_WIDTH]
        zwa = zc[:, 3 * RWKV_WIDTH:3 * RWKV_WIDTH + DECAY_RANK + AAA_RANK]
        zg = zc[:, 3 * RWKV_WIDTH + DECAY_RANK + AAA_RANK:SHIFT_W]
        lora_in = cat0([jnp.where(lo_t, jnp.tanh(zwa), 0.0), jnp.where(lo_t, 0.0, zwa)])
        lora = _mm(_bf(lora_in), w2cat_ref[...])
        yield
        lw = -math.exp(-0.5) * _sigmoid_tanh(w0_ref[...] + lora[:ct])
        if rows < ct:
            live = row < rows
            lw = jnp.where(live, lw, 0.0)
        l_hi, l_lo = _split2(lw)
        cl = _mm(tri, l_hi) + _mm(tri, l_lo)
        yield
        a = _sigmoid_tanh(a0_ref[...] + lora[ct:])
        g = _mm(_bf(_sigmoid_tanh(zg)), gw2_ref[...])
        yield
        kkr = k * kk_ref[...]
        kn = kkr * lax.rsqrt(jnp.maximum(_seg_sum(kkr * kkr, ones_seg, passes=1), 1e-24))
        yield
        k2 = k * (1.0 + (a - 1.0) * ka_ref[...])
        bonus = _seg_sum(r * k2 * rk_ref[...], ones_seg, passes=1) * v
        yield
        e_in = jnp.exp(cl)
        rt = r * e_in
        pc = e_in[ct - 1:ct, :]
        yield
        at = -kn * jnp.exp(cl - lw)
        yield
        e_inv = jnp.exp(-cl)
        bt = kn * a * e_inv
        kt = k2 * e_inv
        if rows < ct:
            bt = jnp.where(live, bt, 0.0)
            kt = jnp.where(live, kt, 0.0)
        yield
        bp = bt * pc
        kp = kt * pc
        yield
        return dict(a_=tok(at), r_=tok(rt), v_=tok(v), ab_=tok(_bf(at)), bb_=tok(_bf(bt)), kb_=tok(_bf(kt)),
                    rb_=tok(_bf(rt)), vb_=tok(_bf(v)), bpb_=tok(_bf(bp)), kpb_=tok(_bf(kp)), pc=pc,
                    bonus=bonus, g=g)

    def chunk_stages(t):
        v_bd = [_bd(t["vb_"][p], lo_t) for p in pairs]
        amat = [_mm(cat0([t["ab_"][p], t["rb_"][p]]), cat0([_bd(t["bb_"][p], lo_t), _bd(t["kb_"][p], lo_t)]), _NT)
                for p in pairs]
        yield
        a_ab = [jnp.where(strict, amat[p][:ct, :pw], 0.0) for p in pairs]
        a_ak = [_bf(jnp.where(strict, amat[p][:ct, pw:], 0.0)) for p in pairs]
        a_rb = [_bf(jnp.where(incl, amat[p][ct:, :pw], 0.0)) for p in pairs]
        a_rk = [_bf(jnp.where(incl, amat[p][ct:, pw:], 0.0)) for p in pairs]
        x = [cat1([t["a_"][p], _mm(a_ak[p], v_bd[p])]) for p in pairs]
        npow = [_bf(a_ab[p]) for p in pairs]
        yield
        for it in range(n_dbl):
            if it + 1 < n_dbl:
                res = [_mm(npow[p], cat1([_bd(_bf(x[p]), lo_t2), _bd(npow[p], lo_p)])) for p in pairs]
                npow = [_bf(res[p][:, 2 * pw:]) for p in pairs]
            else:
                res = [_mm(npow[p], _bd(_bf(x[p]), lo_t2)) for p in pairs]
            x = [x[p] + res[p][:, :2 * pw] for p in pairs]
            yield
        ry = [_mm(a_rb[p], _bd(_bf(x[p]), lo_t2)) for p in pairs]
        rp = [_bf(t["r_"][p] + ry[p][:, :pw]) for p in pairs]
        y0 = [ry[p][:, pw:] + _mm(a_rk[p], v_bd[p]) for p in pairs]
        yield
        m_mat = [_bf(jnp.where(diag, t["pc"][:, p * pw:(p + 1) * pw], 0.0) + tn(x[p][:, :pw], t["bpb_"][p]))
                 for p in pairs]
        yield
        n_mat = [tn(cat0([x[p][:, pw:], t["v_"][p]]), cat0([t["bpb_"][p], t["kpb_"][p]])) for p in pairs]
        return rp, y0, m_mat, n_mat

    def drive(*gens):
        done = [None] * len(gens)
        live = list(range(len(gens)))
        while live:
            for i in list(live):
                try:
                    next(gens[i])
                except StopIteration as stop:
                    done[i] = stop.value
                    live.remove(i)
        return done

    tokens = [drive(chunk_pre(gi))[0] for gi in range(cps)]
    stage_outs = drive(*[chunk_stages(t) for t in tokens])
    terms = [stage_outs[gi] + (tokens[gi]["bonus"], tokens[gi]["g"]) for gi in range(cps)]
    state = [s_scr[:, p * pw:(p + 1) * pw] for p in pairs]
    for gi, (rp, y0, m_mat, n_mat, _, _) in enumerate(terms):
        for p in pairs:
            s_old = _bf(state[p])
            y_scr[gi * ct:(gi + 1) * ct, p * pw:(p + 1) * pw] = _mm(rp[p], _bd(s_old, lo_s), _NT) + y0[p]
            state[p] = _mm(s_old, _bd(m_mat[p], lo_s)) + n_mat[p]
    for p in pairs:
        s_scr[:, p * pw:(p + 1) * pw] = state[p]
    bonus = cat0([t[4] for t in terms])
    g = cat0([t[5] for t in terms])

    y = y_scr[...]
    mean = _seg_sum(y, ones_seg, passes=2) * (1.0 / HEAD_DIM)
    dv = y - mean
    var = _seg_sum(dv * dv, ones_seg, passes=1) * (1.0 / HEAD_DIM)
    yn = dv * lax.rsqrt(var + GN_EPS) * lnw_ref[...] + lnb_ref[...]
    out = (yn + bonus) * g
    o_ref[...] = out[:(cps - 1) * ct + rows].astype(BF16)

    @pl.when(c == pl.num_programs(1) - 1)
    def _():
        for h in range(N_HEADS):
            sfin_ref[h] = s_scr[:, h * HEAD_DIM:(h + 1) * HEAD_DIM]


def _rwkv(z_rwkv, shift0, s0, lp, *, nb, seq, name):
    rows = min(seq, CHUNK_T)
    cps = min(seq // rows, RWKV_CHUNKS_PER_STEP)
    blk_rows = cps * rows
    assert seq % blk_rows == 0 and z_rwkv.shape == (nb * seq, SHIFT_W)
    nck = seq // blk_rows
    tt = cps * CHUNK_T
    row1 = lambda n: pl.BlockSpec((1, n), lambda b, c: (0, 0))
    full = lambda r, n: pl.BlockSpec((r, n), lambda b, c: (0, 0))
    st_blk = pl.BlockSpec((None, N_HEADS, HEAD_DIM, HEAD_DIM), lambda b, c: (b, 0, 0, 0))
    lora_rows = DECAY_RANK + AAA_RANK
    blocks = (_nbytes((blk_rows, SHIFT_W), F32) + 2 * _nbytes((HEAD_DIM, RWKV_WIDTH), F32)
              + _nbytes((lora_rows + GATE_RANK, RWKV_WIDTH), BF16) + _nbytes((16, SHIFT_W), F32)
              + _nbytes((blk_rows, RWKV_WIDTH), BF16))
    scratch = (_nbytes((HEAD_DIM, RWKV_WIDTH), F32) + _nbytes((8, SHIFT_W), F32)
               + _nbytes((tt, RWKV_WIDTH), F32))
    out, s_fin = pl.pallas_call(
        functools.partial(_rwkv_kernel, rows=rows, cps=cps),
        out_shape=(jax.ShapeDtypeStruct((nb * seq, RWKV_WIDTH), BF16),
                   jax.ShapeDtypeStruct((nb, N_HEADS, HEAD_DIM, HEAD_DIM), F32)),
        grid=(nb, nck),
        in_specs=[pl.BlockSpec((blk_rows, SHIFT_W), lambda b, c: (b * nck + c, 0)),
                  pl.BlockSpec((None, 1, SHIFT_W), lambda b, c: (b, 0, 0)),
                  st_blk,
                  row1(SHIFT_W), row1(RWKV_WIDTH), full(lora_rows, RWKV_WIDTH), row1(RWKV_WIDTH),
                  full(GATE_RANK, RWKV_WIDTH), row1(RWKV_WIDTH),
                  row1(RWKV_WIDTH), row1(RWKV_WIDTH), row1(RWKV_WIDTH), row1(RWKV_WIDTH)],
        out_specs=(pl.BlockSpec((blk_rows, RWKV_WIDTH), lambda b, c: (b * nck + c, 0)), st_blk),
        scratch_shapes=[pltpu.VMEM((HEAD_DIM, RWKV_WIDTH), F32), pltpu.VMEM((8, SHIFT_W), F32),
                        pltpu.VMEM((tt, RWKV_WIDTH), F32)],
        compiler_params=_params(2, _vmem_limit(blocks, scratch, 64 * _nbytes((tt, RWKV_WIDTH), F32))),
        name=name,
    )(z_rwkv, shift0, s0, lp["mu"], lp["w0"], lp["w2cat"], lp["a0"], lp["g_w2"],
      lp["k_k"], lp["k_a"], lp["r_k"], lp["lnx_w"], lp["lnx_b"])
    return out, s_fin


def _oproj_kernel(a_ref, r_ref, wa_ref, wr_ref, x_ref, g_ref, h_ref, n_ref):
    h = (x_ref[...] + jnp.dot(a_ref[...], wa_ref[...], preferred_element_type=F32)
         + jnp.dot(r_ref[...], wr_ref[...], preferred_element_type=F32))
    h_ref[...] = h
    n_ref[...] = _rms(h, g_ref[...]).astype(BF16)


def _oproj(attn, rwkv, w_o, x, g, *, tm, name):
    t = x.shape[0]
    assert t % tm == 0
    half = (ATTN_WIDTH, D_MODEL)
    blocks = (2 * _nbytes((tm, ATTN_WIDTH), BF16) + 2 * _nbytes(half, BF16) + 2 * _nbytes((tm, D_MODEL), F32)
              + _nbytes((tm, D_MODEL), BF16))
    return pl.pallas_call(
        _oproj_kernel,
        out_shape=(jax.ShapeDtypeStruct((t, D_MODEL), F32), jax.ShapeDtypeStruct((t, D_MODEL), BF16)),
        grid=(t // tm,),
        in_specs=[pl.BlockSpec((tm, ATTN_WIDTH), lambda i: (i, 0)),
                  pl.BlockSpec((tm, RWKV_WIDTH), lambda i: (i, 0)),
                  pl.BlockSpec(half, lambda i: (0, 0)),
                  pl.BlockSpec(half, lambda i: (1, 0)),
                  pl.BlockSpec((tm, D_MODEL), lambda i: (i, 0)),
                  pl.BlockSpec((1, D_MODEL), lambda i: (0, 0))],
        out_specs=(pl.BlockSpec((tm, D_MODEL), lambda i: (i, 0)), pl.BlockSpec((tm, D_MODEL), lambda i: (i, 0))),
        compiler_params=_params(1, _vmem_limit(blocks, 0, 2 * _nbytes((tm, D_MODEL), F32))),
        name=name,
    )(attn, rwkv, w_o, w_o, x, g)


def _ffn_kernel(n_ref, h_ref, wg_ref, wu_ref, wd_ref, g_ref, o_ref):
    j = pl.program_id(1)

    @pl.when(j == 0)
    def _():
        o_ref[...] = h_ref[...]

    n = n_ref[...]
    gate = jnp.dot(n, wg_ref[...], preferred_element_type=F32)
    up = jnp.dot(n, wu_ref[...], preferred_element_type=F32)
    act = (gate * _sigmoid(gate) * up).astype(BF16)
    o_ref[...] += jnp.dot(act, wd_ref[...], preferred_element_type=F32)

    @pl.when(j == pl.num_programs(1) - 1)
    def _():
        o_ref[...] = _rms(o_ref[...], g_ref[...])


def _ffn(n2, h, wg, wu, wd, g, *, tm, tf, name):
    t = h.shape[0]
    assert t % tm == 0 and D_FF % tf == 0
    blocks = (_nbytes((tm, D_MODEL), BF16) + 2 * _nbytes((tm, D_MODEL), F32) + 3 * _nbytes((D_MODEL, tf), BF16))
    temps = 3 * _nbytes((tm, tf), F32) + _nbytes((tm, tf), BF16)
    return pl.pallas_call(
        _ffn_kernel,
        out_shape=jax.ShapeDtypeStruct((t, D_MODEL), F32),
        grid=(t // tm, D_FF // tf),
        in_specs=[pl.BlockSpec((tm, D_MODEL), lambda i, j: (i, 0)),
                  pl.BlockSpec((tm, D_MODEL), lambda i, j: (i, 0)),
                  pl.BlockSpec((D_MODEL, tf), lambda i, j: (0, j)),
                  pl.BlockSpec((D_MODEL, tf), lambda i, j: (0, j)),
                  pl.BlockSpec((tf, D_MODEL), lambda i, j: (j, 0)),
                  pl.BlockSpec((1, D_MODEL), lambda i, j: (0, 0))],
        out_specs=pl.BlockSpec((tm, D_MODEL), lambda i, j: (i, 0)),
        compiler_params=_params(2, _vmem_limit(blocks, 0, temps)),
        name=name,
    )(n2, h, wg, wu, wd, g)


def _trunk(x, lp, wts, attn_fn, shift0, s0, *, nb, seq, tag):
    t = nb * seq
    tm = min(t, 1024)
    z_attn = _inproj(x, lp["ln1"], wts["w_in"], col0=0, n=3 * ATTN_WIDTH, tm=tm, tn=1024,
                     name=f"inproj_attn_{tag}")
    z_rwkv = _inproj(x, lp["ln1"], wts["w_in"], col0=3 * ATTN_WIDTH, n=SHIFT_W, tm=tm, tn=SHIFT_W // 2,
                     name=f"inproj_rwkv_{tag}")
    attn = attn_fn(z_attn)
    rwkv, s_fin = _rwkv(z_rwkv, shift0, s0, lp, nb=nb, seq=seq, name=f"rwkv_{tag}")
    tm2 = min(t, 512)
    h, n2 = _oproj(attn, rwkv, wts["w_o"], x, lp["ln2"], tm=tm2, name=f"oproj_{tag}")
    y = _ffn(n2, h, wts["w_gate"], wts["w_up"], wts["w_down"], lp["final_norm"], tm=min(t, 1024), tf=512,
             name=f"ffn_{tag}")
    return y, z_attn, z_rwkv, s_fin


def kernel(x_prompt, x_sample, cache_attn_k, cache_attn_v, state_rwkv_wkv, state_rwkv_shift, ln1, w_in, rel_table,
           mu, w0, w_w2, a0, a_w2, g_w2, k_k, k_a, r_k, lnx_w, lnx_b, w_o, ln2, w_gate, w_up, w_down, final_norm):
    depth = w_in.shape[0]
    assert depth == 1, "final RMSNorm is fused into the (single) layer's FFN call"
    b, t, _ = x_prompt.shape
    bd, s, _ = x_sample.shape
    past = cache_attn_k.shape[2]
    n_keep = min(BAND_PAST, t)
    l = 0
    row = lambda p: p[l].reshape(1, -1)
    lp = dict(ln1=row(ln1), mu=row(mu), w0=row(w0), a0=row(a0),
              w2cat=jnp.concatenate([w_w2[l], a_w2[l]], axis=0).astype(BF16), g_w2=g_w2[l].astype(BF16),
              k_k=row(k_k), k_a=row(k_a), r_k=row(r_k), lnx_w=row(lnx_w), lnx_b=row(lnx_b), ln2=row(ln2),
              final_norm=final_norm.reshape(1, -1))
    wts = dict(w_in=w_in[l].astype(BF16),
               w_o=w_o[l].astype(BF16), w_gate=w_gate[l].astype(BF16), w_up=w_up[l].astype(BF16),
               w_down=w_down[l].astype(BF16))
    table = rel_table[l]

    bias_p = _bias_tiles(table, mq=CHUNK, off=BAND_PAST + CHUNK, col_lo=CHUNK, col_hi=KEY_WIN, name="bias_prompt")
    bias_s = _bias_tiles(table, mq=s, off=past, col_lo=0, col_hi=past + s, name="bias_sample")

    assert b == 1
    yp, zap, zrp, wkv_p = _trunk(
        x_prompt.reshape(b * t, D_MODEL), lp, wts, lambda z: _attn_prompt(z, bias_p),
        jnp.zeros((b, 1, SHIFT_W), F32), jnp.zeros((b, N_HEADS, HEAD_DIM, HEAD_DIM), F32), nb=b, seq=t, tag="p")
    kc = cache_attn_k[l].reshape(bd, past, ATTN_WIDTH)
    vc = cache_attn_v[l].reshape(bd, past, ATTN_WIDTH)
    ys, zas, zrs, wkv_s = _trunk(
        x_sample.reshape(bd * s, D_MODEL), lp, wts, lambda z: _attn_sample(z, kc, vc, bias_s, nb=bd, s=s),
        state_rwkv_shift[l], state_rwkv_wkv[l], nb=bd, seq=s, tag="s")

    heads = lambda z, n, tt: z.reshape(n, tt, N_HEADS, HEAD_DIM)
    zap = zap.reshape(b, t, 3 * ATTN_WIDTH)
    zas = zas.reshape(bd, s, 3 * ATTN_WIDTH)
    k_p = heads(zap[:, t - n_keep:, ATTN_WIDTH:2 * ATTN_WIDTH], b, n_keep)
    v_p = heads(zap[:, t - n_keep:, 2 * ATTN_WIDTH:], b, n_keep)
    k_s = heads(zas[:, :, ATTN_WIDTH:2 * ATTN_WIDTH], bd, s)
    v_s = heads(zas[:, :, 2 * ATTN_WIDTH:], bd, s)
    sh_p = zrp.reshape(b, t, SHIFT_W)[:, t - 1:]
    sh_s = zrs.reshape(bd, s, SHIFT_W)[:, s - 1:]
    return (yp.reshape(b, t, D_MODEL), ys.reshape(bd, s, D_MODEL),
            k_p[None], v_p[None], wkv_p[None], sh_p[None],
            k_s[None], v_s[None], wkv_s[None], sh_s[None])
```

```python
import functools
import math

import jax
import jax.numpy as jnp
from jax import lax
from jax.experimental import pallas as pl
from jax.experimental.pallas import tpu as pltpu

F32 = jnp.float32
BF16 = jnp.bfloat16

D_MODEL = 2048
CHUNK = 64
N_PAST_CHUNKS = 8
BAND_PAST = N_PAST_CHUNKS * CHUNK
ATTN_WIDTH = 1024
HEAD_DIM = 64
N_HEADS = 16
REL_CLIP = 128
N_REL = 2 * REL_CLIP + 1
RWKV_WIDTH = 1024
DECAY_RANK = 64
AAA_RANK = 64
GATE_RANK = 128
SHIFT_W = 3 * RWKV_WIDTH + DECAY_RANK + AAA_RANK + GATE_RANK
D_FF = 5632
RMS_EPS = 1e-6
GN_EPS = 64e-5

V7X_LANES = 128
V7X_VMEM_BYTES = 64 * 1024 * 1024
V7X_SCOPED_VMEM_CAP_BYTES = 60000 * 1024

PAIR_W = 2 * HEAD_DIM
N_PAIRS = N_HEADS // 2
CHUNK_T = 64
KEY_WIN = (N_PAST_CHUNKS + 2) * CHUNK
NEG_BIG = -1e30
ATTN_QB = 512
ATTN_CHUNKS_PER_ITER = 2
ATTN_SOFTMAX_ROWS = 128
RWKV_CHUNKS_PER_STEP = 2

assert PAIR_W == V7X_LANES


def _vmem_limit(pipelined_bytes, scratch_bytes, temp_bytes):
    need = 2 * pipelined_bytes + scratch_bytes + temp_bytes
    return int(min(max(need, 16 * 1024 * 1024), V7X_SCOPED_VMEM_CAP_BYTES))


def _nbytes(shape, dtype):
    return math.prod(shape) * jnp.dtype(dtype).itemsize


def _params(n_grid, vmem):
    return pltpu.CompilerParams(dimension_semantics=("arbitrary",) * n_grid, vmem_limit_bytes=vmem)


def _rms(x, g):
    return x * lax.rsqrt(jnp.mean(x * x, axis=-1, keepdims=True) + RMS_EPS) * g


def _inproj_kernel(x_ref, g_ref, w_ref, o_ref, n_ref):
    @pl.when(pl.program_id(1) == 0)
    def _():
        n_ref[...] = _rms(x_ref[...], g_ref[...]).astype(BF16)

    o_ref[...] = jnp.dot(n_ref[...], w_ref[...], preferred_element_type=F32)


def _inproj(x, g, w, *, col0, n, tm, tn, name):
    t, d = x.shape
    assert t % tm == 0 and n % tn == 0 and col0 % V7X_LANES == 0 and col0 + n <= w.shape[1]
    blocks = _nbytes((tm, d), F32) + _nbytes((d, tn), BF16) + _nbytes((tm, tn), F32)
    return pl.pallas_call(
        _inproj_kernel,
        out_shape=jax.ShapeDtypeStruct((t, n), F32),
        grid=(t // tm, n // tn),
        in_specs=[pl.BlockSpec((tm, d), lambda i, j: (i, 0)),
                  pl.BlockSpec((1, d), lambda i, j: (0, 0)),
                  pl.BlockSpec((pl.Element(d), pl.Element(tn)),
                               lambda i, j: (0, (col0 // V7X_LANES + j * (tn // V7X_LANES)) * V7X_LANES))],
        out_specs=pl.BlockSpec((tm, tn), lambda i, j: (i, j)),
        scratch_shapes=[pltpu.VMEM((tm, d), BF16)],
        compiler_params=_params(2, _vmem_limit(blocks, _nbytes((tm, d), BF16), 2 * _nbytes((tm, d), F32))),
        name=name,
    )(x, g, w)


BIAS_VAR_COL0 = KEY_WIN - 2 * V7X_LANES
BIAS_DIAG_W = 512


def _bias_kernel(tab_ref, o_ref, *, mq, off, col_lo, col_hi):
    nvar = KEY_WIN - BIAS_VAR_COL0
    w = BIAS_DIAG_W
    assert nvar + mq <= w
    lane = lax.broadcasted_iota(jnp.int32, (1, w), 1)
    idx = jnp.clip(off - (BIAS_VAR_COL0 - mq + lane), -REL_CLIP, REL_CLIP) + REL_CLIP
    idx_min = max(0, min(REL_CLIP, off - (BIAS_VAR_COL0 - mq + w - 1)) + REL_CLIP)
    heads = range(N_HEADS)

    def body(j, profs):
        hit = idx == j
        return tuple(jnp.where(hit, tab_ref[h, j], prof) for h, prof in zip(heads, profs))

    init = tuple(jnp.full((1, w), tab_ref[h, N_REL - 1], F32) for h in heads)
    profs = lax.fori_loop(idx_min, N_REL - 1, body, init)

    c = lax.broadcasted_iota(jnp.int32, (mq, nvar), 1) + BIAS_VAR_COL0
    valid_var = (c >= col_lo) & (c < col_hi)
    c_far = lax.broadcasted_iota(jnp.int32, (mq, BIAS_VAR_COL0), 1)
    valid_far = (c_far >= col_lo) & (c_far < col_hi)
    for h in heads:
        tile = pltpu.roll(jnp.broadcast_to(profs[h], (mq, w)), w - mq, axis=1, stride=1, stride_axis=0)
        rows = slice((h % 2) * mq, (h % 2 + 1) * mq)
        far = jnp.full((mq, BIAS_VAR_COL0), tab_ref[h, N_REL - 1], F32)
        o_ref[h // 2, rows, 0:BIAS_VAR_COL0] = jnp.where(valid_far, far, NEG_BIG)
        o_ref[h // 2, rows, BIAS_VAR_COL0:KEY_WIN] = jnp.where(valid_var, tile[:, 0:nvar], NEG_BIG)


def _bias_tiles(table, *, mq, off, col_lo, col_hi, name):
    assert off - (BIAS_VAR_COL0 - 1) >= REL_CLIP
    return pl.pallas_call(
        functools.partial(_bias_kernel, mq=mq, off=off, col_lo=col_lo, col_hi=col_hi),
        out_shape=jax.ShapeDtypeStruct((N_PAIRS, 2 * mq, KEY_WIN), F32),
        in_specs=[pl.BlockSpec(memory_space=pltpu.SMEM)],
        out_specs=pl.BlockSpec(memory_space=pltpu.VMEM),
        name=name,
    )(table)


def _lane_lo(rows):
    return lax.broadcasted_iota(jnp.int32, (rows, PAIR_W), 1) < HEAD_DIM


def _attend(q_pair, k_pair, v_pair, bias_ref, o_put, s_scr, p_scr, *, mq, n_win, min_col):
    lo = _lane_lo(mq)
    units = [(w, p) for w in range(n_win) for p in range(N_PAIRS)]
    for u, (w, p) in enumerate(units):
        qs = q_pair(w, p) * (HEAD_DIM ** -0.5)
        q2 = jnp.concatenate([jnp.where(lo, qs, 0.0), jnp.where(lo, 0.0, qs)], axis=0).astype(BF16)
        s = lax.dot_general(q2, k_pair(w, p), (((1,), (1,)), ((), ())), preferred_element_type=F32) + bias_ref[p]
        first = min_col(w)
        if first is not None:
            col = lax.broadcasted_iota(jnp.int32, (1, KEY_WIN), 1)
            s = jnp.where(col >= first, s, NEG_BIG)
        s_scr[u] = s
    inv_l = []
    rb = min(2 * mq, ATTN_SOFTMAX_ROWS)
    for u in range(len(units)):
        parts = []
        for r0 in range(0, 2 * mq, rb):
            s = s_scr[u, r0:r0 + rb, :]
            e = jnp.exp(s - jnp.max(s, axis=-1, keepdims=True))
            parts.append(1.0 / jnp.sum(e, axis=-1, keepdims=True))
            p_scr[u, r0:r0 + rb, :] = e.astype(BF16)
        inv_l.append(jnp.concatenate(parts, axis=0))
    for u, (w, p) in enumerate(units):
        pv = jnp.dot(p_scr[u], v_pair(w, p), preferred_element_type=F32) * inv_l[u]
        o_put(w, p, jnp.where(lo, pv[:mq], pv[mq:]))


def _attn_prompt_kernel(q_ref, kc_ref, vc_ref, bias_ref, o_ref, kbuf, vbuf, s_scr, p_scr):
    i = pl.program_id(0)
    prev0, cur0, end = CHUNK, CHUNK + ATTN_QB, CHUNK + 2 * ATTN_QB

    @pl.when(i == 0)
    def _():
        zeros = jnp.zeros((cur0, ATTN_WIDTH), BF16)
        kbuf[0:cur0, :] = zeros
        vbuf[0:cur0, :] = zeros

    kbuf[cur0:end, :] = kc_ref[...].astype(BF16)
    vbuf[cur0:end, :] = vc_ref[...].astype(BF16)
    chunks_per_step = ATTN_QB // CHUNK
    lead_chunks = N_PAST_CHUNKS + 1
    n_win = ATTN_CHUNKS_PER_ITER

    def iter_body(j, carry, *, masked):
        sl = lambda p: slice(p * PAIR_W, (p + 1) * PAIR_W)
        r0 = lambda w: pl.multiple_of((j * n_win + w) * CHUNK, CHUNK)
        min_col = lambda w: (lead_chunks - (i * chunks_per_step + j * n_win + w)) * CHUNK if masked else None

        def o_put(w, p, o):
            o_ref[pl.ds(r0(w), CHUNK), sl(p)] = o.astype(BF16)

        _attend(lambda w, p: q_ref[pl.ds(r0(w), CHUNK), sl(p)], lambda w, p: kbuf[pl.ds(r0(w), KEY_WIN), sl(p)],
                lambda w, p: vbuf[pl.ds(r0(w), KEY_WIN), sl(p)], bias_ref, o_put, s_scr, p_scr, mq=CHUNK,
                n_win=n_win, min_col=min_col)
        return carry

    masked_steps = -(-lead_chunks // chunks_per_step)

    @pl.when(i < masked_steps)
    def _():
        lax.fori_loop(0, chunks_per_step // n_win, functools.partial(iter_body, masked=True), 0)

    @pl.when(i >= masked_steps)
    def _():
        lax.fori_loop(0, chunks_per_step // n_win, functools.partial(iter_body, masked=False), 0)

    kbuf[prev0:cur0, :] = kbuf[cur0:end, :]
    vbuf[prev0:cur0, :] = vbuf[cur0:end, :]


def _attn_prompt(z_attn, bias):
    t = z_attn.shape[0]
    assert t % ATTN_QB == 0 and (ATTN_QB // CHUNK) % ATTN_CHUNKS_PER_ITER == 0 and ATTN_QB >= BAND_PAST
    blk = (ATTN_QB, ATTN_WIDTH)
    buf_rows = CHUNK + 2 * ATTN_QB
    s_shape = (ATTN_CHUNKS_PER_ITER * N_PAIRS, 2 * CHUNK, KEY_WIN)
    blocks = 3 * _nbytes(blk, F32) + _nbytes(bias.shape, F32) + _nbytes(blk, BF16)
    scratch = 2 * _nbytes((buf_rows, ATTN_WIDTH), BF16) + _nbytes(s_shape, F32) + _nbytes(s_shape, BF16)
    return pl.pallas_call(
        _attn_prompt_kernel,
        out_shape=jax.ShapeDtypeStruct((t, ATTN_WIDTH), BF16),
        grid=(t // ATTN_QB,),
        in_specs=[pl.BlockSpec(blk, lambda i: (i, 0)),
                  pl.BlockSpec(blk, lambda i: (i, 1)),
                  pl.BlockSpec(blk, lambda i: (i, 2)),
                  pl.BlockSpec(bias.shape, lambda i: (0, 0, 0))],
        out_specs=pl.BlockSpec(blk, lambda i: (i, 0)),
        scratch_shapes=[pltpu.VMEM((buf_rows, ATTN_WIDTH), BF16), pltpu.VMEM((buf_rows, ATTN_WIDTH), BF16),
                        pltpu.VMEM(s_shape, F32), pltpu.VMEM(s_shape, BF16)],
        compiler_params=_params(1, _vmem_limit(blocks, scratch, 4 * _nbytes((2 * CHUNK, KEY_WIN), F32))),
        name="attn_prompt",
    )(z_attn, z_attn, z_attn, bias)


def _attn_sample_kernel(q_ref, kn_ref, vn_ref, kc_ref, vc_ref, bias_ref, o_ref, kbuf, vbuf, s_scr, p_scr, *,
                        s, past):
    tail = jnp.zeros((KEY_WIN - past - s, ATTN_WIDTH), BF16)
    kbuf[0:past, :] = kc_ref[...].astype(BF16)
    vbuf[0:past, :] = vc_ref[...].astype(BF16)
    kbuf[past:past + s, :] = kn_ref[...].astype(BF16)
    vbuf[past:past + s, :] = vn_ref[...].astype(BF16)
    kbuf[past + s:KEY_WIN, :] = tail
    vbuf[past + s:KEY_WIN, :] = tail
    sl = lambda p: slice(p * PAIR_W, (p + 1) * PAIR_W)

    def o_put(w, p, o):
        o_ref[:, sl(p)] = o.astype(BF16)

    _attend(lambda w, p: q_ref[:, sl(p)], lambda w, p: kbuf[:, sl(p)], lambda w, p: vbuf[:, sl(p)], bias_ref,
            o_put, s_scr, p_scr, mq=s, n_win=1, min_col=lambda w: None)


def _attn_sample(z_attn, k_cache, v_cache, bias, *, nb, s):
    past = k_cache.shape[1]
    assert past + s <= KEY_WIN and z_attn.shape[0] == nb * s
    blk = (s, ATTN_WIDTH)
    cblk = (None, past, ATTN_WIDTH)
    s_shape = (N_PAIRS, 2 * s, KEY_WIN)
    blocks = (3 * _nbytes(blk, F32) + 2 * _nbytes((past, ATTN_WIDTH), F32) + _nbytes(bias.shape, F32)
              + _nbytes(blk, BF16))
    scratch = 2 * _nbytes((KEY_WIN, ATTN_WIDTH), BF16) + _nbytes(s_shape, F32) + _nbytes(s_shape, BF16)
    return pl.pallas_call(
        functools.partial(_attn_sample_kernel, s=s, past=past),
        out_shape=jax.ShapeDtypeStruct((nb * s, ATTN_WIDTH), BF16),
        grid=(nb,),
        in_specs=[pl.BlockSpec(blk, lambda b: (b, 0)),
                  pl.BlockSpec(blk, lambda b: (b, 1)),
                  pl.BlockSpec(blk, lambda b: (b, 2)),
                  pl.BlockSpec(cblk, lambda b: (b, 0, 0)),
                  pl.BlockSpec(cblk, lambda b: (b, 0, 0)),
                  pl.BlockSpec(bias.shape, lambda b: (0, 0, 0))],
        out_specs=pl.BlockSpec(blk, lambda b: (b, 0)),
        scratch_shapes=[pltpu.VMEM((KEY_WIN, ATTN_WIDTH), BF16), pltpu.VMEM((KEY_WIN, ATTN_WIDTH), BF16),
                        pltpu.VMEM(s_shape, F32), pltpu.VMEM(s_shape, BF16)],
        compiler_params=_params(1, _vmem_limit(blocks, scratch, 4 * _nbytes((2 * s, KEY_WIN), F32))),
        name="attn_sample",
    )(z_attn, z_attn, z_attn, k_cache, v_cache, bias)


def _bf(x):
    return x.astype(BF16)


def _split2(x):
    hi = _bf(x)
    return hi, _bf(x - hi.astype(F32))


_NN = (((1,), (0,)), ((), ()))
_NT = (((1,), (1,)), ((), ()))


def _mm(x, y, dims=_NN):
    return lax.dot_general(x, y, dims, preferred_element_type=F32)


def _bd(y, lo):
    zero = jnp.zeros_like(y)
    return jnp.concatenate([jnp.where(lo, y, zero), jnp.where(lo, zero, y)], axis=0)


def _seg_sum(x, ones_seg, *, passes):
    r = x.shape[0]
    half = x.shape[1] // (2 * PAIR_W)
    slab = lambda p: x[:, p * PAIR_W:(p + 1) * PAIR_W]
    xx = jnp.concatenate([jnp.concatenate([slab(p) for p in range(half)], axis=0),
                          jnp.concatenate([slab(half + p) for p in range(half)], axis=0)], axis=1)
    hi, lo = _split2(xx)
    s = _mm(hi, ones_seg)
    if passes == 2:
        s = s + _mm(lo, ones_seg)
    return jnp.concatenate([s[p * r:(p + 1) * r, 0:PAIR_W] for p in range(half)]
                           + [s[p * r:(p + 1) * r, PAIR_W:2 * PAIR_W] for p in range(half)], axis=1)


def _sigmoid(x):
    return 1.0 / (1.0 + jnp.exp(-x))


def _sigmoid_tanh(x):
    return 0.5 + 0.5 * jnp.tanh(0.5 * x)


_TOK_F32 = ("at", "rt", "v", "bonus", "g")
_TOK_BF16 = ("at", "bt", "kt", "rt", "v", "bp", "kp")


def _rwkv_kernel(zf_ref, zn_ref, sh_ref, s0_ref, mu_ref, w0_ref, w2cat_ref, a0_ref, gw2_ref, kk_ref, ka_ref,
                 rk_ref, lnw_ref, lnb_ref, o_ref, sfin_ref, s_scr, carry_scr, y_scr, f32_scr, bf_scr, pc_scr, *,
                 rows, cps, nck):
    ct = CHUNK_T
    pw = PAIR_W
    tt = cps * ct
    assert rows == ct or cps == 1
    c = pl.program_id(1)

    seg = 2 * pw
    ones_seg = _bf(lax.broadcasted_iota(jnp.int32, (seg, seg), 0) // HEAD_DIM
                   == lax.broadcasted_iota(jnp.int32, (seg, seg), 1) // HEAD_DIM)
    row = lax.broadcasted_iota(jnp.int32, (ct, 1), 0)
    row_all = lax.broadcasted_iota(jnp.int32, (tt, 1), 0)
    tri = _bf(lax.broadcasted_iota(jnp.int32, (ct, ct), 1) <= lax.broadcasted_iota(jnp.int32, (ct, ct), 0))
    t_lane = lax.broadcasted_iota(jnp.int32, (ct, 2 * ct), 1) % ct
    strict = t_lane < row
    incl = t_lane <= row
    lo_t = _lane_lo(ct)
    lo_p = lax.broadcasted_iota(jnp.int32, (ct, 2 * ct), 1) < ct
    lo_t2 = lax.broadcasted_iota(jnp.int32, (ct, 2 * pw), 1) % pw < HEAD_DIM
    diag = (lax.broadcasted_iota(jnp.int32, (HEAD_DIM, pw), 1) % HEAD_DIM
            == lax.broadcasted_iota(jnp.int32, (HEAD_DIM, pw), 0))
    lo_s = _lane_lo(HEAD_DIM)
    n_dbl = int(math.log2(ct))
    assert 1 << n_dbl == ct and 2 * ct == pw
    pairs = range(N_PAIRS)
    cat0 = lambda xs: jnp.concatenate(xs, axis=0)
    cat1 = lambda xs: jnp.concatenate(xs, axis=1)

    def tn(xv, yv):
        zz = _mm(_bf(xv.T), yv)
        return jnp.where(lo_s, zz[:HEAD_DIM], zz[HEAD_DIM:])

    def chunk_pre(zc):
        r = zc[:, 0:RWKV_WIDTH]
        k = zc[:, RWKV_WIDTH:2 * RWKV_WIDTH]
        v = zc[:, 2 * RWKV_WIDTH:3 * RWKV_WIDTH]
        zwa = zc[:, 3 * RWKV_WIDTH:3 * RWKV_WIDTH + DECAY_RANK + AAA_RANK]
        zg = zc[:, 3 * RWKV_WIDTH + DECAY_RANK + AAA_RANK:SHIFT_W]
        lora_in = cat0([jnp.where(lo_t, jnp.tanh(zwa), 0.0), jnp.where(lo_t, 0.0, zwa)])
        lora = _mm(_bf(lora_in), w2cat_ref[...])
        yield
        lw = -math.exp(-0.5) * _sigmoid_tanh(w0_ref[...] + lora[:ct])
        if rows < ct:
            live = row < rows
            lw = jnp.where(live, lw, 0.0)
        l_hi, l_lo = _split2(lw)
        cl = _mm(tri, l_hi) + _mm(tri, l_lo)
        yield
        a = _sigmoid_tanh(a0_ref[...] + lora[ct:])
        g = _mm(_bf(_sigmoid_tanh(zg)), gw2_ref[...])
        yield
        kkr = k * kk_ref[...]
        kn = kkr * lax.rsqrt(jnp.maximum(_seg_sum(kkr * kkr, ones_seg, passes=1), 1e-24))
        yield
        k2 = k * (1.0 + (a - 1.0) * ka_ref[...])
        bonus = _seg_sum(r * k2 * rk_ref[...], ones_seg, passes=1) * v
        yield
        e_in = jnp.exp(cl)
        rt = r * e_in
        pc = e_in[ct - 1:ct, :]
        yield
        at = -kn * jnp.exp(cl - lw)
        yield
        e_inv = jnp.exp(-cl)
        bt = kn * a * e_inv
        kt = k2 * e_inv
        if rows < ct:
            bt = jnp.where(live, bt, 0.0)
            kt = jnp.where(live, kt, 0.0)
        yield
        return dict(at=at, bt=bt, kt=kt, rt=rt, v=v, bp=bt * pc, kp=kt * pc, pc=pc, bonus=bonus, g=g)

    def pre_block(z_ref, carry, wr):
        z = z_ref[...]
        if rows < ct:
            z = jnp.concatenate([z, jnp.zeros((ct - rows, SHIFT_W), F32)], axis=0)
        zprev = jnp.where(row_all == 0, carry, pltpu.roll(z, 1, axis=0))
        last = (cps - 1) * ct + rows - 1
        carry_scr[0:1, :] = z[last:last + 1, :]
        zs = z + (zprev - z) * mu_ref[...]
        yield
        for gi in range(cps):
            t = yield from chunk_pre(zs[gi * ct:(gi + 1) * ct])
            rs = slice(gi * ct, (gi + 1) * ct)
            for i, name in enumerate(_TOK_F32):
                f32_scr[wr, i, rs, :] = t[name]
            yield
            for i, name in enumerate(_TOK_BF16):
                bf_scr[wr, i, rs, :] = _bf(t[name])
            pc_scr[wr, gi:gi + 1, :] = t["pc"]
            yield

    def chunk_stages(rd, gi):
        rs = slice(gi * ct, (gi + 1) * ct)
        f32 = lambda name, p: f32_scr[rd, _TOK_F32.index(name), rs, p * pw:(p + 1) * pw]
        b16 = lambda name, p: bf_scr[rd, _TOK_BF16.index(name), rs, p * pw:(p + 1) * pw]
        v_bd = [_bd(b16("v", p), lo_t) for p in pairs]
        amat = [_mm(cat0([b16("at", p), b16("rt", p)]), cat0([_bd(b16("bt", p), lo_t), _bd(b16("kt", p), lo_t)]),
                    _NT) for p in pairs]
        yield
        a_ab = [jnp.where(strict, amat[p][:ct, :pw], 0.0) for p in pairs]
        a_ak = [_bf(jnp.where(strict, amat[p][:ct, pw:], 0.0)) for p in pairs]
        a_rb = [_bf(jnp.where(incl, amat[p][ct:, :pw], 0.0)) for p in pairs]
        a_rk = [_bf(jnp.where(incl, amat[p][ct:, pw:], 0.0)) for p in pairs]
        x = [cat1([f32("at", p), _mm(a_ak[p], v_bd[p])]) for p in pairs]
        npow = [_bf(a_ab[p]) for p in pairs]
        yield
        for it in range(n_dbl):
            if it + 1 < n_dbl:
                res = [_mm(npow[p], cat1([_bd(_bf(x[p]), lo_t2), _bd(npow[p], lo_p)])) for p in pairs]
                npow = [_bf(res[p][:, 2 * pw:]) for p in pairs]
            else:
                res = [_mm(npow[p], _bd(_bf(x[p]), lo_t2)) for p in pairs]
            x = [x[p] + res[p][:, :2 * pw] for p in pairs]
            yield
        ry = [_mm(a_rb[p], _bd(_bf(x[p]), lo_t2)) for p in pairs]
        rp = [_bf(f32("rt", p) + ry[p][:, :pw]) for p in pairs]
        y0 = [ry[p][:, pw:] + _mm(a_rk[p], v_bd[p]) for p in pairs]
        yield
        m_mat = [_bf(jnp.where(diag, pc_scr[rd, gi:gi + 1, p * pw:(p + 1) * pw], 0.0)
                     + tn(x[p][:, :pw], b16("bp", p))) for p in pairs]
        yield
        n_mat = [tn(cat0([x[p][:, pw:], f32("v", p)]), cat0([b16("bp", p), b16("kp", p)])) for p in pairs]
        return rp, y0, m_mat, n_mat

    def lockstep(*gens):
        done = [None] * len(gens)
        live = list(range(len(gens)))
        while live:
            for i in list(live):
                try:
                    next(gens[i])
                except StopIteration as stop:
                    done[i] = stop.value
                    live.remove(i)
            yield
        return done

    def main_block(rd):
        terms = yield from lockstep(*[chunk_stages(rd, gi) for gi in range(cps)])
        state = [s_scr[:, p * pw:(p + 1) * pw] for p in pairs]
        for gi, (rp, y0, m_mat, n_mat) in enumerate(terms):
            for p in pairs:
                s_old = _bf(state[p])
                y_scr[gi * ct:(gi + 1) * ct, p * pw:(p + 1) * pw] = _mm(rp[p], _bd(s_old, lo_s), _NT) + y0[p]
                state[p] = _mm(s_old, _bd(m_mat[p], lo_s)) + n_mat[p]
            yield
        for p in pairs:
            s_scr[:, p * pw:(p + 1) * pw] = state[p]
        y = y_scr[...]
        mean = _seg_sum(y, ones_seg, passes=2) * (1.0 / HEAD_DIM)
        dv = y - mean
        yield
        var = _seg_sum(dv * dv, ones_seg, passes=1) * (1.0 / HEAD_DIM)
        yn = dv * lax.rsqrt(var + GN_EPS) * lnw_ref[...] + lnb_ref[...]
        out = (yn + f32_scr[rd, _TOK_F32.index("bonus")]) * f32_scr[rd, _TOK_F32.index("g")]
        o_ref[...] = out[:(cps - 1) * ct + rows].astype(BF16)

    def run(*gens):
        for _ in lockstep(*gens):
            pass

    @pl.when(c == 0)
    def _():
        for p in pairs:
            s_scr[:, p * pw:(p + 1) * pw] = jnp.concatenate([s0_ref[2 * p], s0_ref[2 * p + 1]], axis=1)
        run(pre_block(zf_ref, sh_ref[...], 0))

    if nck == 1:
        run(main_block(0))
    else:
        for parity in (0, 1):
            @pl.when(c % 2 == parity)
            def _(parity=parity):
                run(main_block(parity), pre_block(zn_ref, carry_scr[0:1, :], 1 - parity))

    @pl.when(c == nck - 1)
    def _():
        for h in range(N_HEADS):
            sfin_ref[h] = s_scr[:, h * HEAD_DIM:(h + 1) * HEAD_DIM]


def _rwkv(z_rwkv, shift0, s0, lp, *, nb, seq, name):
    rows = min(seq, CHUNK_T)
    cps = min(seq // rows, RWKV_CHUNKS_PER_STEP)
    blk_rows = cps * rows
    assert seq % blk_rows == 0 and z_rwkv.shape == (nb * seq, SHIFT_W)
    nck = seq // blk_rows
    tt = cps * CHUNK_T
    row1 = lambda n: pl.BlockSpec((1, n), lambda b, c: (0, 0))
    full = lambda r, n: pl.BlockSpec((r, n), lambda b, c: (0, 0))
    st_blk = pl.BlockSpec((None, N_HEADS, HEAD_DIM, HEAD_DIM), lambda b, c: (b, 0, 0, 0))
    lora_rows = DECAY_RANK + AAA_RANK
    blocks = (2 * _nbytes((blk_rows, SHIFT_W), F32) + 2 * _nbytes((HEAD_DIM, RWKV_WIDTH), F32)
              + _nbytes((lora_rows + GATE_RANK, RWKV_WIDTH), BF16) + _nbytes((16, SHIFT_W), F32)
              + _nbytes((blk_rows, RWKV_WIDTH), BF16))
    stage_shapes = [((2, len(_TOK_F32), tt, RWKV_WIDTH), F32), ((2, len(_TOK_BF16), tt, RWKV_WIDTH), BF16),
                    ((2, 8, RWKV_WIDTH), F32)]
    scratch = (_nbytes((HEAD_DIM, RWKV_WIDTH), F32) + _nbytes((8, SHIFT_W), F32)
               + _nbytes((tt, RWKV_WIDTH), F32) + sum(_nbytes(s, d) for s, d in stage_shapes))
    assert cps <= 8
    out, s_fin = pl.pallas_call(
        functools.partial(_rwkv_kernel, rows=rows, cps=cps, nck=nck),
        out_shape=(jax.ShapeDtypeStruct((nb * seq, RWKV_WIDTH), BF16),
                   jax.ShapeDtypeStruct((nb, N_HEADS, HEAD_DIM, HEAD_DIM), F32)),
        grid=(nb, nck),
        in_specs=[pl.BlockSpec((blk_rows, SHIFT_W), lambda b, c: (b * nck, 0)),
                  pl.BlockSpec((blk_rows, SHIFT_W), lambda b, c: (b * nck + jnp.minimum(c + 1, nck - 1), 0)),
                  pl.BlockSpec((None, 1, SHIFT_W), lambda b, c: (b, 0, 0)),
                  st_blk,
                  row1(SHIFT_W), row1(RWKV_WIDTH), full(lora_rows, RWKV_WIDTH), row1(RWKV_WIDTH),
                  full(GATE_RANK, RWKV_WIDTH), row1(RWKV_WIDTH),
                  row1(RWKV_WIDTH), row1(RWKV_WIDTH), row1(RWKV_WIDTH), row1(RWKV_WIDTH)],
        out_specs=(pl.BlockSpec((blk_rows, RWKV_WIDTH), lambda b, c: (b * nck + c, 0)), st_blk),
        scratch_shapes=[pltpu.VMEM((HEAD_DIM, RWKV_WIDTH), F32), pltpu.VMEM((8, SHIFT_W), F32),
                        pltpu.VMEM((tt, RWKV_WIDTH), F32)] + [pltpu.VMEM(s, d) for s, d in stage_shapes],
        compiler_params=_params(2, _vmem_limit(blocks, scratch, 64 * _nbytes((tt, RWKV_WIDTH), F32))),
        name=name,
    )(z_rwkv, z_rwkv, shift0, s0, lp["mu"], lp["w0"], lp["w2cat"], lp["a0"], lp["g_w2"],
      lp["k_k"], lp["k_a"], lp["r_k"], lp["lnx_w"], lp["lnx_b"])
    return out, s_fin


def _oproj_kernel(a_ref, r_ref, wa_ref, wr_ref, x_ref, g_ref, h_ref, n_ref):
    h = (x_ref[...] + jnp.dot(a_ref[...], wa_ref[...], preferred_element_type=F32)
         + jnp.dot(r_ref[...], wr_ref[...], preferred_element_type=F32))
    h_ref[...] = h
    n_ref[...] = _rms(h, g_ref[...]).astype(BF16)


def _oproj(attn, rwkv, w_o, x, g, *, tm, name):
    t = x.shape[0]
    assert t % tm == 0
    half = (ATTN_WIDTH, D_MODEL)
    blocks = (2 * _nbytes((tm, ATTN_WIDTH), BF16) + 2 * _nbytes(half, BF16) + 2 * _nbytes((tm, D_MODEL), F32)
              + _nbytes((tm, D_MODEL), BF16))
    return pl.pallas_call(
        _oproj_kernel,
        out_shape=(jax.ShapeDtypeStruct((t, D_MODEL), F32), jax.ShapeDtypeStruct((t, D_MODEL), BF16)),
        grid=(t // tm,),
        in_specs=[pl.BlockSpec((tm, ATTN_WIDTH), lambda i: (i, 0)),
                  pl.BlockSpec((tm, RWKV_WIDTH), lambda i: (i, 0)),
                  pl.BlockSpec(half, lambda i: (0, 0)),
                  pl.BlockSpec(half, lambda i: (1, 0)),
                  pl.BlockSpec((tm, D_MODEL), lambda i: (i, 0)),
                  pl.BlockSpec((1, D_MODEL), lambda i: (0, 0))],
        out_specs=(pl.BlockSpec((tm, D_MODEL), lambda i: (i, 0)), pl.BlockSpec((tm, D_MODEL), lambda i: (i, 0))),
        compiler_params=_params(1, _vmem_limit(blocks, 0, 2 * _nbytes((tm, D_MODEL), F32))),
        name=name,
    )(attn, rwkv, w_o, w_o, x, g)


def _ffn_kernel(n_ref, h_ref, wg_ref, wu_ref, wd_ref, g_ref, o_ref):
    j = pl.program_id(1)

    @pl.when(j == 0)
    def _():
        o_ref[...] = h_ref[...]

    n = n_ref[...]
    gate = jnp.dot(n, wg_ref[...], preferred_element_type=F32)
    up = jnp.dot(n, wu_ref[...], preferred_element_type=F32)
    act = (gate * _sigmoid(gate) * up).astype(BF16)
    o_ref[...] += jnp.dot(act, wd_ref[...], preferred_element_type=F32)

    @pl.when(j == pl.num_programs(1) - 1)
    def _():
        o_ref[...] = _rms(o_ref[...], g_ref[...])


def _ffn(n2, h, wg, wu, wd, g, *, tm, tf, name):
    t = h.shape[0]
    assert t % tm == 0 and D_FF % tf == 0
    blocks = (_nbytes((tm, D_MODEL), BF16) + 2 * _nbytes((tm, D_MODEL), F32) + 3 * _nbytes((D_MODEL, tf), BF16))
    temps = 3 * _nbytes((tm, tf), F32) + _nbytes((tm, tf), BF16)
    return pl.pallas_call(
        _ffn_kernel,
        out_shape=jax.ShapeDtypeStruct((t, D_MODEL), F32),
        grid=(t // tm, D_FF // tf),
        in_specs=[pl.BlockSpec((tm, D_MODEL), lambda i, j: (i, 0)),
                  pl.BlockSpec((tm, D_MODEL), lambda i, j: (i, 0)),
                  pl.BlockSpec((D_MODEL, tf), lambda i, j: (0, j)),
                  pl.BlockSpec((D_MODEL, tf), lambda i, j: (0, j)),
                  pl.BlockSpec((tf, D_MODEL), lambda i, j: (j, 0)),
                  pl.BlockSpec((1, D_MODEL), lambda i, j: (0, 0))],
        out_specs=pl.BlockSpec((tm, D_MODEL), lambda i, j: (i, 0)),
        compiler_params=_params(2, _vmem_limit(blocks, 0, temps)),
        name=name,
    )(n2, h, wg, wu, wd, g)


def _trunk(x, lp, wts, attn_fn, shift0, s0, *, nb, seq, tag):
    t = nb * seq
    tm = min(t, 1024)
    z_attn = _inproj(x, lp["ln1"], wts["w_in"], col0=0, n=3 * ATTN_WIDTH, tm=tm, tn=1024,
                     name=f"inproj_attn_{tag}")
    z_rwkv = _inproj(x, lp["ln1"], wts["w_in"], col0=3 * ATTN_WIDTH, n=SHIFT_W, tm=tm, tn=SHIFT_W // 2,
                     name=f"inproj_rwkv_{tag}")
    attn = attn_fn(z_attn)
    rwkv, s_fin = _rwkv(z_rwkv, shift0, s0, lp, nb=nb, seq=seq, name=f"rwkv_{tag}")
    tm2 = min(t, 512)
    h, n2 = _oproj(attn, rwkv, wts["w_o"], x, lp["ln2"], tm=tm2, name=f"oproj_{tag}")
    y = _ffn(n2, h, wts["w_gate"], wts["w_up"], wts["w_down"], lp["final_norm"], tm=min(t, 1024), tf=512,
             name=f"ffn_{tag}")
    return y, z_attn, z_rwkv, s_fin


def kernel(x_prompt, x_sample, cache_attn_k, cache_attn_v, state_rwkv_wkv, state_rwkv_shift, ln1, w_in, rel_table,
           mu, w0, w_w2, a0, a_w2, g_w2, k_k, k_a, r_k, lnx_w, lnx_b, w_o, ln2, w_gate, w_up, w_down, final_norm):
    depth = w_in.shape[0]
    assert depth == 1, "final RMSNorm is fused into the (single) layer's FFN call"
    b, t, _ = x_prompt.shape
    bd, s, _ = x_sample.shape
    past = cache_attn_k.shape[2]
    n_keep = min(BAND_PAST, t)
    l = 0
    row = lambda p: p[l].reshape(1, -1)
    lp = dict(ln1=row(ln1), mu=row(mu), w0=row(w0), a0=row(a0),
              w2cat=jnp.concatenate([w_w2[l], a_w2[l]], axis=0).astype(BF16), g_w2=g_w2[l].astype(BF16),
              k_k=row(k_k), k_a=row(k_a), r_k=row(r_k), lnx_w=row(lnx_w), lnx_b=row(lnx_b), ln2=row(ln2),
              final_norm=final_norm.reshape(1, -1))
    wts = dict(w_in=w_in[l].astype(BF16),
               w_o=w_o[l].astype(BF16), w_gate=w_gate[l].astype(BF16), w_up=w_up[l].astype(BF16),
               w_down=w_down[l].astype(BF16))
    table = rel_table[l]

    bias_p = _bias_tiles(table, mq=CHUNK, off=BAND_PAST + CHUNK, col_lo=CHUNK, col_hi=KEY_WIN, name="bias_prompt")
    bias_s = _bias_tiles(table, mq=s, off=past, col_lo=0, col_hi=past + s, name="bias_sample")

    assert b == 1
    yp, zap, zrp, wkv_p = _trunk(
        x_prompt.reshape(b * t, D_MODEL), lp, wts, lambda z: _attn_prompt(z, bias_p),
        jnp.zeros((b, 1, SHIFT_W), F32), jnp.zeros((b, N_HEADS, HEAD_DIM, HEAD_DIM), F32), nb=b, seq=t, tag="p")
    kc = cache_attn_k[l].reshape(bd, past, ATTN_WIDTH)
    vc = cache_attn_v[l].reshape(bd, past, ATTN_WIDTH)
    ys, zas, zrs, wkv_s = _trunk(
        x_sample.reshape(bd * s, D_MODEL), lp, wts, lambda z: _attn_sample(z, kc, vc, bias_s, nb=bd, s=s),
        state_rwkv_shift[l], state_rwkv_wkv[l], nb=bd, seq=s, tag="s")

    heads = lambda z, n, tt: z.reshape(n, tt, N_HEADS, HEAD_DIM)
    zap = zap.reshape(b, t, 3 * ATTN_WIDTH)
    zas = zas.reshape(bd, s, 3 * ATTN_WIDTH)
    k_p = heads(zap[:, t - n_keep:, ATTN_WIDTH:2 * ATTN_WIDTH], b, n_keep)
    v_p = heads(zap[:, t - n_keep:, 2 * ATTN_WIDTH:], b, n_keep)
    k_s = heads(zas[:, :, ATTN_WIDTH:2 * ATTN_WIDTH], bd, s)
    v_s = heads(zas[:, :, 2 * ATTN_WIDTH:], bd, s)
    sh_p = zrp.reshape(b, t, SHIFT_W)[:, t - 1:]
    sh_s = zrs.reshape(bd, s, SHIFT_W)[:, s - 1:]
    return (yp.reshape(b, t, D_MODEL), ys.reshape(bd, s, D_MODEL),
            k_p[None], v_p[None], wkv_p[None], sh_p[None],
            k_s[None], v_s[None], wkv_s[None], sh_s[None])
```

```python
import functools
import math

import jax
import jax.numpy as jnp
from jax import lax
from jax.experimental import pallas as pl
from jax.experimental.pallas import tpu as pltpu

F32 = jnp.float32
BF16 = jnp.bfloat16

D_MODEL = 2048
CHUNK = 64
N_PAST_CHUNKS = 8
BAND_PAST = N_PAST_CHUNKS * CHUNK
ATTN_WIDTH = 1024
HEAD_DIM = 64
N_HEADS = 16
REL_CLIP = 128
N_REL = 2 * REL_CLIP + 1
RWKV_WIDTH = 1024
DECAY_RANK = 64
AAA_RANK = 64
GATE_RANK = 128
SHIFT_W = 3 * RWKV_WIDTH + DECAY_RANK + AAA_RANK + GATE_RANK
D_FF = 5632
RMS_EPS = 1e-6
GN_EPS = 64e-5

V7X_LANES = 128
V7X_VMEM_BYTES = 64 * 1024 * 1024
V7X_SCOPED_VMEM_CAP_BYTES = 60000 * 1024

PAIR_W = 2 * HEAD_DIM
N_PAIRS = N_HEADS // 2
CHUNK_T = 64
KEY_WIN = (N_PAST_CHUNKS + 2) * CHUNK
NEG_BIG = -1e30
ATTN_QB = 512
ATTN_CHUNKS_PER_ITER = 2
ATTN_SOFTMAX_ROWS = 128
RWKV_CHUNKS_PER_STEP = 2

assert PAIR_W == V7X_LANES


def _vmem_limit(pipelined_bytes, scratch_bytes, temp_bytes):
    need = 2 * pipelined_bytes + scratch_bytes + temp_bytes
    return int(min(max(need, 16 * 1024 * 1024), V7X_SCOPED_VMEM_CAP_BYTES))


def _nbytes(shape, dtype):
    return math.prod(shape) * jnp.dtype(dtype).itemsize


def _params(n_grid, vmem):
    return pltpu.CompilerParams(dimension_semantics=("arbitrary",) * n_grid, vmem_limit_bytes=vmem)


def _rms(x, g):
    return x * lax.rsqrt(jnp.mean(x * x, axis=-1, keepdims=True) + RMS_EPS) * g


def _norm_kernel(x_ref, g_ref, o_ref):
    o_ref[...] = _rms(x_ref[...], g_ref[...]).astype(BF16)


def _norm(x, g, *, tm, name):
    t, d = x.shape
    assert t % tm == 0
    blocks = _nbytes((tm, d), F32) + _nbytes((tm, d), BF16)
    return pl.pallas_call(
        _norm_kernel,
        out_shape=jax.ShapeDtypeStruct((t, d), BF16),
        grid=(t // tm,),
        in_specs=[pl.BlockSpec((tm, d), lambda i: (i, 0)), pl.BlockSpec((1, d), lambda i: (0, 0))],
        out_specs=pl.BlockSpec((tm, d), lambda i: (i, 0)),
        compiler_params=_params(1, _vmem_limit(blocks, 0, 2 * _nbytes((tm, d), F32))),
        name=name,
    )(x, g)


def _inproj_kernel(n_ref, w_ref, o_ref):
    o_ref[...] = jnp.dot(n_ref[...], w_ref[...], preferred_element_type=F32)


def _inproj(nx, w, *, col0, n, tm, tn, name):
    t, d = nx.shape
    assert t % tm == 0 and n % tn == 0 and col0 % V7X_LANES == 0 and tn % V7X_LANES == 0 and col0 + n <= w.shape[1]
    blocks = _nbytes((tm, d), BF16) + _nbytes((d, tn), BF16) + _nbytes((tm, tn), F32)
    return pl.pallas_call(
        _inproj_kernel,
        out_shape=jax.ShapeDtypeStruct((t, n), F32),
        grid=(t // tm, n // tn),
        in_specs=[pl.BlockSpec((tm, d), lambda i, j: (i, 0)),
                  pl.BlockSpec((pl.Element(d), pl.Element(tn)),
                               lambda i, j: (0, (col0 // V7X_LANES + j * (tn // V7X_LANES)) * V7X_LANES))],
        out_specs=pl.BlockSpec((tm, tn), lambda i, j: (i, j)),
        compiler_params=_params(2, _vmem_limit(blocks, 0, _nbytes((tm, tn), F32))),
        name=name,
    )(nx, w)


BIAS_VAR_COL0 = KEY_WIN - 2 * V7X_LANES
BIAS_DIAG_W = 512


def _bias_kernel(tab_ref, o_ref, *, mq, off, col_lo, col_hi):
    nvar = KEY_WIN - BIAS_VAR_COL0
    w = BIAS_DIAG_W
    assert nvar + mq <= w
    lane = lax.broadcasted_iota(jnp.int32, (1, w), 1)
    idx = jnp.clip(off - (BIAS_VAR_COL0 - mq + lane), -REL_CLIP, REL_CLIP) + REL_CLIP
    idx_min = max(0, min(REL_CLIP, off - (BIAS_VAR_COL0 - mq + w - 1)) + REL_CLIP)
    heads = range(N_HEADS)

    def body(j, profs):
        hit = idx == j
        return tuple(jnp.where(hit, tab_ref[h, j], prof) for h, prof in zip(heads, profs))

    init = tuple(jnp.full((1, w), tab_ref[h, N_REL - 1], F32) for h in heads)
    profs = lax.fori_loop(idx_min, N_REL - 1, body, init)

    c = lax.broadcasted_iota(jnp.int32, (mq, nvar), 1) + BIAS_VAR_COL0
    valid_var = (c >= col_lo) & (c < col_hi)
    c_far = lax.broadcasted_iota(jnp.int32, (mq, BIAS_VAR_COL0), 1)
    valid_far = (c_far >= col_lo) & (c_far < col_hi)
    for h in heads:
        tile = pltpu.roll(jnp.broadcast_to(profs[h], (mq, w)), w - mq, axis=1, stride=1, stride_axis=0)
        rows = slice((h % 2) * mq, (h % 2 + 1) * mq)
        far = jnp.full((mq, BIAS_VAR_COL0), tab_ref[h, N_REL - 1], F32)
        o_ref[h // 2, rows, 0:BIAS_VAR_COL0] = jnp.where(valid_far, far, NEG_BIG)
        o_ref[h // 2, rows, BIAS_VAR_COL0:KEY_WIN] = jnp.where(valid_var, tile[:, 0:nvar], NEG_BIG)


def _bias_tiles(table, *, mq, off, col_lo, col_hi, name):
    assert off - (BIAS_VAR_COL0 - 1) >= REL_CLIP
    return pl.pallas_call(
        functools.partial(_bias_kernel, mq=mq, off=off, col_lo=col_lo, col_hi=col_hi),
        out_shape=jax.ShapeDtypeStruct((N_PAIRS, 2 * mq, KEY_WIN), F32),
        in_specs=[pl.BlockSpec(memory_space=pltpu.SMEM)],
        out_specs=pl.BlockSpec(memory_space=pltpu.VMEM),
        name=name,
    )(table)


def _lane_lo(rows):
    return lax.broadcasted_iota(jnp.int32, (rows, PAIR_W), 1) < HEAD_DIM


def _attend(q_pair, k_pair, v_pair, bias_ref, o_put, s_scr, p_scr, *, mq, n_win, min_col):
    lo = _lane_lo(mq)
    units = [(w, p) for w in range(n_win) for p in range(N_PAIRS)]
    for u, (w, p) in enumerate(units):
        qs = q_pair(w, p) * (HEAD_DIM ** -0.5)
        q2 = jnp.concatenate([jnp.where(lo, qs, 0.0), jnp.where(lo, 0.0, qs)], axis=0).astype(BF16)
        s = lax.dot_general(q2, k_pair(w, p), (((1,), (1,)), ((), ())), preferred_element_type=F32) + bias_ref[p]
        first = min_col(w)
        if first is not None:
            col = lax.broadcasted_iota(jnp.int32, (1, KEY_WIN), 1)
            s = jnp.where(col >= first, s, NEG_BIG)
        s_scr[u] = s
    inv_l = []
    rb = min(2 * mq, ATTN_SOFTMAX_ROWS)
    for u in range(len(units)):
        parts = []
        for r0 in range(0, 2 * mq, rb):
            s = s_scr[u, r0:r0 + rb, :]
            e = jnp.exp(s - jnp.max(s, axis=-1, keepdims=True))
            parts.append(1.0 / jnp.sum(e, axis=-1, keepdims=True))
            p_scr[u, r0:r0 + rb, :] = e.astype(BF16)
        inv_l.append(jnp.concatenate(parts, axis=0))
    for u, (w, p) in enumerate(units):
        pv = jnp.dot(p_scr[u], v_pair(w, p), preferred_element_type=F32) * inv_l[u]
        o_put(w, p, jnp.where(lo, pv[:mq], pv[mq:]))


def _attn_prompt_kernel(q_ref, kc_ref, vc_ref, bias_ref, o_ref, kbuf, vbuf, s_scr, p_scr):
    i = pl.program_id(0)
    prev0, cur0, end = CHUNK, CHUNK + ATTN_QB, CHUNK + 2 * ATTN_QB

    @pl.when(i == 0)
    def _():
        zeros = jnp.zeros((cur0, ATTN_WIDTH), BF16)
        kbuf[0:cur0, :] = zeros
        vbuf[0:cur0, :] = zeros

    kbuf[cur0:end, :] = kc_ref[...].astype(BF16)
    vbuf[cur0:end, :] = vc_ref[...].astype(BF16)
    chunks_per_step = ATTN_QB // CHUNK
    lead_chunks = N_PAST_CHUNKS + 1
    n_win = ATTN_CHUNKS_PER_ITER

    def iter_body(j, carry, *, masked):
        sl = lambda p: slice(p * PAIR_W, (p + 1) * PAIR_W)
        r0 = lambda w: pl.multiple_of((j * n_win + w) * CHUNK, CHUNK)
        min_col = lambda w: (lead_chunks - (i * chunks_per_step + j * n_win + w)) * CHUNK if masked else None

        def o_put(w, p, o):
            o_ref[pl.ds(r0(w), CHUNK), sl(p)] = o.astype(BF16)

        _attend(lambda w, p: q_ref[pl.ds(r0(w), CHUNK), sl(p)], lambda w, p: kbuf[pl.ds(r0(w), KEY_WIN), sl(p)],
                lambda w, p: vbuf[pl.ds(r0(w), KEY_WIN), sl(p)], bias_ref, o_put, s_scr, p_scr, mq=CHUNK,
                n_win=n_win, min_col=min_col)
        return carry

    masked_steps = -(-lead_chunks // chunks_per_step)

    @pl.when(i < masked_steps)
    def _():
        lax.fori_loop(0, chunks_per_step // n_win, functools.partial(iter_body, masked=True), 0)

    @pl.when(i >= masked_steps)
    def _():
        lax.fori_loop(0, chunks_per_step // n_win, functools.partial(iter_body, masked=False), 0)

    kbuf[prev0:cur0, :] = kbuf[cur0:end, :]
    vbuf[prev0:cur0, :] = vbuf[cur0:end, :]


def _attn_prompt(z_attn, bias):
    t = z_attn.shape[0]
    assert t % ATTN_QB == 0 and (ATTN_QB // CHUNK) % ATTN_CHUNKS_PER_ITER == 0 and ATTN_QB >= BAND_PAST
    blk = (ATTN_QB, ATTN_WIDTH)
    buf_rows = CHUNK + 2 * ATTN_QB
    s_shape = (ATTN_CHUNKS_PER_ITER * N_PAIRS, 2 * CHUNK, KEY_WIN)
    blocks = 3 * _nbytes(blk, F32) + _nbytes(bias.shape, F32) + _nbytes(blk, BF16)
    scratch = 2 * _nbytes((buf_rows, ATTN_WIDTH), BF16) + _nbytes(s_shape, F32) + _nbytes(s_shape, BF16)
    return pl.pallas_call(
        _attn_prompt_kernel,
        out_shape=jax.ShapeDtypeStruct((t, ATTN_WIDTH), BF16),
        grid=(t // ATTN_QB,),
        in_specs=[pl.BlockSpec(blk, lambda i: (i, 0)),
                  pl.BlockSpec(blk, lambda i: (i, 1)),
                  pl.BlockSpec(blk, lambda i: (i, 2)),
                  pl.BlockSpec(bias.shape, lambda i: (0, 0, 0))],
        out_specs=pl.BlockSpec(blk, lambda i: (i, 0)),
        scratch_shapes=[pltpu.VMEM((buf_rows, ATTN_WIDTH), BF16), pltpu.VMEM((buf_rows, ATTN_WIDTH), BF16),
                        pltpu.VMEM(s_shape, F32), pltpu.VMEM(s_shape, BF16)],
        compiler_params=_params(1, _vmem_limit(blocks, scratch, 4 * _nbytes((2 * CHUNK, KEY_WIN), F32))),
        name="attn_prompt",
    )(z_attn, z_attn, z_attn, bias)


def _attn_sample_kernel(q_ref, kn_ref, vn_ref, kc_ref, vc_ref, bias_ref, o_ref, kbuf, vbuf, s_scr, p_scr, *,
                        s, past):
    tail = jnp.zeros((KEY_WIN - past - s, ATTN_WIDTH), BF16)
    kbuf[0:past, :] = kc_ref[...].astype(BF16)
    vbuf[0:past, :] = vc_ref[...].astype(BF16)
    kbuf[past:past + s, :] = kn_ref[...].astype(BF16)
    vbuf[past:past + s, :] = vn_ref[...].astype(BF16)
    kbuf[past + s:KEY_WIN, :] = tail
    vbuf[past + s:KEY_WIN, :] = tail
    sl = lambda p: slice(p * PAIR_W, (p + 1) * PAIR_W)

    def o_put(w, p, o):
        o_ref[:, sl(p)] = o.astype(BF16)

    _attend(lambda w, p: q_ref[:, sl(p)], lambda w, p: kbuf[:, sl(p)], lambda w, p: vbuf[:, sl(p)], bias_ref,
            o_put, s_scr, p_scr, mq=s, n_win=1, min_col=lambda w: None)


def _attn_sample(z_attn, k_cache, v_cache, bias, *, nb, s):
    past = k_cache.shape[1]
    assert past + s <= KEY_WIN and z_attn.shape[0] == nb * s
    blk = (s, ATTN_WIDTH)
    cblk = (None, past, ATTN_WIDTH)
    s_shape = (N_PAIRS, 2 * s, KEY_WIN)
    blocks = (3 * _nbytes(blk, F32) + 2 * _nbytes((past, ATTN_WIDTH), F32) + _nbytes(bias.shape, F32)
              + _nbytes(blk, BF16))
    scratch = 2 * _nbytes((KEY_WIN, ATTN_WIDTH), BF16) + _nbytes(s_shape, F32) + _nbytes(s_shape, BF16)
    return pl.pallas_call(
        functools.partial(_attn_sample_kernel, s=s, past=past),
        out_shape=jax.ShapeDtypeStruct((nb * s, ATTN_WIDTH), BF16),
        grid=(nb,),
        in_specs=[pl.BlockSpec(blk, lambda b: (b, 0)),
                  pl.BlockSpec(blk, lambda b: (b, 1)),
                  pl.BlockSpec(blk, lambda b: (b, 2)),
                  pl.BlockSpec(cblk, lambda b: (b, 0, 0)),
                  pl.BlockSpec(cblk, lambda b: (b, 0, 0)),
                  pl.BlockSpec(bias.shape, lambda b: (0, 0, 0))],
        out_specs=pl.BlockSpec(blk, lambda b: (b, 0)),
        scratch_shapes=[pltpu.VMEM((KEY_WIN, ATTN_WIDTH), BF16), pltpu.VMEM((KEY_WIN, ATTN_WIDTH), BF16),
                        pltpu.VMEM(s_shape, F32), pltpu.VMEM(s_shape, BF16)],
        compiler_params=_params(1, _vmem_limit(blocks, scratch, 4 * _nbytes((2 * s, KEY_WIN), F32))),
        name="attn_sample",
    )(z_attn, z_attn, z_attn, k_cache, v_cache, bias)


def _bf(x):
    return x.astype(BF16)


def _split2(x):
    hi = _bf(x)
    return hi, _bf(x - hi.astype(F32))


_NN = (((1,), (0,)), ((), ()))
_NT = (((1,), (1,)), ((), ()))


def _mm(x, y, dims=_NN):
    return lax.dot_general(x, y, dims, preferred_element_type=F32)


def _bd(y, lo):
    zero = jnp.zeros_like(y)
    return jnp.concatenate([jnp.where(lo, y, zero), jnp.where(lo, zero, y)], axis=0)


def _seg_sum(x, ones_seg, *, passes):
    r = x.shape[0]
    half = x.shape[1] // (2 * PAIR_W)
    slab = lambda p: x[:, p * PAIR_W:(p + 1) * PAIR_W]
    xx = jnp.concatenate([jnp.concatenate([slab(p) for p in range(half)], axis=0),
                          jnp.concatenate([slab(half + p) for p in range(half)], axis=0)], axis=1)
    hi, lo = _split2(xx)
    s = _mm(hi, ones_seg)
    if passes == 2:
        s = s + _mm(lo, ones_seg)
    return jnp.concatenate([s[p * r:(p + 1) * r, 0:PAIR_W] for p in range(half)]
                           + [s[p * r:(p + 1) * r, PAIR_W:2 * PAIR_W] for p in range(half)], axis=1)


def _sigmoid(x):
    return 1.0 / (1.0 + jnp.exp(-x))


def _sigmoid_tanh(x):
    return 0.5 + 0.5 * jnp.tanh(0.5 * x)


_TOK_F32 = ("at", "rt", "v", "bonus", "g")
_TOK_BF16 = ("at", "bt", "kt", "rt", "v", "bp", "kp")


def _rwkv_kernel(zf_ref, zn_ref, sh_ref, s0_ref, mu_ref, w0_ref, w2cat_ref, a0_ref, gw2_ref, kk_ref, ka_ref,
                 rk_ref, lnw_ref, lnb_ref, o_ref, sfin_ref, s_scr, carry_scr, y_scr, f32_scr, bf_scr, pc_scr, *,
                 rows, cps, nck):
    ct = CHUNK_T
    pw = PAIR_W
    tt = cps * ct
    assert rows == ct or cps == 1
    c = pl.program_id(1)

    seg = 2 * pw
    ones_seg = _bf(lax.broadcasted_iota(jnp.int32, (seg, seg), 0) // HEAD_DIM
                   == lax.broadcasted_iota(jnp.int32, (seg, seg), 1) // HEAD_DIM)
    row = lax.broadcasted_iota(jnp.int32, (ct, 1), 0)
    row_all = lax.broadcasted_iota(jnp.int32, (tt, 1), 0)
    tri = _bf(lax.broadcasted_iota(jnp.int32, (ct, ct), 1) <= lax.broadcasted_iota(jnp.int32, (ct, ct), 0))
    t_lane = lax.broadcasted_iota(jnp.int32, (ct, 2 * ct), 1) % ct
    strict = t_lane < row
    incl = t_lane <= row
    lo_t = _lane_lo(ct)
    lo_p = lax.broadcasted_iota(jnp.int32, (ct, 2 * ct), 1) < ct
    lo_t2 = lax.broadcasted_iota(jnp.int32, (ct, 2 * pw), 1) % pw < HEAD_DIM
    diag = (lax.broadcasted_iota(jnp.int32, (HEAD_DIM, pw), 1) % HEAD_DIM
            == lax.broadcasted_iota(jnp.int32, (HEAD_DIM, pw), 0))
    lo_s = _lane_lo(HEAD_DIM)
    n_dbl = int(math.log2(ct))
    assert 1 << n_dbl == ct and 2 * ct == pw
    pairs = range(N_PAIRS)
    cat0 = lambda xs: jnp.concatenate(xs, axis=0)
    cat1 = lambda xs: jnp.concatenate(xs, axis=1)

    def tn(xv, yv):
        zz = _mm(_bf(xv.T), yv)
        return jnp.where(lo_s, zz[:HEAD_DIM], zz[HEAD_DIM:])

    def chunk_pre(zc):
        r = zc[:, 0:RWKV_WIDTH]
        k = zc[:, RWKV_WIDTH:2 * RWKV_WIDTH]
        v = zc[:, 2 * RWKV_WIDTH:3 * RWKV_WIDTH]
        zwa = zc[:, 3 * RWKV_WIDTH:3 * RWKV_WIDTH + DECAY_RANK + AAA_RANK]
        zg = zc[:, 3 * RWKV_WIDTH + DECAY_RANK + AAA_RANK:SHIFT_W]
        lora_in = cat0([jnp.where(lo_t, jnp.tanh(zwa), 0.0), jnp.where(lo_t, 0.0, zwa)])
        lora = _mm(_bf(lora_in), w2cat_ref[...])
        yield
        lw = -math.exp(-0.5) * _sigmoid_tanh(w0_ref[...] + lora[:ct])
        if rows < ct:
            live = row < rows
            lw = jnp.where(live, lw, 0.0)
        l_hi, l_lo = _split2(lw)
        cl = _mm(tri, l_hi) + _mm(tri, l_lo)
        yield
        a = _sigmoid_tanh(a0_ref[...] + lora[ct:])
        g = _mm(_bf(_sigmoid_tanh(zg)), gw2_ref[...])
        yield
        kkr = k * kk_ref[...]
        kn = kkr * lax.rsqrt(jnp.maximum(_seg_sum(kkr * kkr, ones_seg, passes=1), 1e-24))
        yield
        k2 = k * (1.0 + (a - 1.0) * ka_ref[...])
        bonus = _seg_sum(r * k2 * rk_ref[...], ones_seg, passes=1) * v
        yield
        e_in = jnp.exp(cl)
        rt = r * e_in
        pc = e_in[ct - 1:ct, :]
        yield
        at = -kn * jnp.exp(cl - lw)
        yield
        e_inv = jnp.exp(-cl)
        bt = kn * a * e_inv
        kt = k2 * e_inv
        if rows < ct:
            bt = jnp.where(live, bt, 0.0)
            kt = jnp.where(live, kt, 0.0)
        yield
        return dict(at=at, bt=bt, kt=kt, rt=rt, v=v, bp=bt * pc, kp=kt * pc, pc=pc, bonus=bonus, g=g)

    def pre_block(z_ref, carry, wr):
        z = z_ref[...]
        if rows < ct:
            z = jnp.concatenate([z, jnp.zeros((ct - rows, SHIFT_W), F32)], axis=0)
        zprev = jnp.where(row_all == 0, carry, pltpu.roll(z, 1, axis=0))
        last = (cps - 1) * ct + rows - 1
        carry_scr[0:1, :] = z[last:last + 1, :]
        zs = z + (zprev - z) * mu_ref[...]
        yield
        for gi in range(cps):
            t = yield from chunk_pre(zs[gi * ct:(gi + 1) * ct])
            rs = slice(gi * ct, (gi + 1) * ct)
            for i, name in enumerate(_TOK_F32):
                f32_scr[wr, i, rs, :] = t[name]
            yield
            for i, name in enumerate(_TOK_BF16):
                bf_scr[wr, i, rs, :] = _bf(t[name])
            pc_scr[wr, gi:gi + 1, :] = t["pc"]
            yield

    def chunk_stages(rd, gi):
        rs = slice(gi * ct, (gi + 1) * ct)
        f32 = lambda name, p: f32_scr[rd, _TOK_F32.index(name), rs, p * pw:(p + 1) * pw]
        b16 = lambda name, p: bf_scr[rd, _TOK_BF16.index(name), rs, p * pw:(p + 1) * pw]
        v_bd = [_bd(b16("v", p), lo_t) for p in pairs]
        amat = [_mm(cat0([b16("at", p), b16("rt", p)]), cat0([_bd(b16("bt", p), lo_t), _bd(b16("kt", p), lo_t)]),
                    _NT) for p in pairs]
        yield
        a_ab = [jnp.where(strict, amat[p][:ct, :pw], 0.0) for p in pairs]
        a_ak = [_bf(jnp.where(strict, amat[p][:ct, pw:], 0.0)) for p in pairs]
        a_rb = [_bf(jnp.where(incl, amat[p][ct:, :pw], 0.0)) for p in pairs]
        a_rk = [_bf(jnp.where(incl, amat[p][ct:, pw:], 0.0)) for p in pairs]
        x = [cat1([f32("at", p), _mm(a_ak[p], v_bd[p])]) for p in pairs]
        npow = [_bf(a_ab[p]) for p in pairs]
        yield
        for it in range(n_dbl):
            if it + 1 < n_dbl:
                res = [_mm(npow[p], cat1([_bd(_bf(x[p]), lo_t2), _bd(npow[p], lo_p)])) for p in pairs]
                npow = [_bf(res[p][:, 2 * pw:]) for p in pairs]
            else:
                res = [_mm(npow[p], _bd(_bf(x[p]), lo_t2)) for p in pairs]
            x = [x[p] + res[p][:, :2 * pw] for p in pairs]
            yield
        ry = [_mm(a_rb[p], _bd(_bf(x[p]), lo_t2)) for p in pairs]
        rp = [_bf(f32("rt", p) + ry[p][:, :pw]) for p in pairs]
        y0 = [ry[p][:, pw:] + _mm(a_rk[p], v_bd[p]) for p in pairs]
        yield
        m_mat = [_bf(jnp.where(diag, pc_scr[rd, gi:gi + 1, p * pw:(p + 1) * pw], 0.0)
                     + tn(x[p][:, :pw], b16("bp", p))) for p in pairs]
        yield
        n_mat = [tn(cat0([x[p][:, pw:], f32("v", p)]), cat0([b16("bp", p), b16("kp", p)])) for p in pairs]
        return rp, y0, m_mat, n_mat

    def lockstep(*gens):
        done = [None] * len(gens)
        live = list(range(len(gens)))
        while live:
            for i in list(live):
                try:
                    next(gens[i])
                except StopIteration as stop:
                    done[i] = stop.value
                    live.remove(i)
            yield
        return done

    def main_block(rd):
        terms = yield from lockstep(*[chunk_stages(rd, gi) for gi in range(cps)])
        state = [s_scr[:, p * pw:(p + 1) * pw] for p in pairs]
        for gi, (rp, y0, m_mat, n_mat) in enumerate(terms):
            for p in pairs:
                s_old = _bf(state[p])
                y_scr[gi * ct:(gi + 1) * ct, p * pw:(p + 1) * pw] = _mm(rp[p], _bd(s_old, lo_s), _NT) + y0[p]
                state[p] = _mm(s_old, _bd(m_mat[p], lo_s)) + n_mat[p]
            yield
        for p in pairs:
            s_scr[:, p * pw:(p + 1) * pw] = state[p]
        y = y_scr[...]
        mean = _seg_sum(y, ones_seg, passes=2) * (1.0 / HEAD_DIM)
        dv = y - mean
        yield
        var = _seg_sum(dv * dv, ones_seg, passes=1) * (1.0 / HEAD_DIM)
        yn = dv * lax.rsqrt(var + GN_EPS) * lnw_ref[...] + lnb_ref[...]
        out = (yn + f32_scr[rd, _TOK_F32.index("bonus")]) * f32_scr[rd, _TOK_F32.index("g")]
        o_ref[...] = out[:(cps - 1) * ct + rows].astype(BF16)

    def run(*gens):
        for _ in lockstep(*gens):
            pass

    @pl.when(c == 0)
    def _():
        for p in pairs:
            s_scr[:, p * pw:(p + 1) * pw] = jnp.concatenate([s0_ref[2 * p], s0_ref[2 * p + 1]], axis=1)
        run(pre_block(zf_ref, sh_ref[...], 0))

    if nck == 1:
        run(main_block(0))
    else:
        for parity in (0, 1):
            @pl.when(c % 2 == parity)
            def _(parity=parity):
                run(main_block(parity), pre_block(zn_ref, carry_scr[0:1, :], 1 - parity))

    @pl.when(c == nck - 1)
    def _():
        for h in range(N_HEADS):
            sfin_ref[h] = s_scr[:, h * HEAD_DIM:(h + 1) * HEAD_DIM]


def _rwkv(z_rwkv, shift0, s0, lp, *, nb, seq, name):
    rows = min(seq, CHUNK_T)
    cps = min(seq // rows, RWKV_CHUNKS_PER_STEP)
    blk_rows = cps * rows
    assert seq % blk_rows == 0 and z_rwkv.shape == (nb * seq, SHIFT_W)
    nck = seq // blk_rows
    tt = cps * CHUNK_T
    row1 = lambda n: pl.BlockSpec((1, n), lambda b, c: (0, 0))
    full = lambda r, n: pl.BlockSpec((r, n), lambda b, c: (0, 0))
    st_blk = pl.BlockSpec((None, N_HEADS, HEAD_DIM, HEAD_DIM), lambda b, c: (b, 0, 0, 0))
    lora_rows = DECAY_RANK + AAA_RANK
    blocks = (2 * _nbytes((blk_rows, SHIFT_W), F32) + 2 * _nbytes((HEAD_DIM, RWKV_WIDTH), F32)
              + _nbytes((lora_rows + GATE_RANK, RWKV_WIDTH), BF16) + _nbytes((16, SHIFT_W), F32)
              + _nbytes((blk_rows, RWKV_WIDTH), BF16))
    stage_shapes = [((2, len(_TOK_F32), tt, RWKV_WIDTH), F32), ((2, len(_TOK_BF16), tt, RWKV_WIDTH), BF16),
                    ((2, 8, RWKV_WIDTH), F32)]
    scratch = (_nbytes((HEAD_DIM, RWKV_WIDTH), F32) + _nbytes((8, SHIFT_W), F32)
               + _nbytes((tt, RWKV_WIDTH), F32) + sum(_nbytes(s, d) for s, d in stage_shapes))
    assert cps <= 8
    out, s_fin = pl.pallas_call(
        functools.partial(_rwkv_kernel, rows=rows, cps=cps, nck=nck),
        out_shape=(jax.ShapeDtypeStruct((nb * seq, RWKV_WIDTH), BF16),
                   jax.ShapeDtypeStruct((nb, N_HEADS, HEAD_DIM, HEAD_DIM), F32)),
        grid=(nb, nck),
        in_specs=[pl.BlockSpec((blk_rows, SHIFT_W), lambda b, c: (b * nck, 0)),
                  pl.BlockSpec((blk_rows, SHIFT_W), lambda b, c: (b * nck + jnp.minimum(c + 1, nck - 1), 0)),
                  pl.BlockSpec((None, 1, SHIFT_W), lambda b, c: (b, 0, 0)),
                  st_blk,
                  row1(SHIFT_W), row1(RWKV_WIDTH), full(lora_rows, RWKV_WIDTH), row1(RWKV_WIDTH),
                  full(GATE_RANK, RWKV_WIDTH), row1(RWKV_WIDTH),
                  row1(RWKV_WIDTH), row1(RWKV_WIDTH), row1(RWKV_WIDTH), row1(RWKV_WIDTH)],
        out_specs=(pl.BlockSpec((blk_rows, RWKV_WIDTH), lambda b, c: (b * nck + c, 0)), st_blk),
        scratch_shapes=[pltpu.VMEM((HEAD_DIM, RWKV_WIDTH), F32), pltpu.VMEM((8, SHIFT_W), F32),
                        pltpu.VMEM((tt, RWKV_WIDTH), F32)] + [pltpu.VMEM(s, d) for s, d in stage_shapes],
        compiler_params=_params(2, _vmem_limit(blocks, scratch, 64 * _nbytes((tt, RWKV_WIDTH), F32))),
        name=name,
    )(z_rwkv, z_rwkv, shift0, s0, lp["mu"], lp["w0"], lp["w2cat"], lp["a0"], lp["g_w2"],
      lp["k_k"], lp["k_a"], lp["r_k"], lp["lnx_w"], lp["lnx_b"])
    return out, s_fin


def _oproj_kernel(a_ref, r_ref, wa_ref, wr_ref, x_ref, g_ref, h_ref, n_ref):
    h = (x_ref[...] + jnp.dot(a_ref[...], wa_ref[...], preferred_element_type=F32)
         + jnp.dot(r_ref[...], wr_ref[...], preferred_element_type=F32))
    h_ref[...] = h
    n_ref[...] = _rms(h, g_ref[...]).astype(BF16)


def _oproj(attn, rwkv, w_o, x, g, *, tm, name):
    t = x.shape[0]
    assert t % tm == 0
    half = (ATTN_WIDTH, D_MODEL)
    blocks = (2 * _nbytes((tm, ATTN_WIDTH), BF16) + 2 * _nbytes(half, BF16) + 2 * _nbytes((tm, D_MODEL), F32)
              + _nbytes((tm, D_MODEL), BF16))
    return pl.pallas_call(
        _oproj_kernel,
        out_shape=(jax.ShapeDtypeStruct((t, D_MODEL), F32), jax.ShapeDtypeStruct((t, D_MODEL), BF16)),
        grid=(t // tm,),
        in_specs=[pl.BlockSpec((tm, ATTN_WIDTH), lambda i: (i, 0)),
                  pl.BlockSpec((tm, RWKV_WIDTH), lambda i: (i, 0)),
                  pl.BlockSpec(half, lambda i: (0, 0)),
                  pl.BlockSpec(half, lambda i: (1, 0)),
                  pl.BlockSpec((tm, D_MODEL), lambda i: (i, 0)),
                  pl.BlockSpec((1, D_MODEL), lambda i: (0, 0))],
        out_specs=(pl.BlockSpec((tm, D_MODEL), lambda i: (i, 0)), pl.BlockSpec((tm, D_MODEL), lambda i: (i, 0))),
        compiler_params=_params(1, _vmem_limit(blocks, 0, 2 * _nbytes((tm, D_MODEL), F32))),
        name=name,
    )(attn, rwkv, w_o, w_o, x, g)


def _ffn_kernel(n_ref, h_ref, wg_ref, wu_ref, wd_ref, g_ref, o_ref):
    j = pl.program_id(1)

    @pl.when(j == 0)
    def _():
        o_ref[...] = h_ref[...]

    n = n_ref[...]
    gate = jnp.dot(n, wg_ref[...], preferred_element_type=F32)
    up = jnp.dot(n, wu_ref[...], preferred_element_type=F32)
    act = (gate * _sigmoid(gate) * up).astype(BF16)
    o_ref[...] += jnp.dot(act, wd_ref[...], preferred_element_type=F32)

    @pl.when(j == pl.num_programs(1) - 1)
    def _():
        o_ref[...] = _rms(o_ref[...], g_ref[...])


def _ffn(n2, h, wg, wu, wd, g, *, tm, tf, name):
    t = h.shape[0]
    assert t % tm == 0 and D_FF % tf == 0
    blocks = (_nbytes((tm, D_MODEL), BF16) + 2 * _nbytes((tm, D_MODEL), F32) + 3 * _nbytes((D_MODEL, tf), BF16))
    temps = 3 * _nbytes((tm, tf), F32) + _nbytes((tm, tf), BF16)
    return pl.pallas_call(
        _ffn_kernel,
        out_shape=jax.ShapeDtypeStruct((t, D_MODEL), F32),
        grid=(t // tm, D_FF // tf),
        in_specs=[pl.BlockSpec((tm, D_MODEL), lambda i, j: (i, 0)),
                  pl.BlockSpec((tm, D_MODEL), lambda i, j: (i, 0)),
                  pl.BlockSpec((D_MODEL, tf), lambda i, j: (0, j)),
                  pl.BlockSpec((D_MODEL, tf), lambda i, j: (0, j)),
                  pl.BlockSpec((tf, D_MODEL), lambda i, j: (j, 0)),
                  pl.BlockSpec((1, D_MODEL), lambda i, j: (0, 0))],
        out_specs=pl.BlockSpec((tm, D_MODEL), lambda i, j: (i, 0)),
        compiler_params=_params(2, _vmem_limit(blocks, 0, temps)),
        name=name,
    )(n2, h, wg, wu, wd, g)


def _trunk(x, lp, wts, attn_fn, shift0, s0, *, nb, seq, tag):
    t = nb * seq
    tm = min(t, 1024)
    nx = _norm(x, lp["ln1"], tm=tm, name=f"norm_{tag}")
    z_attn = _inproj(nx, wts["w_in"], col0=0, n=3 * ATTN_WIDTH, tm=tm, tn=3 * ATTN_WIDTH // 2,
                     name=f"inproj_attn_{tag}")
    z_rwkv = _inproj(nx, wts["w_in"], col0=3 * ATTN_WIDTH, n=SHIFT_W, tm=tm, tn=SHIFT_W // 2,
                     name=f"inproj_rwkv_{tag}")
    attn = attn_fn(z_attn)
    rwkv, s_fin = _rwkv(z_rwkv, shift0, s0, lp, nb=nb, seq=seq, name=f"rwkv_{tag}")
    tm2 = min(t, 512)
    h, n2 = _oproj(attn, rwkv, wts["w_o"], x, lp["ln2"], tm=tm2, name=f"oproj_{tag}")
    y = _ffn(n2, h, wts["w_gate"], wts["w_up"], wts["w_down"], lp["final_norm"], tm=min(t, 1024), tf=512,
             name=f"ffn_{tag}")
    return y, z_attn, z_rwkv, s_fin


def kernel(x_prompt, x_sample, cache_attn_k, cache_attn_v, state_rwkv_wkv, state_rwkv_shift, ln1, w_in, rel_table,
           mu, w0, w_w2, a0, a_w2, g_w2, k_k, k_a, r_k, lnx_w, lnx_b, w_o, ln2, w_gate, w_up, w_down, final_norm):
    depth = w_in.shape[0]
    assert depth == 1, "final RMSNorm is fused into the (single) layer's FFN call"
    b, t, _ = x_prompt.shape
    bd, s, _ = x_sample.shape
    past = cache_attn_k.shape[2]
    n_keep = min(BAND_PAST, t)
    l = 0
    row = lambda p: p[l].reshape(1, -1)
    lp = dict(ln1=row(ln1), mu=row(mu), w0=row(w0), a0=row(a0),
              w2cat=jnp.concatenate([w_w2[l], a_w2[l]], axis=0).astype(BF16), g_w2=g_w2[l].astype(BF16),
              k_k=row(k_k), k_a=row(k_a), r_k=row(r_k), lnx_w=row(lnx_w), lnx_b=row(lnx_b), ln2=row(ln2),
              final_norm=final_norm.reshape(1, -1))
    wts = dict(w_in=w_in[l].astype(BF16),
               w_o=w_o[l].astype(BF16), w_gate=w_gate[l].astype(BF16), w_up=w_up[l].astype(BF16),
               w_down=w_down[l].astype(BF16))
    table = rel_table[l]

    bias_p = _bias_tiles(table, mq=CHUNK, off=BAND_PAST + CHUNK, col_lo=CHUNK, col_hi=KEY_WIN, name="bias_prompt")
    bias_s = _bias_tiles(table, mq=s, off=past, col_lo=0, col_hi=past + s, name="bias_sample")

    assert b == 1
    yp, zap, zrp, wkv_p = _trunk(
        x_prompt.reshape(b * t, D_MODEL), lp, wts, lambda z: _attn_prompt(z, bias_p),
        jnp.zeros((b, 1, SHIFT_W), F32), jnp.zeros((b, N_HEADS, HEAD_DIM, HEAD_DIM), F32), nb=b, seq=t, tag="p")
    kc = cache_attn_k[l].reshape(bd, past, ATTN_WIDTH)
    vc = cache_attn_v[l].reshape(bd, past, ATTN_WIDTH)
    ys, zas, zrs, wkv_s = _trunk(
        x_sample.reshape(bd * s, D_MODEL), lp, wts, lambda z: _attn_sample(z, kc, vc, bias_s, nb=bd, s=s),
        state_rwkv_shift[l], state_rwkv_wkv[l], nb=bd, seq=s, tag="s")

    heads = lambda z, n, tt: z.reshape(n, tt, N_HEADS, HEAD_DIM)
    zap = zap.reshape(b, t, 3 * ATTN_WIDTH)
    zas = zas.reshape(bd, s, 3 * ATTN_WIDTH)
    k_p = heads(zap[:, t - n_keep:, ATTN_WIDTH:2 * ATTN_WIDTH], b, n_keep)
    v_p = heads(zap[:, t - n_keep:, 2 * ATTN_WIDTH:], b, n_keep)
    k_s = heads(zas[:, :, ATTN_WIDTH:2 * ATTN_WIDTH], bd, s)
    v_s = heads(zas[:, :, 2 * ATTN_WIDTH:], bd, s)
    sh_p = zrp.reshape(b, t, SHIFT_W)[:, t - 1:]
    sh_s = zrs.reshape(bd, s, SHIFT_W)[:, s - 1:]
    return (yp.reshape(b, t, D_MODEL), ys.reshape(bd, s, D_MODEL),
            k_p[None], v_p[None], wkv_p[None], sh_p[None],
            k_s[None], v_s[None], wkv_s[None], sh_s[None])
```

```python
import functools
import math

import jax
import jax.numpy as jnp
from jax import lax
from jax.experimental import pallas as pl
from jax.experimental.pallas import tpu as pltpu

F32 = jnp.float32
BF16 = jnp.bfloat16

D_MODEL = 2048
CHUNK = 64
N_PAST_CHUNKS = 8
BAND_PAST = N_PAST_CHUNKS * CHUNK
ATTN_WIDTH = 1024
HEAD_DIM = 64
N_HEADS = 16
REL_CLIP = 128
N_REL = 2 * REL_CLIP + 1
RWKV_WIDTH = 1024
DECAY_RANK = 64
AAA_RANK = 64
GATE_RANK = 128
SHIFT_W = 3 * RWKV_WIDTH + DECAY_RANK + AAA_RANK + GATE_RANK
D_FF = 5632
RMS_EPS = 1e-6
GN_EPS = 64e-5

V7X_LANES = 128
V7X_VMEM_BYTES = 64 * 1024 * 1024
V7X_SCOPED_VMEM_CAP_BYTES = 60000 * 1024

PAIR_W = 2 * HEAD_DIM
N_PAIRS = N_HEADS // 2
CHUNK_T = 64
KEY_WIN = (N_PAST_CHUNKS + 2) * CHUNK
NEG_BIG = -1e30
ATTN_QB = 512
ATTN_CHUNKS_PER_ITER = 2
ATTN_SOFTMAX_ROWS = 128
RWKV_CHUNKS_PER_STEP = 2

assert PAIR_W == V7X_LANES


def _vmem_limit(pipelined_bytes, scratch_bytes, temp_bytes):
    need = 2 * pipelined_bytes + scratch_bytes + temp_bytes
    return int(min(max(need, 16 * 1024 * 1024), V7X_SCOPED_VMEM_CAP_BYTES))


def _nbytes(shape, dtype):
    return math.prod(shape) * jnp.dtype(dtype).itemsize


def _params(n_grid, vmem):
    return pltpu.CompilerParams(dimension_semantics=("arbitrary",) * n_grid, vmem_limit_bytes=vmem)


def _rms(x, g):
    return x * lax.rsqrt(jnp.mean(x * x, axis=-1, keepdims=True) + RMS_EPS) * g


def _inproj_kernel(x_ref, g_ref, w_ref, o_ref, n_ref):
    @pl.when(pl.program_id(1) == 0)
    def _():
        n_ref[...] = _rms(x_ref[...], g_ref[...]).astype(BF16)

    o_ref[...] = jnp.dot(n_ref[...], w_ref[...], preferred_element_type=F32)


def _inproj(x, g, w, *, col0, n, tm, tn, name):
    t, d = x.shape
    assert t % tm == 0 and n % tn == 0 and col0 % V7X_LANES == 0 and col0 + n <= w.shape[1]
    blocks = _nbytes((tm, d), F32) + _nbytes((d, tn), BF16) + _nbytes((tm, tn), F32)
    return pl.pallas_call(
        _inproj_kernel,
        out_shape=jax.ShapeDtypeStruct((t, n), F32),
        grid=(t // tm, n // tn),
        in_specs=[pl.BlockSpec((tm, d), lambda i, j: (i, 0)),
                  pl.BlockSpec((1, d), lambda i, j: (0, 0)),
                  pl.BlockSpec((pl.Element(d), pl.Element(tn)),
                               lambda i, j: (0, (col0 // V7X_LANES + j * (tn // V7X_LANES)) * V7X_LANES))],
        out_specs=pl.BlockSpec((tm, tn), lambda i, j: (i, j)),
        scratch_shapes=[pltpu.VMEM((tm, d), BF16)],
        compiler_params=_params(2, _vmem_limit(blocks, _nbytes((tm, d), BF16), 2 * _nbytes((tm, d), F32))),
        name=name,
    )(x, g, w)


BIAS_VAR_COL0 = KEY_WIN - 2 * V7X_LANES
BIAS_DIAG_W = 512


def _bias_kernel(tab_ref, o_ref, *, mq, off, col_lo, col_hi):
    nvar = KEY_WIN - BIAS_VAR_COL0
    w = BIAS_DIAG_W
    assert nvar + mq <= w
    lane = lax.broadcasted_iota(jnp.int32, (1, w), 1)
    idx = jnp.clip(off - (BIAS_VAR_COL0 - mq + lane), -REL_CLIP, REL_CLIP) + REL_CLIP
    idx_min = max(0, min(REL_CLIP, off - (BIAS_VAR_COL0 - mq + w - 1)) + REL_CLIP)
    heads = range(N_HEADS)

    def body(j, profs):
        hit = idx == j
        return tuple(jnp.where(hit, tab_ref[h, j], prof) for h, prof in zip(heads, profs))

    init = tuple(jnp.full((1, w), tab_ref[h, N_REL - 1], F32) for h in heads)
    profs = lax.fori_loop(idx_min, N_REL - 1, body, init)

    c = lax.broadcasted_iota(jnp.int32, (mq, nvar), 1) + BIAS_VAR_COL0
    valid_var = (c >= col_lo) & (c < col_hi)
    c_far = lax.broadcasted_iota(jnp.int32, (mq, BIAS_VAR_COL0), 1)
    valid_far = (c_far >= col_lo) & (c_far < col_hi)
    for h in heads:
        tile = pltpu.roll(jnp.broadcast_to(profs[h], (mq, w)), w - mq, axis=1, stride=1, stride_axis=0)
        rows = slice((h % 2) * mq, (h % 2 + 1) * mq)
        far = jnp.full((mq, BIAS_VAR_COL0), tab_ref[h, N_REL - 1], F32)
        o_ref[h // 2, rows, 0:BIAS_VAR_COL0] = jnp.where(valid_far, far, NEG_BIG)
        o_ref[h // 2, rows, BIAS_VAR_COL0:KEY_WIN] = jnp.where(valid_var, tile[:, 0:nvar], NEG_BIG)


def _bias_tiles(table, *, mq, off, col_lo, col_hi, name):
    assert off - (BIAS_VAR_COL0 - 1) >= REL_CLIP
    return pl.pallas_call(
        functools.partial(_bias_kernel, mq=mq, off=off, col_lo=col_lo, col_hi=col_hi),
        out_shape=jax.ShapeDtypeStruct((N_PAIRS, 2 * mq, KEY_WIN), F32),
        in_specs=[pl.BlockSpec(memory_space=pltpu.SMEM)],
        out_specs=pl.BlockSpec(memory_space=pltpu.VMEM),
        name=name,
    )(table)


def _lane_lo(rows):
    return lax.broadcasted_iota(jnp.int32, (rows, PAIR_W), 1) < HEAD_DIM


def _attend(q_pair, k_pair, v_pair, bias_ref, o_put, s_scr, p_scr, *, mq, n_win, min_col):
    lo = _lane_lo(mq)
    units = [(w, p) for w in range(n_win) for p in range(N_PAIRS)]
    for u, (w, p) in enumerate(units):
        qs = q_pair(w, p) * (HEAD_DIM ** -0.5)
        q2 = jnp.concatenate([jnp.where(lo, qs, 0.0), jnp.where(lo, 0.0, qs)], axis=0).astype(BF16)
        s = lax.dot_general(q2, k_pair(w, p), (((1,), (1,)), ((), ())), preferred_element_type=F32) + bias_ref[p]
        first = min_col(w)
        if first is not None:
            col = lax.broadcasted_iota(jnp.int32, (1, KEY_WIN), 1)
            s = jnp.where(col >= first, s, NEG_BIG)
        s_scr[u] = s
    inv_l = []
    rb = min(2 * mq, ATTN_SOFTMAX_ROWS)
    for u in range(len(units)):
        parts = []
        for r0 in range(0, 2 * mq, rb):
            s = s_scr[u, r0:r0 + rb, :]
            e = jnp.exp(s - jnp.max(s, axis=-1, keepdims=True))
            parts.append(1.0 / jnp.sum(e, axis=-1, keepdims=True))
            p_scr[u, r0:r0 + rb, :] = e.astype(BF16)
        inv_l.append(jnp.concatenate(parts, axis=0))
    for u, (w, p) in enumerate(units):
        pv = jnp.dot(p_scr[u], v_pair(w, p), preferred_element_type=F32) * inv_l[u]
        o_put(w, p, jnp.where(lo, pv[:mq], pv[mq:]))


def _attn_prompt_kernel(q_ref, kc_ref, vc_ref, bias_ref, o_ref, kbuf, vbuf, s_scr, p_scr):
    i = pl.program_id(0)
    prev0, cur0, end = CHUNK, CHUNK + ATTN_QB, CHUNK + 2 * ATTN_QB

    @pl.when(i == 0)
    def _():
        zeros = jnp.zeros((cur0, ATTN_WIDTH), BF16)
        kbuf[0:cur0, :] = zeros
        vbuf[0:cur0, :] = zeros

    kbuf[cur0:end, :] = kc_ref[...].astype(BF16)
    vbuf[cur0:end, :] = vc_ref[...].astype(BF16)
    chunks_per_step = ATTN_QB // CHUNK
    lead_chunks = N_PAST_CHUNKS + 1
    n_win = ATTN_CHUNKS_PER_ITER

    def iter_body(j, carry, *, masked):
        sl = lambda p: slice(p * PAIR_W, (p + 1) * PAIR_W)
        r0 = lambda w: pl.multiple_of((j * n_win + w) * CHUNK, CHUNK)
        min_col = lambda w: (lead_chunks - (i * chunks_per_step + j * n_win + w)) * CHUNK if masked else None

        def o_put(w, p, o):
            o_ref[pl.ds(r0(w), CHUNK), sl(p)] = o.astype(BF16)

        _attend(lambda w, p: q_ref[pl.ds(r0(w), CHUNK), sl(p)], lambda w, p: kbuf[pl.ds(r0(w), KEY_WIN), sl(p)],
                lambda w, p: vbuf[pl.ds(r0(w), KEY_WIN), sl(p)], bias_ref, o_put, s_scr, p_scr, mq=CHUNK,
                n_win=n_win, min_col=min_col)
        return carry

    masked_steps = -(-lead_chunks // chunks_per_step)

    @pl.when(i < masked_steps)
    def _():
        lax.fori_loop(0, chunks_per_step // n_win, functools.partial(iter_body, masked=True), 0)

    @pl.when(i >= masked_steps)
    def _():
        lax.fori_loop(0, chunks_per_step // n_win, functools.partial(iter_body, masked=False), 0)

    kbuf[prev0:cur0, :] = kbuf[cur0:end, :]
    vbuf[prev0:cur0, :] = vbuf[cur0:end, :]


def _attn_prompt(z_attn, bias):
    t = z_attn.shape[0]
    assert t % ATTN_QB == 0 and (ATTN_QB // CHUNK) % ATTN_CHUNKS_PER_ITER == 0 and ATTN_QB >= BAND_PAST
    blk = (ATTN_QB, ATTN_WIDTH)
    buf_rows = CHUNK + 2 * ATTN_QB
    s_shape = (ATTN_CHUNKS_PER_ITER * N_PAIRS, 2 * CHUNK, KEY_WIN)
    blocks = 3 * _nbytes(blk, F32) + _nbytes(bias.shape, F32) + _nbytes(blk, BF16)
    scratch = 2 * _nbytes((buf_rows, ATTN_WIDTH), BF16) + _nbytes(s_shape, F32) + _nbytes(s_shape, BF16)
    return pl.pallas_call(
        _attn_prompt_kernel,
        out_shape=jax.ShapeDtypeStruct((t, ATTN_WIDTH), BF16),
        grid=(t // ATTN_QB,),
        in_specs=[pl.BlockSpec(blk, lambda i: (i, 0)),
                  pl.BlockSpec(blk, lambda i: (i, 1)),
                  pl.BlockSpec(blk, lambda i: (i, 2)),
                  pl.BlockSpec(bias.shape, lambda i: (0, 0, 0))],
        out_specs=pl.BlockSpec(blk, lambda i: (i, 0)),
        scratch_shapes=[pltpu.VMEM((buf_rows, ATTN_WIDTH), BF16), pltpu.VMEM((buf_rows, ATTN_WIDTH), BF16),
                        pltpu.VMEM(s_shape, F32), pltpu.VMEM(s_shape, BF16)],
        compiler_params=_params(1, _vmem_limit(blocks, scratch, 4 * _nbytes((2 * CHUNK, KEY_WIN), F32))),
        name="attn_prompt",
    )(z_attn, z_attn, z_attn, bias)


def _attn_sample_kernel(q_ref, kn_ref, vn_ref, kc_ref, vc_ref, bias_ref, o_ref, kbuf, vbuf, s_scr, p_scr, *,
                        s, past):
    tail = jnp.zeros((KEY_WIN - past - s, ATTN_WIDTH), BF16)
    kbuf[0:past, :] = kc_ref[...].astype(BF16)
    vbuf[0:past, :] = vc_ref[...].astype(BF16)
    kbuf[past:past + s, :] = kn_ref[...].astype(BF16)
    vbuf[past:past + s, :] = vn_ref[...].astype(BF16)
    kbuf[past + s:KEY_WIN, :] = tail
    vbuf[past + s:KEY_WIN, :] = tail
    sl = lambda p: slice(p * PAIR_W, (p + 1) * PAIR_W)

    def o_put(w, p, o):
        o_ref[:, sl(p)] = o.astype(BF16)

    _attend(lambda w, p: q_ref[:, sl(p)], lambda w, p: kbuf[:, sl(p)], lambda w, p: vbuf[:, sl(p)], bias_ref,
            o_put, s_scr, p_scr, mq=s, n_win=1, min_col=lambda w: None)


def _attn_sample(z_attn, k_cache, v_cache, bias, *, nb, s):
    past = k_cache.shape[1]
    assert past + s <= KEY_WIN and z_attn.shape[0] == nb * s
    blk = (s, ATTN_WIDTH)
    cblk = (None, past, ATTN_WIDTH)
    s_shape = (N_PAIRS, 2 * s, KEY_WIN)
    blocks = (3 * _nbytes(blk, F32) + 2 * _nbytes((past, ATTN_WIDTH), F32) + _nbytes(bias.shape, F32)
              + _nbytes(blk, BF16))
    scratch = 2 * _nbytes((KEY_WIN, ATTN_WIDTH), BF16) + _nbytes(s_shape, F32) + _nbytes(s_shape, BF16)
    return pl.pallas_call(
        functools.partial(_attn_sample_kernel, s=s, past=past),
        out_shape=jax.ShapeDtypeStruct((nb * s, ATTN_WIDTH), BF16),
        grid=(nb,),
        in_specs=[pl.BlockSpec(blk, lambda b: (b, 0)),
                  pl.BlockSpec(blk, lambda b: (b, 1)),
                  pl.BlockSpec(blk, lambda b: (b, 2)),
                  pl.BlockSpec(cblk, lambda b: (b, 0, 0)),
                  pl.BlockSpec(cblk, lambda b: (b, 0, 0)),
                  pl.BlockSpec(bias.shape, lambda b: (0, 0, 0))],
        out_specs=pl.BlockSpec(blk, lambda b: (b, 0)),
        scratch_shapes=[pltpu.VMEM((KEY_WIN, ATTN_WIDTH), BF16), pltpu.VMEM((KEY_WIN, ATTN_WIDTH), BF16),
                        pltpu.VMEM(s_shape, F32), pltpu.VMEM(s_shape, BF16)],
        compiler_params=_params(1, _vmem_limit(blocks, scratch, 4 * _nbytes((2 * s, KEY_WIN), F32))),
        name="attn_sample",
    )(z_attn, z_attn, z_attn, k_cache, v_cache, bias)


def _bf(x):
    return x.astype(BF16)


def _split2(x):
    hi = _bf(x)
    return hi, _bf(x - hi.astype(F32))


_NN = (((1,), (0,)), ((), ()))
_NT = (((1,), (1,)), ((), ()))


def _mm(x, y, dims=_NN):
    return lax.dot_general(x, y, dims, preferred_element_type=F32)


def _bd(y, lo):
    zero = jnp.zeros_like(y)
    return jnp.concatenate([jnp.where(lo, y, zero), jnp.where(lo, zero, y)], axis=0)


def _seg_sum(x, ones_seg, *, passes):
    r = x.shape[0]
    half = x.shape[1] // (2 * PAIR_W)
    slab = lambda p: x[:, p * PAIR_W:(p + 1) * PAIR_W]
    xx = jnp.concatenate([jnp.concatenate([slab(p) for p in range(half)], axis=0),
                          jnp.concatenate([slab(half + p) for p in range(half)], axis=0)], axis=1)
    hi, lo = _split2(xx)
    s = _mm(hi, ones_seg)
    if passes == 2:
        s = s + _mm(lo, ones_seg)
    return jnp.concatenate([s[p * r:(p + 1) * r, 0:PAIR_W] for p in range(half)]
                           + [s[p * r:(p + 1) * r, PAIR_W:2 * PAIR_W] for p in range(half)], axis=1)


def _sigmoid(x):
    return 1.0 / (1.0 + jnp.exp(-x))


def _sigmoid_tanh(x):
    return 0.5 + 0.5 * jnp.tanh(0.5 * x)


_TOK_F32 = ("at", "rt", "v", "bonus", "g")
_TOK_BF16 = ("at", "bt", "kt", "rt", "v", "bp", "kp")


_RWKV_N_IN = 14
_RWKV_N_OUT = 2
_RWKV_N_SCRATCH = 6


def _rwkv_kernel(*refs, rows, cps, nck, cast_periods):
    n_cast = len(cast_periods)
    (zf_ref, zn_ref, sh_ref, s0_ref, mu_ref, w0_ref, w2cat_ref, a0_ref, gw2_ref, kk_ref, ka_ref, rk_ref, lnw_ref,
     lnb_ref) = refs[:_RWKV_N_IN]
    cast_in = refs[_RWKV_N_IN:_RWKV_N_IN + n_cast]
    o_ref, sfin_ref = refs[_RWKV_N_IN + n_cast:_RWKV_N_IN + n_cast + _RWKV_N_OUT]
    cast_out = refs[_RWKV_N_IN + n_cast + _RWKV_N_OUT:_RWKV_N_IN + 2 * n_cast + _RWKV_N_OUT]
    s_scr, carry_scr, y_scr, f32_scr, bf_scr, pc_scr = refs[-_RWKV_N_SCRATCH:]
    ct = CHUNK_T
    pw = PAIR_W
    tt = cps * ct
    assert rows == ct or cps == 1
    c = pl.program_id(1)

    seg = 2 * pw
    ones_seg = _bf(lax.broadcasted_iota(jnp.int32, (seg, seg), 0) // HEAD_DIM
                   == lax.broadcasted_iota(jnp.int32, (seg, seg), 1) // HEAD_DIM)
    row = lax.broadcasted_iota(jnp.int32, (ct, 1), 0)
    row_all = lax.broadcasted_iota(jnp.int32, (tt, 1), 0)
    tri = _bf(lax.broadcasted_iota(jnp.int32, (ct, ct), 1) <= lax.broadcasted_iota(jnp.int32, (ct, ct), 0))
    t_lane = lax.broadcasted_iota(jnp.int32, (ct, 2 * ct), 1) % ct
    strict = t_lane < row
    incl = t_lane <= row
    lo_t = _lane_lo(ct)
    lo_p = lax.broadcasted_iota(jnp.int32, (ct, 2 * ct), 1) < ct
    lo_t2 = lax.broadcasted_iota(jnp.int32, (ct, 2 * pw), 1) % pw < HEAD_DIM
    diag = (lax.broadcasted_iota(jnp.int32, (HEAD_DIM, pw), 1) % HEAD_DIM
            == lax.broadcasted_iota(jnp.int32, (HEAD_DIM, pw), 0))
    lo_s = _lane_lo(HEAD_DIM)
    n_dbl = int(math.log2(ct))
    assert 1 << n_dbl == ct and 2 * ct == pw
    pairs = range(N_PAIRS)
    cat0 = lambda xs: jnp.concatenate(xs, axis=0)
    cat1 = lambda xs: jnp.concatenate(xs, axis=1)

    def tn(xv, yv):
        zz = _mm(_bf(xv.T), yv)
        return jnp.where(lo_s, zz[:HEAD_DIM], zz[HEAD_DIM:])

    def chunk_pre(zc):
        r = zc[:, 0:RWKV_WIDTH]
        k = zc[:, RWKV_WIDTH:2 * RWKV_WIDTH]
        v = zc[:, 2 * RWKV_WIDTH:3 * RWKV_WIDTH]
        zwa = zc[:, 3 * RWKV_WIDTH:3 * RWKV_WIDTH + DECAY_RANK + AAA_RANK]
        zg = zc[:, 3 * RWKV_WIDTH + DECAY_RANK + AAA_RANK:SHIFT_W]
        lora_in = cat0([jnp.where(lo_t, jnp.tanh(zwa), 0.0), jnp.where(lo_t, 0.0, zwa)])
        lora = _mm(_bf(lora_in), w2cat_ref[...])
        yield
        lw = -math.exp(-0.5) * _sigmoid_tanh(w0_ref[...] + lora[:ct])
        if rows < ct:
            live = row < rows
            lw = jnp.where(live, lw, 0.0)
        l_hi, l_lo = _split2(lw)
        cl = _mm(tri, l_hi) + _mm(tri, l_lo)
        yield
        a = _sigmoid_tanh(a0_ref[...] + lora[ct:])
        g = _mm(_bf(_sigmoid_tanh(zg)), gw2_ref[...])
        yield
        kkr = k * kk_ref[...]
        kn = kkr * lax.rsqrt(jnp.maximum(_seg_sum(kkr * kkr, ones_seg, passes=1), 1e-24))
        yield
        k2 = k * (1.0 + (a - 1.0) * ka_ref[...])
        bonus = _seg_sum(r * k2 * rk_ref[...], ones_seg, passes=1) * v
        yield
        e_in = jnp.exp(cl)
        rt = r * e_in
        pc = e_in[ct - 1:ct, :]
        yield
        at = -kn * jnp.exp(cl - lw)
        yield
        e_inv = jnp.exp(-cl)
        bt = kn * a * e_inv
        kt = k2 * e_inv
        if rows < ct:
            bt = jnp.where(live, bt, 0.0)
            kt = jnp.where(live, kt, 0.0)
        yield
        return dict(at=at, bt=bt, kt=kt, rt=rt, v=v, bp=bt * pc, kp=kt * pc, pc=pc, bonus=bonus, g=g)

    def pre_block(z_ref, carry, wr):
        z = z_ref[...]
        if rows < ct:
            z = jnp.concatenate([z, jnp.zeros((ct - rows, SHIFT_W), F32)], axis=0)
        zprev = jnp.where(row_all == 0, carry, pltpu.roll(z, 1, axis=0))
        last = (cps - 1) * ct + rows - 1
        carry_scr[0:1, :] = z[last:last + 1, :]
        zs = z + (zprev - z) * mu_ref[...]
        yield
        for gi in range(cps):
            t = yield from chunk_pre(zs[gi * ct:(gi + 1) * ct])
            rs = slice(gi * ct, (gi + 1) * ct)
            for i, name in enumerate(_TOK_F32):
                f32_scr[wr, i, rs, :] = t[name]
            yield
            for i, name in enumerate(_TOK_BF16):
                bf_scr[wr, i, rs, :] = _bf(t[name])
            pc_scr[wr, gi:gi + 1, :] = t["pc"]
            yield

    def chunk_stages(rd, gi):
        rs = slice(gi * ct, (gi + 1) * ct)
        f32 = lambda name, p: f32_scr[rd, _TOK_F32.index(name), rs, p * pw:(p + 1) * pw]
        b16 = lambda name, p: bf_scr[rd, _TOK_BF16.index(name), rs, p * pw:(p + 1) * pw]
        v_bd = [_bd(b16("v", p), lo_t) for p in pairs]
        amat = [_mm(cat0([b16("at", p), b16("rt", p)]), cat0([_bd(b16("bt", p), lo_t), _bd(b16("kt", p), lo_t)]),
                    _NT) for p in pairs]
        yield
        a_ab = [jnp.where(strict, amat[p][:ct, :pw], 0.0) for p in pairs]
        a_ak = [_bf(jnp.where(strict, amat[p][:ct, pw:], 0.0)) for p in pairs]
        a_rb = [_bf(jnp.where(incl, amat[p][ct:, :pw], 0.0)) for p in pairs]
        a_rk = [_bf(jnp.where(incl, amat[p][ct:, pw:], 0.0)) for p in pairs]
        x = [cat1([f32("at", p), _mm(a_ak[p], v_bd[p])]) for p in pairs]
        npow = [_bf(a_ab[p]) for p in pairs]
        yield
        for it in range(n_dbl):
            if it + 1 < n_dbl:
                res = [_mm(npow[p], cat1([_bd(_bf(x[p]), lo_t2), _bd(npow[p], lo_p)])) for p in pairs]
                npow = [_bf(res[p][:, 2 * pw:]) for p in pairs]
            else:
                res = [_mm(npow[p], _bd(_bf(x[p]), lo_t2)) for p in pairs]
            x = [x[p] + res[p][:, :2 * pw] for p in pairs]
            yield
        ry = [_mm(a_rb[p], _bd(_bf(x[p]), lo_t2)) for p in pairs]
        rp = [_bf(f32("rt", p) + ry[p][:, :pw]) for p in pairs]
        y0 = [ry[p][:, pw:] + _mm(a_rk[p], v_bd[p]) for p in pairs]
        yield
        m_mat = [_bf(jnp.where(diag, pc_scr[rd, gi:gi + 1, p * pw:(p + 1) * pw], 0.0)
                     + tn(x[p][:, :pw], b16("bp", p))) for p in pairs]
        yield
        n_mat = [tn(cat0([x[p][:, pw:], f32("v", p)]), cat0([b16("bp", p), b16("kp", p)])) for p in pairs]
        return rp, y0, m_mat, n_mat

    def lockstep(*gens):
        done = [None] * len(gens)
        live = list(range(len(gens)))
        while live:
            for i in list(live):
                try:
                    next(gens[i])
                except StopIteration as stop:
                    done[i] = stop.value
                    live.remove(i)
            yield
        return done

    def main_block(rd):
        terms = yield from lockstep(*[chunk_stages(rd, gi) for gi in range(cps)])
        state = [s_scr[:, p * pw:(p + 1) * pw] for p in pairs]
        for gi, (rp, y0, m_mat, n_mat) in enumerate(terms):
            for p in pairs:
                s_old = _bf(state[p])
                y_scr[gi * ct:(gi + 1) * ct, p * pw:(p + 1) * pw] = _mm(rp[p], _bd(s_old, lo_s), _NT) + y0[p]
                state[p] = _mm(s_old, _bd(m_mat[p], lo_s)) + n_mat[p]
            yield
        for p in pairs:
            s_scr[:, p * pw:(p + 1) * pw] = state[p]
        y = y_scr[...]
        mean = _seg_sum(y, ones_seg, passes=2) * (1.0 / HEAD_DIM)
        dv = y - mean
        yield
        var = _seg_sum(dv * dv, ones_seg, passes=1) * (1.0 / HEAD_DIM)
        yn = dv * lax.rsqrt(var + GN_EPS) * lnw_ref[...] + lnb_ref[...]
        out = (yn + f32_scr[rd, _TOK_F32.index("bonus")]) * f32_scr[rd, _TOK_F32.index("g")]
        o_ref[...] = out[:(cps - 1) * ct + rows].astype(BF16)

    def cast_block(parity):
        for src, dst, period in zip(cast_in, cast_out, cast_periods):
            if parity % period == 0:
                dst[...] = _bf(src[...])
            yield

    def run(*gens):
        for _ in lockstep(*gens):
            pass

    @pl.when(c == 0)
    def _():
        for p in pairs:
            s_scr[:, p * pw:(p + 1) * pw] = jnp.concatenate([s0_ref[2 * p], s0_ref[2 * p + 1]], axis=1)
        run(pre_block(zf_ref, sh_ref[...], 0))

    if nck == 1:
        assert n_cast == 0
        run(main_block(0))
    else:
        for parity in (0, 1):
            @pl.when(c % 2 == parity)
            def _(parity=parity):
                run(main_block(parity), pre_block(zn_ref, carry_scr[0:1, :], 1 - parity), cast_block(parity))

    @pl.when(c == nck - 1)
    def _():
        for h in range(N_HEADS):
            sfin_ref[h] = s_scr[:, h * HEAD_DIM:(h + 1) * HEAD_DIM]


def _rwkv(z_rwkv, shift0, s0, lp, *, nb, seq, name, cast_weights=()):
    rows = min(seq, CHUNK_T)
    cps = min(seq // rows, RWKV_CHUNKS_PER_STEP)
    blk_rows = cps * rows
    assert seq % blk_rows == 0 and z_rwkv.shape == (nb * seq, SHIFT_W)
    nck = seq // blk_rows
    tt = cps * CHUNK_T
    row1 = lambda n: pl.BlockSpec((1, n), lambda b, c: (0, 0))
    full = lambda r, n: pl.BlockSpec((r, n), lambda b, c: (0, 0))
    st_blk = pl.BlockSpec((None, N_HEADS, HEAD_DIM, HEAD_DIM), lambda b, c: (b, 0, 0, 0))
    lora_rows = DECAY_RANK + AAA_RANK
    blocks = (2 * _nbytes((blk_rows, SHIFT_W), F32) + 2 * _nbytes((HEAD_DIM, RWKV_WIDTH), F32)
              + _nbytes((lora_rows + GATE_RANK, RWKV_WIDTH), BF16) + _nbytes((16, SHIFT_W), F32)
              + _nbytes((blk_rows, RWKV_WIDTH), BF16))
    stage_shapes = [((2, len(_TOK_F32), tt, RWKV_WIDTH), F32), ((2, len(_TOK_BF16), tt, RWKV_WIDTH), BF16),
                    ((2, 8, RWKV_WIDTH), F32)]
    scratch = (_nbytes((HEAD_DIM, RWKV_WIDTH), F32) + _nbytes((8, SHIFT_W), F32)
               + _nbytes((tt, RWKV_WIDTH), F32) + sum(_nbytes(s, d) for s, d in stage_shapes))
    assert cps <= 8
    bf16_rows = 16
    cast_periods, cast_specs, cast_shapes = [], [], []
    for w in cast_weights:
        assert nb == 1 and nck % 2 == 0
        period = next(p for p in (1, 2) if w.shape[0] % (nck // p) == 0 and (w.shape[0] // (nck // p)) % bf16_rows == 0)
        slab = (w.shape[0] // (nck // period), w.shape[1])
        cast_periods.append(period)
        cast_specs.append(pl.BlockSpec(slab, lambda b, c, period=period: (c // period, 0)))
        cast_shapes.append(jax.ShapeDtypeStruct(w.shape, BF16))
        blocks += _nbytes(slab, F32) + _nbytes(slab, BF16)
    outs = pl.pallas_call(
        functools.partial(_rwkv_kernel, rows=rows, cps=cps, nck=nck, cast_periods=tuple(cast_periods)),
        out_shape=(jax.ShapeDtypeStruct((nb * seq, RWKV_WIDTH), BF16),
                   jax.ShapeDtypeStruct((nb, N_HEADS, HEAD_DIM, HEAD_DIM), F32), *cast_shapes),
        grid=(nb, nck),
        in_specs=[pl.BlockSpec((blk_rows, SHIFT_W), lambda b, c: (b * nck, 0)),
                  pl.BlockSpec((blk_rows, SHIFT_W), lambda b, c: (b * nck + jnp.minimum(c + 1, nck - 1), 0)),
                  pl.BlockSpec((None, 1, SHIFT_W), lambda b, c: (b, 0, 0)),
                  st_blk,
                  row1(SHIFT_W), row1(RWKV_WIDTH), full(lora_rows, RWKV_WIDTH), row1(RWKV_WIDTH),
                  full(GATE_RANK, RWKV_WIDTH), row1(RWKV_WIDTH),
                  row1(RWKV_WIDTH), row1(RWKV_WIDTH), row1(RWKV_WIDTH), row1(RWKV_WIDTH), *cast_specs],
        out_specs=(pl.BlockSpec((blk_rows, RWKV_WIDTH), lambda b, c: (b * nck + c, 0)), st_blk, *cast_specs),
        scratch_shapes=[pltpu.VMEM((HEAD_DIM, RWKV_WIDTH), F32), pltpu.VMEM((8, SHIFT_W), F32),
                        pltpu.VMEM((tt, RWKV_WIDTH), F32)] + [pltpu.VMEM(s, d) for s, d in stage_shapes],
        compiler_params=_params(2, _vmem_limit(blocks, scratch, 64 * _nbytes((tt, RWKV_WIDTH), F32))),
        name=name,
    )(z_rwkv, z_rwkv, shift0, s0, lp["mu"], lp["w0"], lp["w2cat"], lp["a0"], lp["g_w2"],
      lp["k_k"], lp["k_a"], lp["r_k"], lp["lnx_w"], lp["lnx_b"], *cast_weights)
    return outs[0], outs[1], list(outs[2:])


def _oproj_kernel(a_ref, r_ref, wa_ref, wr_ref, x_ref, g_ref, h_ref, n_ref):
    h = (x_ref[...] + jnp.dot(a_ref[...], wa_ref[...], preferred_element_type=F32)
         + jnp.dot(r_ref[...], wr_ref[...], preferred_element_type=F32))
    h_ref[...] = h
    n_ref[...] = _rms(h, g_ref[...]).astype(BF16)


def _oproj(attn, rwkv, w_o, x, g, *, tm, name):
    t = x.shape[0]
    assert t % tm == 0
    half = (ATTN_WIDTH, D_MODEL)
    blocks = (2 * _nbytes((tm, ATTN_WIDTH), BF16) + 2 * _nbytes(half, BF16) + 2 * _nbytes((tm, D_MODEL), F32)
              + _nbytes((tm, D_MODEL), BF16))
    return pl.pallas_call(
        _oproj_kernel,
        out_shape=(jax.ShapeDtypeStruct((t, D_MODEL), F32), jax.ShapeDtypeStruct((t, D_MODEL), BF16)),
        grid=(t // tm,),
        in_specs=[pl.BlockSpec((tm, ATTN_WIDTH), lambda i: (i, 0)),
                  pl.BlockSpec((tm, RWKV_WIDTH), lambda i: (i, 0)),
                  pl.BlockSpec(half, lambda i: (0, 0)),
                  pl.BlockSpec(half, lambda i: (1, 0)),
                  pl.BlockSpec((tm, D_MODEL), lambda i: (i, 0)),
                  pl.BlockSpec((1, D_MODEL), lambda i: (0, 0))],
        out_specs=(pl.BlockSpec((tm, D_MODEL), lambda i: (i, 0)), pl.BlockSpec((tm, D_MODEL), lambda i: (i, 0))),
        compiler_params=_params(1, _vmem_limit(blocks, 0, 2 * _nbytes((tm, D_MODEL), F32))),
        name=name,
    )(attn, rwkv, w_o, w_o, x, g)


def _ffn_kernel(n_ref, h_ref, wg_ref, wu_ref, wd_ref, g_ref, o_ref):
    j = pl.program_id(1)

    @pl.when(j == 0)
    def _():
        o_ref[...] = h_ref[...]

    n = n_ref[...]
    gate = jnp.dot(n, wg_ref[...], preferred_element_type=F32)
    up = jnp.dot(n, wu_ref[...], preferred_element_type=F32)
    act = (gate * _sigmoid(gate) * up).astype(BF16)
    o_ref[...] += jnp.dot(act, wd_ref[...], preferred_element_type=F32)

    @pl.when(j == pl.num_programs(1) - 1)
    def _():
        o_ref[...] = _rms(o_ref[...], g_ref[...])


def _ffn(n2, h, wg, wu, wd, g, *, tm, tf, name):
    t = h.shape[0]
    assert t % tm == 0 and D_FF % tf == 0
    blocks = (_nbytes((tm, D_MODEL), BF16) + 2 * _nbytes((tm, D_MODEL), F32) + 3 * _nbytes((D_MODEL, tf), BF16))
    temps = 3 * _nbytes((tm, tf), F32) + _nbytes((tm, tf), BF16)
    return pl.pallas_call(
        _ffn_kernel,
        out_shape=jax.ShapeDtypeStruct((t, D_MODEL), F32),
        grid=(t // tm, D_FF // tf),
        in_specs=[pl.BlockSpec((tm, D_MODEL), lambda i, j: (i, 0)),
                  pl.BlockSpec((tm, D_MODEL), lambda i, j: (i, 0)),
                  pl.BlockSpec((D_MODEL, tf), lambda i, j: (0, j)),
                  pl.BlockSpec((D_MODEL, tf), lambda i, j: (0, j)),
                  pl.BlockSpec((tf, D_MODEL), lambda i, j: (j, 0)),
                  pl.BlockSpec((1, D_MODEL), lambda i, j: (0, 0))],
        out_specs=pl.BlockSpec((tm, D_MODEL), lambda i, j: (i, 0)),
        compiler_params=_params(2, _vmem_limit(blocks, 0, temps)),
        name=name,
    )(n2, h, wg, wu, wd, g)


_LATE_WEIGHTS = ("w_o", "w_gate", "w_up", "w_down")


def _trunk(x, lp, wts, attn_fn, shift0, s0, *, nb, seq, tag):
    t = nb * seq
    tm = min(t, 1024)
    z_attn = _inproj(x, lp["ln1"], wts["w_in"], col0=0, n=3 * ATTN_WIDTH, tm=tm, tn=1024,
                     name=f"inproj_attn_{tag}")
    z_rwkv = _inproj(x, lp["ln1"], wts["w_in"], col0=3 * ATTN_WIDTH, n=SHIFT_W, tm=tm, tn=SHIFT_W // 2,
                     name=f"inproj_rwkv_{tag}")
    attn = attn_fn(z_attn)
    pending = [k for k in _LATE_WEIGHTS if wts[k].dtype != BF16]
    rwkv, s_fin, converted = _rwkv(z_rwkv, shift0, s0, lp, nb=nb, seq=seq, name=f"rwkv_{tag}",
                                   cast_weights=[wts[k] for k in pending])
    wts.update(zip(pending, converted))
    tm2 = min(t, 512)
    h, n2 = _oproj(attn, rwkv, wts["w_o"], x, lp["ln2"], tm=tm2, name=f"oproj_{tag}")
    y = _ffn(n2, h, wts["w_gate"], wts["w_up"], wts["w_down"], lp["final_norm"], tm=min(t, 1024), tf=512,
             name=f"ffn_{tag}")
    return y, z_attn, z_rwkv, s_fin


def kernel(x_prompt, x_sample, cache_attn_k, cache_attn_v, state_rwkv_wkv, state_rwkv_shift, ln1, w_in, rel_table,
           mu, w0, w_w2, a0, a_w2, g_w2, k_k, k_a, r_k, lnx_w, lnx_b, w_o, ln2, w_gate, w_up, w_down, final_norm):
    depth = w_in.shape[0]
    assert depth == 1, "final RMSNorm is fused into the (single) layer's FFN call"
    b, t, _ = x_prompt.shape
    bd, s, _ = x_sample.shape
    past = cache_attn_k.shape[2]
    n_keep = min(BAND_PAST, t)
    l = 0
    row = lambda p: p[l].reshape(1, -1)
    lp = dict(ln1=row(ln1), mu=row(mu), w0=row(w0), a0=row(a0),
              w2cat=jnp.concatenate([w_w2[l], a_w2[l]], axis=0).astype(BF16), g_w2=g_w2[l].astype(BF16),
              k_k=row(k_k), k_a=row(k_a), r_k=row(r_k), lnx_w=row(lnx_w), lnx_b=row(lnx_b), ln2=row(ln2),
              final_norm=final_norm.reshape(1, -1))
    wts = dict(w_in=w_in[l].astype(BF16), w_o=w_o[l], w_gate=w_gate[l], w_up=w_up[l], w_down=w_down[l])
    table = rel_table[l]

    bias_p = _bias_tiles(table, mq=CHUNK, off=BAND_PAST + CHUNK, col_lo=CHUNK, col_hi=KEY_WIN, name="bias_prompt")
    bias_s = _bias_tiles(table, mq=s, off=past, col_lo=0, col_hi=past + s, name="bias_sample")

    assert b == 1
    yp, zap, zrp, wkv_p = _trunk(
        x_prompt.reshape(b * t, D_MODEL), lp, wts, lambda z: _attn_prompt(z, bias_p),
        jnp.zeros((b, 1, SHIFT_W), F32), jnp.zeros((b, N_HEADS, HEAD_DIM, HEAD_DIM), F32), nb=b, seq=t, tag="p")
    kc = cache_attn_k[l].reshape(bd, past, ATTN_WIDTH)
    vc = cache_attn_v[l].reshape(bd, past, ATTN_WIDTH)
    ys, zas, zrs, wkv_s = _trunk(
        x_sample.reshape(bd * s, D_MODEL), lp, wts, lambda z: _attn_sample(z, kc, vc, bias_s, nb=bd, s=s),
        state_rwkv_shift[l], state_rwkv_wkv[l], nb=bd, seq=s, tag="s")

    heads = lambda z, n, tt: z.reshape(n, tt, N_HEADS, HEAD_DIM)
    zap = zap.reshape(b, t, 3 * ATTN_WIDTH)
    zas = zas.reshape(bd, s, 3 * ATTN_WIDTH)
    k_p = heads(zap[:, t - n_keep:, ATTN_WIDTH:2 * ATTN_WIDTH], b, n_keep)
    v_p = heads(zap[:, t - n_keep:, 2 * ATTN_WIDTH:], b, n_keep)
    k_s = heads(zas[:, :, ATTN_WIDTH:2 * ATTN_WIDTH], bd, s)
    v_s = heads(zas[:, :, 2 * ATTN_WIDTH:], bd, s)
    sh_p = zrp.reshape(b, t, SHIFT_W)[:, t - 1:]
    sh_s = zrs.reshape(bd, s, SHIFT_W)[:, s - 1:]
    return (yp.reshape(b, t, D_MODEL), ys.reshape(bd, s, D_MODEL),
            k_p[None], v_p[None], wkv_p[None], sh_p[None],
            k_s[None], v_s[None], wkv_s[None], sh_s[None])
```

```python
import functools
import math

import jax
import jax.numpy as jnp
from jax import lax
from jax.experimental import pallas as pl
from jax.experimental.pallas import tpu as pltpu

F32 = jnp.float32
BF16 = jnp.bfloat16

D_MODEL = 2048
CHUNK = 64
N_PAST_CHUNKS = 8
BAND_PAST = N_PAST_CHUNKS * CHUNK
ATTN_WIDTH = 1024
HEAD_DIM = 64
N_HEADS = 16
REL_CLIP = 128
N_REL = 2 * REL_CLIP + 1
RWKV_WIDTH = 1024
DECAY_RANK = 64
AAA_RANK = 64
GATE_RANK = 128
SHIFT_W = 3 * RWKV_WIDTH + DECAY_RANK + AAA_RANK + GATE_RANK
D_FF = 5632
RMS_EPS = 1e-6
GN_EPS = 64e-5

V7X_LANES = 128
V7X_VMEM_BYTES = 64 * 1024 * 1024
V7X_SCOPED_VMEM_CAP_BYTES = 60000 * 1024

PAIR_W = 2 * HEAD_DIM
N_PAIRS = N_HEADS // 2
CHUNK_T = 64
KEY_WIN = (N_PAST_CHUNKS + 2) * CHUNK
NEG_BIG = -1e30
ATTN_QB = 512
ATTN_CHUNKS_PER_ITER = 2
ATTN_SOFTMAX_ROWS = 128
RWKV_CHUNKS_PER_STEP = 2

assert PAIR_W == V7X_LANES


def _vmem_limit(pipelined_bytes, scratch_bytes, temp_bytes):
    need = 2 * pipelined_bytes + scratch_bytes + temp_bytes
    del need
    return V7X_SCOPED_VMEM_CAP_BYTES


def _nbytes(shape, dtype):
    return math.prod(shape) * jnp.dtype(dtype).itemsize


def _params(n_grid, vmem, flags=None):
    return pltpu.CompilerParams(dimension_semantics=("arbitrary",) * n_grid, vmem_limit_bytes=vmem, flags=flags)


def _rms(x, g):
    return x * lax.rsqrt(jnp.mean(x * x, axis=-1, keepdims=True) + RMS_EPS) * g


def _inproj_kernel(x_ref, g_ref, w_ref, o_ref, n_ref):
    @pl.when(pl.program_id(1) == 0)
    def _():
        n_ref[...] = _rms(x_ref[...], g_ref[...]).astype(BF16)

    o_ref[...] = jnp.dot(n_ref[...], w_ref[...], preferred_element_type=F32)


def _inproj(x, g, w, *, col0, n, tm, tn, name):
    t, d = x.shape
    assert t % tm == 0 and n % tn == 0 and col0 % V7X_LANES == 0 and col0 + n <= w.shape[1]
    blocks = _nbytes((tm, d), F32) + _nbytes((d, tn), BF16) + _nbytes((tm, tn), F32)
    return pl.pallas_call(
        _inproj_kernel,
        out_shape=jax.ShapeDtypeStruct((t, n), F32),
        grid=(t // tm, n // tn),
        in_specs=[pl.BlockSpec((tm, d), lambda i, j: (i, 0)),
                  pl.BlockSpec((1, d), lambda i, j: (0, 0)),
                  pl.BlockSpec((pl.Element(d), pl.Element(tn)),
                               lambda i, j: (0, (col0 // V7X_LANES + j * (tn // V7X_LANES)) * V7X_LANES))],
        out_specs=pl.BlockSpec((tm, tn), lambda i, j: (i, j)),
        scratch_shapes=[pltpu.VMEM((tm, d), BF16)],
        compiler_params=_params(2, _vmem_limit(blocks, _nbytes((tm, d), BF16), 2 * _nbytes((tm, d), F32))),
        name=name,
    )(x, g, w)


BIAS_VAR_COL0 = KEY_WIN - 2 * V7X_LANES
BIAS_DIAG_W = 512


def _bias_kernel(tab_ref, o_ref, *, mq, off, col_lo, col_hi):
    nvar = KEY_WIN - BIAS_VAR_COL0
    w = BIAS_DIAG_W
    assert nvar + mq <= w
    lane = lax.broadcasted_iota(jnp.int32, (1, w), 1)
    idx = jnp.clip(off - (BIAS_VAR_COL0 - mq + lane), -REL_CLIP, REL_CLIP) + REL_CLIP
    idx_min = max(0, min(REL_CLIP, off - (BIAS_VAR_COL0 - mq + w - 1)) + REL_CLIP)
    heads = range(N_HEADS)

    def body(j, profs):
        hit = idx == j
        return tuple(jnp.where(hit, tab_ref[h, j], prof) for h, prof in zip(heads, profs))

    init = tuple(jnp.full((1, w), tab_ref[h, N_REL - 1], F32) for h in heads)
    profs = lax.fori_loop(idx_min, N_REL - 1, body, init)

    c = lax.broadcasted_iota(jnp.int32, (mq, nvar), 1) + BIAS_VAR_COL0
    valid_var = (c >= col_lo) & (c < col_hi)
    c_far = lax.broadcasted_iota(jnp.int32, (mq, BIAS_VAR_COL0), 1)
    valid_far = (c_far >= col_lo) & (c_far < col_hi)
    for h in heads:
        tile = pltpu.roll(jnp.broadcast_to(profs[h], (mq, w)), w - mq, axis=1, stride=1, stride_axis=0)
        rows = slice((h % 2) * mq, (h % 2 + 1) * mq)
        far = jnp.full((mq, BIAS_VAR_COL0), tab_ref[h, N_REL - 1], F32)
        o_ref[h // 2, rows, 0:BIAS_VAR_COL0] = jnp.where(valid_far, far, NEG_BIG)
        o_ref[h // 2, rows, BIAS_VAR_COL0:KEY_WIN] = jnp.where(valid_var, tile[:, 0:nvar], NEG_BIG)


def _bias_tiles(table, *, mq, off, col_lo, col_hi, name):
    assert off - (BIAS_VAR_COL0 - 1) >= REL_CLIP
    return pl.pallas_call(
        functools.partial(_bias_kernel, mq=mq, off=off, col_lo=col_lo, col_hi=col_hi),
        out_shape=jax.ShapeDtypeStruct((N_PAIRS, 2 * mq, KEY_WIN), F32),
        in_specs=[pl.BlockSpec(memory_space=pltpu.SMEM)],
        out_specs=pl.BlockSpec(memory_space=pltpu.VMEM),
        name=name,
    )(table)


def _lane_lo(rows):
    return lax.broadcasted_iota(jnp.int32, (rows, PAIR_W), 1) < HEAD_DIM


def _attend(q_pair, k_pair, v_pair, bias_ref, o_put, s_scr, p_scr, *, mq, n_win, min_col):
    lo = _lane_lo(mq)
    units = [(w, p) for w in range(n_win) for p in range(N_PAIRS)]
    for u, (w, p) in enumerate(units):
        qs = q_pair(w, p) * (HEAD_DIM ** -0.5)
        q2 = jnp.concatenate([jnp.where(lo, qs, 0.0), jnp.where(lo, 0.0, qs)], axis=0).astype(BF16)
        s = lax.dot_general(q2, k_pair(w, p), (((1,), (1,)), ((), ())), preferred_element_type=F32) + bias_ref[p]
        first = min_col(w)
        if first is not None:
            col = lax.broadcasted_iota(jnp.int32, (1, KEY_WIN), 1)
            s = jnp.where(col >= first, s, NEG_BIG)
        s_scr[u] = s
    inv_l = []
    rb = min(2 * mq, ATTN_SOFTMAX_ROWS)
    for u in range(len(units)):
        parts = []
        for r0 in range(0, 2 * mq, rb):
            s = s_scr[u, r0:r0 + rb, :]
            e = jnp.exp(s - jnp.max(s, axis=-1, keepdims=True))
            parts.append(1.0 / jnp.sum(e, axis=-1, keepdims=True))
            p_scr[u, r0:r0 + rb, :] = e.astype(BF16)
        inv_l.append(jnp.concatenate(parts, axis=0))
    for u, (w, p) in enumerate(units):
        pv = jnp.dot(p_scr[u], v_pair(w, p), preferred_element_type=F32) * inv_l[u]
        o_put(w, p, jnp.where(lo, pv[:mq], pv[mq:]))


def _attn_prompt_kernel(q_ref, kc_ref, vc_ref, bias_ref, o_ref, kbuf, vbuf, s_scr, p_scr):
    i = pl.program_id(0)
    prev0, cur0, end = CHUNK, CHUNK + ATTN_QB, CHUNK + 2 * ATTN_QB

    @pl.when(i == 0)
    def _():
        zeros = jnp.zeros((cur0, ATTN_WIDTH), BF16)
        kbuf[0:cur0, :] = zeros
        vbuf[0:cur0, :] = zeros

    kbuf[cur0:end, :] = kc_ref[...].astype(BF16)
    vbuf[cur0:end, :] = vc_ref[...].astype(BF16)
    chunks_per_step = ATTN_QB // CHUNK
    lead_chunks = N_PAST_CHUNKS + 1
    n_win = ATTN_CHUNKS_PER_ITER

    def iter_body(j, carry, *, masked):
        sl = lambda p: slice(p * PAIR_W, (p + 1) * PAIR_W)
        r0 = lambda w: pl.multiple_of((j * n_win + w) * CHUNK, CHUNK)
        min_col = lambda w: (lead_chunks - (i * chunks_per_step + j * n_win + w)) * CHUNK if masked else None

        def o_put(w, p, o):
            o_ref[pl.ds(r0(w), CHUNK), sl(p)] = o.astype(BF16)

        _attend(lambda w, p: q_ref[pl.ds(r0(w), CHUNK), sl(p)], lambda w, p: kbuf[pl.ds(r0(w), KEY_WIN), sl(p)],
                lambda w, p: vbuf[pl.ds(r0(w), KEY_WIN), sl(p)], bias_ref, o_put, s_scr, p_scr, mq=CHUNK,
                n_win=n_win, min_col=min_col)
        return carry

    masked_steps = -(-lead_chunks // chunks_per_step)

    @pl.when(i < masked_steps)
    def _():
        lax.fori_loop(0, chunks_per_step // n_win, functools.partial(iter_body, masked=True), 0)

    @pl.when(i >= masked_steps)
    def _():
        lax.fori_loop(0, chunks_per_step // n_win, functools.partial(iter_body, masked=False), 0)

    kbuf[prev0:cur0, :] = kbuf[cur0:end, :]
    vbuf[prev0:cur0, :] = vbuf[cur0:end, :]


def _attn_prompt(z_attn, bias):
    t = z_attn.shape[0]
    assert t % ATTN_QB == 0 and (ATTN_QB // CHUNK) % ATTN_CHUNKS_PER_ITER == 0 and ATTN_QB >= BAND_PAST
    blk = (ATTN_QB, ATTN_WIDTH)
    buf_rows = CHUNK + 2 * ATTN_QB
    s_shape = (ATTN_CHUNKS_PER_ITER * N_PAIRS, 2 * CHUNK, KEY_WIN)
    blocks = 3 * _nbytes(blk, F32) + _nbytes(bias.shape, F32) + _nbytes(blk, BF16)
    scratch = 2 * _nbytes((buf_rows, ATTN_WIDTH), BF16) + _nbytes(s_shape, F32) + _nbytes(s_shape, BF16)
    return pl.pallas_call(
        _attn_prompt_kernel,
        out_shape=jax.ShapeDtypeStruct((t, ATTN_WIDTH), BF16),
        grid=(t // ATTN_QB,),
        in_specs=[pl.BlockSpec(blk, lambda i: (i, 0)),
                  pl.BlockSpec(blk, lambda i: (i, 1)),
                  pl.BlockSpec(blk, lambda i: (i, 2)),
                  pl.BlockSpec(bias.shape, lambda i: (0, 0, 0))],
        out_specs=pl.BlockSpec(blk, lambda i: (i, 0)),
        scratch_shapes=[pltpu.VMEM((buf_rows, ATTN_WIDTH), BF16), pltpu.VMEM((buf_rows, ATTN_WIDTH), BF16),
                        pltpu.VMEM(s_shape, F32), pltpu.VMEM(s_shape, BF16)],
        compiler_params=_params(1, _vmem_limit(blocks, scratch, 4 * _nbytes((2 * CHUNK, KEY_WIN), F32))),
        name="attn_prompt",
    )(z_attn, z_attn, z_attn, bias)


def _attn_sample_kernel(q_ref, kn_ref, vn_ref, kc_ref, vc_ref, bias_ref, o_ref, kbuf, vbuf, s_scr, p_scr, *,
                        s, past):
    tail = jnp.zeros((KEY_WIN - past - s, ATTN_WIDTH), BF16)
    kbuf[0:past, :] = kc_ref[...].astype(BF16)
    vbuf[0:past, :] = vc_ref[...].astype(BF16)
    kbuf[past:past + s, :] = kn_ref[...].astype(BF16)
    vbuf[past:past + s, :] = vn_ref[...].astype(BF16)
    kbuf[past + s:KEY_WIN, :] = tail
    vbuf[past + s:KEY_WIN, :] = tail
    sl = lambda p: slice(p * PAIR_W, (p + 1) * PAIR_W)

    def o_put(w, p, o):
        o_ref[:, sl(p)] = o.astype(BF16)

    _attend(lambda w, p: q_ref[:, sl(p)], lambda w, p: kbuf[:, sl(p)], lambda w, p: vbuf[:, sl(p)], bias_ref,
            o_put, s_scr, p_scr, mq=s, n_win=1, min_col=lambda w: None)


def _attn_sample(z_attn, k_cache, v_cache, bias, *, nb, s):
    past = k_cache.shape[1]
    assert past + s <= KEY_WIN and z_attn.shape[0] == nb * s
    blk = (s, ATTN_WIDTH)
    cblk = (None, past, ATTN_WIDTH)
    s_shape = (N_PAIRS, 2 * s, KEY_WIN)
    blocks = (3 * _nbytes(blk, F32) + 2 * _nbytes((past, ATTN_WIDTH), F32) + _nbytes(bias.shape, F32)
              + _nbytes(blk, BF16))
    scratch = 2 * _nbytes((KEY_WIN, ATTN_WIDTH), BF16) + _nbytes(s_shape, F32) + _nbytes(s_shape, BF16)
    return pl.pallas_call(
        functools.partial(_attn_sample_kernel, s=s, past=past),
        out_shape=jax.ShapeDtypeStruct((nb * s, ATTN_WIDTH), BF16),
        grid=(nb,),
        in_specs=[pl.BlockSpec(blk, lambda b: (b, 0)),
                  pl.BlockSpec(blk, lambda b: (b, 1)),
                  pl.BlockSpec(blk, lambda b: (b, 2)),
                  pl.BlockSpec(cblk, lambda b: (b, 0, 0)),
                  pl.BlockSpec(cblk, lambda b: (b, 0, 0)),
                  pl.BlockSpec(bias.shape, lambda b: (0, 0, 0))],
        out_specs=pl.BlockSpec(blk, lambda b: (b, 0)),
        scratch_shapes=[pltpu.VMEM((KEY_WIN, ATTN_WIDTH), BF16), pltpu.VMEM((KEY_WIN, ATTN_WIDTH), BF16),
                        pltpu.VMEM(s_shape, F32), pltpu.VMEM(s_shape, BF16)],
        compiler_params=_params(1, _vmem_limit(blocks, scratch, 4 * _nbytes((2 * s, KEY_WIN), F32))),
        name="attn_sample",
    )(z_attn, z_attn, z_attn, k_cache, v_cache, bias)


def _bf(x):
    return x.astype(BF16)


def _split2(x):
    hi = _bf(x)
    return hi, _bf(x - hi.astype(F32))


_NN = (((1,), (0,)), ((), ()))
_NT = (((1,), (1,)), ((), ()))


def _mm(x, y, dims=_NN):
    return lax.dot_general(x, y, dims, preferred_element_type=F32)


def _bd(y, lo):
    zero = jnp.zeros_like(y)
    return jnp.concatenate([jnp.where(lo, y, zero), jnp.where(lo, zero, y)], axis=0)


def _seg_sum(x, ones_seg, *, passes):
    r = x.shape[0]
    half = x.shape[1] // (2 * PAIR_W)
    slab = lambda p: x[:, p * PAIR_W:(p + 1) * PAIR_W]
    xx = jnp.concatenate([jnp.concatenate([slab(p) for p in range(half)], axis=0),
                          jnp.concatenate([slab(half + p) for p in range(half)], axis=0)], axis=1)
    hi, lo = _split2(xx)
    s = _mm(hi, ones_seg)
    if passes == 2:
        s = s + _mm(lo, ones_seg)
    return jnp.concatenate([s[p * r:(p + 1) * r, 0:PAIR_W] for p in range(half)]
                           + [s[p * r:(p + 1) * r, PAIR_W:2 * PAIR_W] for p in range(half)], axis=1)


def _sigmoid(x):
    return 1.0 / (1.0 + jnp.exp(-x))


def _sigmoid_tanh(x):
    return 0.5 + 0.5 * jnp.tanh(0.5 * x)


_TOK_F32 = ("at", "rt", "v", "bonus", "g")
_TOK_BF16 = ("at", "bt", "kt", "rt", "v", "bp", "kp")


_RWKV_N_IN = 14
_RWKV_N_OUT = 2
_RWKV_N_SCRATCH = 6


def _rwkv_kernel(*refs, rows, cps, nck, cast_periods):
    n_cast = len(cast_periods)
    (zf_ref, zn_ref, sh_ref, s0_ref, mu_ref, w0_ref, w2cat_ref, a0_ref, gw2_ref, kk_ref, ka_ref, rk_ref, lnw_ref,
     lnb_ref) = refs[:_RWKV_N_IN]
    cast_in = refs[_RWKV_N_IN:_RWKV_N_IN + n_cast]
    o_ref, sfin_ref = refs[_RWKV_N_IN + n_cast:_RWKV_N_IN + n_cast + _RWKV_N_OUT]
    cast_out = refs[_RWKV_N_IN + n_cast + _RWKV_N_OUT:_RWKV_N_IN + 2 * n_cast + _RWKV_N_OUT]
    s_scr, carry_scr, y_scr, f32_scr, bf_scr, pc_scr = refs[-_RWKV_N_SCRATCH:]
    ct = CHUNK_T
    pw = PAIR_W
    tt = cps * ct
    assert rows == ct or cps == 1
    c = pl.program_id(1)

    seg = 2 * pw
    ones_seg = _bf(lax.broadcasted_iota(jnp.int32, (seg, seg), 0) // HEAD_DIM
                   == lax.broadcasted_iota(jnp.int32, (seg, seg), 1) // HEAD_DIM)
    row = lax.broadcasted_iota(jnp.int32, (ct, 1), 0)
    row_all = lax.broadcasted_iota(jnp.int32, (tt, 1), 0)
    tri = _bf(lax.broadcasted_iota(jnp.int32, (ct, ct), 1) <= lax.broadcasted_iota(jnp.int32, (ct, ct), 0))
    t_lane = lax.broadcasted_iota(jnp.int32, (ct, 2 * ct), 1) % ct
    strict = t_lane < row
    incl = t_lane <= row
    lo_t = _lane_lo(ct)
    lo_p = lax.broadcasted_iota(jnp.int32, (ct, 2 * ct), 1) < ct
    lo_t2 = lax.broadcasted_iota(jnp.int32, (ct, 2 * pw), 1) % pw < HEAD_DIM
    diag = (lax.broadcasted_iota(jnp.int32, (HEAD_DIM, pw), 1) % HEAD_DIM
            == lax.broadcasted_iota(jnp.int32, (HEAD_DIM, pw), 0))
    lo_s = _lane_lo(HEAD_DIM)
    n_dbl = int(math.log2(ct))
    assert 1 << n_dbl == ct and 2 * ct == pw
    pairs = range(N_PAIRS)
    cat0 = lambda xs: jnp.concatenate(xs, axis=0)
    cat1 = lambda xs: jnp.concatenate(xs, axis=1)

    def tn(xv, yv):
        zz = _mm(_bf(xv.T), yv)
        return jnp.where(lo_s, zz[:HEAD_DIM], zz[HEAD_DIM:])

    def chunk_pre(zc):
        r = zc[:, 0:RWKV_WIDTH]
        k = zc[:, RWKV_WIDTH:2 * RWKV_WIDTH]
        v = zc[:, 2 * RWKV_WIDTH:3 * RWKV_WIDTH]
        zwa = zc[:, 3 * RWKV_WIDTH:3 * RWKV_WIDTH + DECAY_RANK + AAA_RANK]
        zg = zc[:, 3 * RWKV_WIDTH + DECAY_RANK + AAA_RANK:SHIFT_W]
        lora_in = cat0([jnp.where(lo_t, jnp.tanh(zwa), 0.0), jnp.where(lo_t, 0.0, zwa)])
        lora = _mm(_bf(lora_in), w2cat_ref[...])
        yield
        lw = -math.exp(-0.5) * _sigmoid_tanh(w0_ref[...] + lora[:ct])
        if rows < ct:
            live = row < rows
            lw = jnp.where(live, lw, 0.0)
        l_hi, l_lo = _split2(lw)
        cl = _mm(tri, l_hi) + _mm(tri, l_lo)
        yield
        a = _sigmoid_tanh(a0_ref[...] + lora[ct:])
        g = _mm(_bf(_sigmoid_tanh(zg)), gw2_ref[...])
        yield
        kkr = k * kk_ref[...]
        kn = kkr * lax.rsqrt(jnp.maximum(_seg_sum(kkr * kkr, ones_seg, passes=1), 1e-24))
        yield
        k2 = k * (1.0 + (a - 1.0) * ka_ref[...])
        bonus = _seg_sum(r * k2 * rk_ref[...], ones_seg, passes=1) * v
        yield
        e_in = jnp.exp(cl)
        rt = r * e_in
        pc = e_in[ct - 1:ct, :]
        yield
        at = -kn * jnp.exp(cl - lw)
        yield
        e_inv = jnp.exp(-cl)
        bt = kn * a * e_inv
        kt = k2 * e_inv
        if rows < ct:
            bt = jnp.where(live, bt, 0.0)
            kt = jnp.where(live, kt, 0.0)
        yield
        return dict(at=at, bt=bt, kt=kt, rt=rt, v=v, bp=bt * pc, kp=kt * pc, pc=pc, bonus=bonus, g=g)

    def pre_block(z_ref, carry, wr):
        z = z_ref[...]
        if rows < ct:
            z = jnp.concatenate([z, jnp.zeros((ct - rows, SHIFT_W), F32)], axis=0)
        zprev = jnp.where(row_all == 0, carry, pltpu.roll(z, 1, axis=0))
        last = (cps - 1) * ct + rows - 1
        carry_scr[0:1, :] = z[last:last + 1, :]
        zs = z + (zprev - z) * mu_ref[...]
        yield
        for gi in range(cps):
            t = yield from chunk_pre(zs[gi * ct:(gi + 1) * ct])
            rs = slice(gi * ct, (gi + 1) * ct)
            for i, name in enumerate(_TOK_F32):
                f32_scr[wr, i, rs, :] = t[name]
            yield
            for i, name in enumerate(_TOK_BF16):
                bf_scr[wr, i, rs, :] = _bf(t[name])
            pc_scr[wr, gi:gi + 1, :] = t["pc"]
            yield

    def chunk_stages(rd, gi):
        rs = slice(gi * ct, (gi + 1) * ct)
        f32 = lambda name, p: f32_scr[rd, _TOK_F32.index(name), rs, p * pw:(p + 1) * pw]
        b16 = lambda name, p: bf_scr[rd, _TOK_BF16.index(name), rs, p * pw:(p + 1) * pw]
        v_bd = [_bd(b16("v", p), lo_t) for p in pairs]
        amat = [_mm(cat0([b16("at", p), b16("rt", p)]), cat0([_bd(b16("bt", p), lo_t), _bd(b16("kt", p), lo_t)]),
                    _NT) for p in pairs]
        yield
        a_ab = [jnp.where(strict, amat[p][:ct, :pw], 0.0) for p in pairs]
        a_ak = [_bf(jnp.where(strict, amat[p][:ct, pw:], 0.0)) for p in pairs]
        a_rb = [_bf(jnp.where(incl, amat[p][ct:, :pw], 0.0)) for p in pairs]
        a_rk = [_bf(jnp.where(incl, amat[p][ct:, pw:], 0.0)) for p in pairs]
        x = [cat1([f32("at", p), _mm(a_ak[p], v_bd[p])]) for p in pairs]
        npow = [_bf(a_ab[p]) for p in pairs]
        yield
        for it in range(n_dbl):
            if it + 1 < n_dbl:
                res = [_mm(npow[p], cat1([_bd(_bf(x[p]), lo_t2), _bd(npow[p], lo_p)])) for p in pairs]
                npow = [_bf(res[p][:, 2 * pw:]) for p in pairs]
            else:
                res = [_mm(npow[p], _bd(_bf(x[p]), lo_t2)) for p in pairs]
            x = [x[p] + res[p][:, :2 * pw] for p in pairs]
            yield
        ry = [_mm(a_rb[p], _bd(_bf(x[p]), lo_t2)) for p in pairs]
        rp = [_bf(f32("rt", p) + ry[p][:, :pw]) for p in pairs]
        y0 = [ry[p][:, pw:] + _mm(a_rk[p], v_bd[p]) for p in pairs]
        yield
        m_mat = [_bf(jnp.where(diag, pc_scr[rd, gi:gi + 1, p * pw:(p + 1) * pw], 0.0)
                     + tn(x[p][:, :pw], b16("bp", p))) for p in pairs]
        yield
        n_mat = [tn(cat0([x[p][:, pw:], f32("v", p)]), cat0([b16("bp", p), b16("kp", p)])) for p in pairs]
        return rp, y0, m_mat, n_mat

    def lockstep(*gens):
        done = [None] * len(gens)
        live = list(range(len(gens)))
        while live:
            for i in list(live):
                try:
                    next(gens[i])
                except StopIteration as stop:
                    done[i] = stop.value
                    live.remove(i)
            yield
        return done

    def main_block(rd):
        terms = yield from lockstep(*[chunk_stages(rd, gi) for gi in range(cps)])
        state = [s_scr[:, p * pw:(p + 1) * pw] for p in pairs]
        for gi, (rp, y0, m_mat, n_mat) in enumerate(terms):
            for p in pairs:
                s_old = _bf(state[p])
                y_scr[gi * ct:(gi + 1) * ct, p * pw:(p + 1) * pw] = _mm(rp[p], _bd(s_old, lo_s), _NT) + y0[p]
                state[p] = _mm(s_old, _bd(m_mat[p], lo_s)) + n_mat[p]
            yield
        for p in pairs:
            s_scr[:, p * pw:(p + 1) * pw] = state[p]
        y = y_scr[...]
        mean = _seg_sum(y, ones_seg, passes=2) * (1.0 / HEAD_DIM)
        dv = y - mean
        yield
        var = _seg_sum(dv * dv, ones_seg, passes=1) * (1.0 / HEAD_DIM)
        yn = dv * lax.rsqrt(var + GN_EPS) * lnw_ref[...] + lnb_ref[...]
        out = (yn + f32_scr[rd, _TOK_F32.index("bonus")]) * f32_scr[rd, _TOK_F32.index("g")]
        o_ref[...] = out[:(cps - 1) * ct + rows].astype(BF16)

    def cast_block(parity):
        for src, dst, period in zip(cast_in, cast_out, cast_periods):
            if parity % period == 0:
                dst[...] = _bf(src[...])
            yield

    def run(*gens):
        for _ in lockstep(*gens):
            pass

    @pl.when(c == 0)
    def _():
        for p in pairs:
            s_scr[:, p * pw:(p + 1) * pw] = jnp.concatenate([s0_ref[2 * p], s0_ref[2 * p + 1]], axis=1)
        run(pre_block(zf_ref, sh_ref[...], 0))

    if nck == 1:
        assert n_cast == 0
        run(main_block(0))
    else:
        for parity in (0, 1):
            @pl.when(c % 2 == parity)
            def _(parity=parity):
                run(main_block(parity), pre_block(zn_ref, carry_scr[0:1, :], 1 - parity), cast_block(parity))

    @pl.when(c == nck - 1)
    def _():
        for h in range(N_HEADS):
            sfin_ref[h] = s_scr[:, h * HEAD_DIM:(h + 1) * HEAD_DIM]


def _rwkv(z_rwkv, shift0, s0, lp, *, nb, seq, name, cast_weights=()):
    rows = min(seq, CHUNK_T)
    cps = min(seq // rows, RWKV_CHUNKS_PER_STEP)
    blk_rows = cps * rows
    assert seq % blk_rows == 0 and z_rwkv.shape == (nb * seq, SHIFT_W)
    nck = seq // blk_rows
    tt = cps * CHUNK_T
    row1 = lambda n: pl.BlockSpec((1, n), lambda b, c: (0, 0))
    full = lambda r, n: pl.BlockSpec((r, n), lambda b, c: (0, 0))
    st_blk = pl.BlockSpec((None, N_HEADS, HEAD_DIM, HEAD_DIM), lambda b, c: (b, 0, 0, 0))
    lora_rows = DECAY_RANK + AAA_RANK
    blocks = (2 * _nbytes((blk_rows, SHIFT_W), F32) + 2 * _nbytes((HEAD_DIM, RWKV_WIDTH), F32)
              + _nbytes((lora_rows + GATE_RANK, RWKV_WIDTH), BF16) + _nbytes((16, SHIFT_W), F32)
              + _nbytes((blk_rows, RWKV_WIDTH), BF16))
    stage_shapes = [((2, len(_TOK_F32), tt, RWKV_WIDTH), F32), ((2, len(_TOK_BF16), tt, RWKV_WIDTH), BF16),
                    ((2, 8, RWKV_WIDTH), F32)]
    scratch = (_nbytes((HEAD_DIM, RWKV_WIDTH), F32) + _nbytes((8, SHIFT_W), F32)
               + _nbytes((tt, RWKV_WIDTH), F32) + sum(_nbytes(s, d) for s, d in stage_shapes))
    assert cps <= 8
    bf16_rows = 16
    cast_periods, cast_specs, cast_shapes = [], [], []
    for w in cast_weights:
        assert nb == 1 and nck % 2 == 0
        period = next(p for p in (1, 2) if w.shape[0] % (nck // p) == 0 and (w.shape[0] // (nck // p)) % bf16_rows == 0)
        slab = (w.shape[0] // (nck // period), w.shape[1])
        cast_periods.append(period)
        cast_specs.append(pl.BlockSpec(slab, lambda b, c, period=period: (c // period, 0)))
        cast_shapes.append(jax.ShapeDtypeStruct(w.shape, BF16))
        blocks += _nbytes(slab, F32) + _nbytes(slab, BF16)
    outs = pl.pallas_call(
        functools.partial(_rwkv_kernel, rows=rows, cps=cps, nck=nck, cast_periods=tuple(cast_periods)),
        out_shape=(jax.ShapeDtypeStruct((nb * seq, RWKV_WIDTH), BF16),
                   jax.ShapeDtypeStruct((nb, N_HEADS, HEAD_DIM, HEAD_DIM), F32), *cast_shapes),
        grid=(nb, nck),
        in_specs=[pl.BlockSpec((blk_rows, SHIFT_W), lambda b, c: (b * nck, 0)),
                  pl.BlockSpec((blk_rows, SHIFT_W), lambda b, c: (b * nck + jnp.minimum(c + 1, nck - 1), 0)),
                  pl.BlockSpec((None, 1, SHIFT_W), lambda b, c: (b, 0, 0)),
                  st_blk,
                  row1(SHIFT_W), row1(RWKV_WIDTH), full(lora_rows, RWKV_WIDTH), row1(RWKV_WIDTH),
                  full(GATE_RANK, RWKV_WIDTH), row1(RWKV_WIDTH),
                  row1(RWKV_WIDTH), row1(RWKV_WIDTH), row1(RWKV_WIDTH), row1(RWKV_WIDTH), *cast_specs],
        out_specs=(pl.BlockSpec((blk_rows, RWKV_WIDTH), lambda b, c: (b * nck + c, 0)), st_blk, *cast_specs),
        scratch_shapes=[pltpu.VMEM((HEAD_DIM, RWKV_WIDTH), F32), pltpu.VMEM((8, SHIFT_W), F32),
                        pltpu.VMEM((tt, RWKV_WIDTH), F32)] + [pltpu.VMEM(s, d) for s, d in stage_shapes],
        compiler_params=_params(2, _vmem_limit(blocks, scratch, 64 * _nbytes((tt, RWKV_WIDTH), F32))),
        name=name,
    )(z_rwkv, z_rwkv, shift0, s0, lp["mu"], lp["w0"], lp["w2cat"], lp["a0"], lp["g_w2"],
      lp["k_k"], lp["k_a"], lp["r_k"], lp["lnx_w"], lp["lnx_b"], *cast_weights)
    return outs[0], outs[1], list(outs[2:])


def _oproj_kernel(a_ref, r_ref, wa_ref, wr_ref, x_ref, g_ref, h_ref, n_ref):
    h = (x_ref[...] + jnp.dot(a_ref[...], wa_ref[...], preferred_element_type=F32)
         + jnp.dot(r_ref[...], wr_ref[...], preferred_element_type=F32))
    h_ref[...] = h
    n_ref[...] = _rms(h, g_ref[...]).astype(BF16)


def _oproj(attn, rwkv, w_o, x, g, *, tm, name):
    t = x.shape[0]
    assert t % tm == 0
    half = (ATTN_WIDTH, D_MODEL)
    blocks = (2 * _nbytes((tm, ATTN_WIDTH), BF16) + 2 * _nbytes(half, BF16) + 2 * _nbytes((tm, D_MODEL), F32)
              + _nbytes((tm, D_MODEL), BF16))
    return pl.pallas_call(
        _oproj_kernel,
        out_shape=(jax.ShapeDtypeStruct((t, D_MODEL), F32), jax.ShapeDtypeStruct((t, D_MODEL), BF16)),
        grid=(t // tm,),
        in_specs=[pl.BlockSpec((tm, ATTN_WIDTH), lambda i: (i, 0)),
                  pl.BlockSpec((tm, RWKV_WIDTH), lambda i: (i, 0)),
                  pl.BlockSpec(half, lambda i: (0, 0)),
                  pl.BlockSpec(half, lambda i: (1, 0)),
                  pl.BlockSpec((tm, D_MODEL), lambda i: (i, 0)),
                  pl.BlockSpec((1, D_MODEL), lambda i: (0, 0))],
        out_specs=(pl.BlockSpec((tm, D_MODEL), lambda i: (i, 0)), pl.BlockSpec((tm, D_MODEL), lambda i: (i, 0))),
        compiler_params=_params(1, _vmem_limit(blocks, 0, 2 * _nbytes((tm, D_MODEL), F32))),
        name=name,
    )(attn, rwkv, w_o, w_o, x, g)


def _ffn_kernel(n_ref, h_ref, wg_ref, wu_ref, wd_ref, g_ref, o_ref):
    j = pl.program_id(1)

    @pl.when(j == 0)
    def _():
        o_ref[...] = h_ref[...]

    n = n_ref[...]
    gate = jnp.dot(n, wg_ref[...], preferred_element_type=F32)
    up = jnp.dot(n, wu_ref[...], preferred_element_type=F32)
    act = (gate * _sigmoid(gate) * up).astype(BF16)
    o_ref[...] += jnp.dot(act, wd_ref[...], preferred_element_type=F32)

    @pl.when(j == pl.num_programs(1) - 1)
    def _():
        o_ref[...] = _rms(o_ref[...], g_ref[...])


def _ffn(n2, h, wg, wu, wd, g, *, tm, tf, name):
    t = h.shape[0]
    assert t % tm == 0 and D_FF % tf == 0
    blocks = (_nbytes((tm, D_MODEL), BF16) + 2 * _nbytes((tm, D_MODEL), F32) + 3 * _nbytes((D_MODEL, tf), BF16))
    temps = 3 * _nbytes((tm, tf), F32) + _nbytes((tm, tf), BF16)
    return pl.pallas_call(
        _ffn_kernel,
        out_shape=jax.ShapeDtypeStruct((t, D_MODEL), F32),
        grid=(t // tm, D_FF // tf),
        in_specs=[pl.BlockSpec((tm, D_MODEL), lambda i, j: (i, 0)),
                  pl.BlockSpec((tm, D_MODEL), lambda i, j: (i, 0)),
                  pl.BlockSpec((D_MODEL, tf), lambda i, j: (0, j)),
                  pl.BlockSpec((D_MODEL, tf), lambda i, j: (0, j)),
                  pl.BlockSpec((tf, D_MODEL), lambda i, j: (j, 0)),
                  pl.BlockSpec((1, D_MODEL), lambda i, j: (0, 0))],
        out_specs=pl.BlockSpec((tm, D_MODEL), lambda i, j: (i, 0)),
        compiler_params=_params(2, _vmem_limit(blocks, 0, temps)),
        name=name,
    )(n2, h, wg, wu, wd, g)


_LATE_WEIGHTS = ("w_o", "w_gate", "w_up", "w_down")


def _trunk(x, lp, wts, attn_fn, shift0, s0, *, nb, seq, tag):
    t = nb * seq
    tm = min(t, 1024)
    z_attn = _inproj(x, lp["ln1"], wts["w_in"], col0=0, n=3 * ATTN_WIDTH, tm=tm, tn=1024,
                     name=f"inproj_attn_{tag}")
    z_rwkv = _inproj(x, lp["ln1"], wts["w_in"], col0=3 * ATTN_WIDTH, n=SHIFT_W, tm=tm, tn=SHIFT_W // 2,
                     name=f"inproj_rwkv_{tag}")
    attn = attn_fn(z_attn)
    pending = [k for k in _LATE_WEIGHTS if wts[k].dtype != BF16]
    rwkv, s_fin, converted = _rwkv(z_rwkv, shift0, s0, lp, nb=nb, seq=seq, name=f"rwkv_{tag}",
                                   cast_weights=[wts[k] for k in pending])
    wts.update(zip(pending, converted))
    tm2 = min(t, 512)
    h, n2 = _oproj(attn, rwkv, wts["w_o"], x, lp["ln2"], tm=tm2, name=f"oproj_{tag}")
    y = _ffn(n2, h, wts["w_gate"], wts["w_up"], wts["w_down"], lp["final_norm"], tm=min(t, 1024), tf=512,
             name=f"ffn_{tag}")
    return y, z_attn, z_rwkv, s_fin


def kernel(x_prompt, x_sample, cache_attn_k, cache_attn_v, state_rwkv_wkv, state_rwkv_shift, ln1, w_in, rel_table,
           mu, w0, w_w2, a0, a_w2, g_w2, k_k, k_a, r_k, lnx_w, lnx_b, w_o, ln2, w_gate, w_up, w_down, final_norm):
    depth = w_in.shape[0]
    assert depth == 1, "final RMSNorm is fused into the (single) layer's FFN call"
    b, t, _ = x_prompt.shape
    bd, s, _ = x_sample.shape
    past = cache_attn_k.shape[2]
    n_keep = min(BAND_PAST, t)
    l = 0
    row = lambda p: p[l].reshape(1, -1)
    lp = dict(ln1=row(ln1), mu=row(mu), w0=row(w0), a0=row(a0),
              w2cat=jnp.concatenate([w_w2[l], a_w2[l]], axis=0).astype(BF16), g_w2=g_w2[l].astype(BF16),
              k_k=row(k_k), k_a=row(k_a), r_k=row(r_k), lnx_w=row(lnx_w), lnx_b=row(lnx_b), ln2=row(ln2),
              final_norm=final_norm.reshape(1, -1))
    wts = dict(w_in=w_in[l].astype(BF16), w_o=w_o[l], w_gate=w_gate[l], w_up=w_up[l], w_down=w_down[l])
    table = rel_table[l]

    bias_p = _bias_tiles(table, mq=CHUNK, off=BAND_PAST + CHUNK, col_lo=CHUNK, col_hi=KEY_WIN, name="bias_prompt")
    bias_s = _bias_tiles(table, mq=s, off=past, col_lo=0, col_hi=past + s, name="bias_sample")

    assert b == 1
    yp, zap, zrp, wkv_p = _trunk(
        x_prompt.reshape(b * t, D_MODEL), lp, wts, lambda z: _attn_prompt(z, bias_p),
        jnp.zeros((b, 1, SHIFT_W), F32), jnp.zeros((b, N_HEADS, HEAD_DIM, HEAD_DIM), F32), nb=b, seq=t, tag="p")
    kc = cache_attn_k[l].reshape(bd, past, ATTN_WIDTH)
    vc = cache_attn_v[l].reshape(bd, past, ATTN_WIDTH)
    ys, zas, zrs, wkv_s = _trunk(
        x_sample.reshape(bd * s, D_MODEL), lp, wts, lambda z: _attn_sample(z, kc, vc, bias_s, nb=bd, s=s),
        state_rwkv_shift[l], state_rwkv_wkv[l], nb=bd, seq=s, tag="s")

    heads = lambda z, n, tt: z.reshape(n, tt, N_HEADS, HEAD_DIM)
    zap = zap.reshape(b, t, 3 * ATTN_WIDTH)
    zas = zas.reshape(bd, s, 3 * ATTN_WIDTH)
    k_p = heads(zap[:, t - n_keep:, ATTN_WIDTH:2 * ATTN_WIDTH], b, n_keep)
    v_p = heads(zap[:, t - n_keep:, 2 * ATTN_WIDTH:], b, n_keep)
    k_s = heads(zas[:, :, ATTN_WIDTH:2 * ATTN_WIDTH], bd, s)
    v_s = heads(zas[:, :, 2 * ATTN_WIDTH:], bd, s)
    sh_p = zrp.reshape(b, t, SHIFT_W)[:, t - 1:]
    sh_s = zrs.reshape(bd, s, SHIFT_W)[:, s - 1:]
    return (yp.reshape(b, t, D_MODEL), ys.reshape(bd, s, D_MODEL),
            k_p[None], v_p[None], wkv_p[None], sh_p[None],
            k_s[None], v_s[None], wkv_s[None], sh_s[None])
```

```python
import functools
import math

import jax
import jax.numpy as jnp
from jax import lax
from jax.experimental import pallas as pl
from jax.experimental.pallas import tpu as pltpu

F32 = jnp.float32
BF16 = jnp.bfloat16

D_MODEL = 2048
CHUNK = 64
N_PAST_CHUNKS = 8
BAND_PAST = N_PAST_CHUNKS * CHUNK
ATTN_WIDTH = 1024
HEAD_DIM = 64
N_HEADS = 16
REL_CLIP = 128
N_REL = 2 * REL_CLIP + 1
RWKV_WIDTH = 1024
DECAY_RANK = 64
AAA_RANK = 64
GATE_RANK = 128
SHIFT_W = 3 * RWKV_WIDTH + DECAY_RANK + AAA_RANK + GATE_RANK
D_FF = 5632
RMS_EPS = 1e-6
GN_EPS = 64e-5

V7X_LANES = 128
V7X_VMEM_BYTES = 64 * 1024 * 1024
V7X_SCOPED_VMEM_CAP_BYTES = 60000 * 1024

PAIR_W = 2 * HEAD_DIM
N_PAIRS = N_HEADS // 2
CHUNK_T = 64
KEY_WIN = (N_PAST_CHUNKS + 2) * CHUNK
NEG_BIG = -1e30
ATTN_QB = 512
ATTN_CHUNKS_PER_ITER = 2
ATTN_SOFTMAX_ROWS = 128
RWKV_CHUNKS_PER_STEP = 2

assert PAIR_W == V7X_LANES


def _vmem_limit(pipelined_bytes, scratch_bytes, temp_bytes):
    del pipelined_bytes, scratch_bytes, temp_bytes
    return V7X_SCOPED_VMEM_CAP_BYTES


def _nbytes(shape, dtype):
    return math.prod(shape) * jnp.dtype(dtype).itemsize


def _params(n_grid, vmem, flags=None):
    return pltpu.CompilerParams(dimension_semantics=("arbitrary",) * n_grid, vmem_limit_bytes=vmem, flags=flags)


def _rms(x, g):
    return x * lax.rsqrt(jnp.mean(x * x, axis=-1, keepdims=True) + RMS_EPS) * g


def _inproj_kernel(*refs, with_side):
    x_ref, g_ref, w_ref = refs[:3]
    o_ref, n_ref = (refs[4], refs[6]) if with_side else (refs[3], refs[4])

    @pl.when(pl.program_id(1) == 0)
    def _():
        n_ref[...] = _rms(x_ref[...], g_ref[...]).astype(BF16)
        if with_side:
            refs[5][...] = refs[3][...].astype(BF16)

    o_ref[...] = jnp.dot(n_ref[...], w_ref[...].astype(BF16), preferred_element_type=F32)


def _inproj(x, g, w, *, col0, n, tm, tn, name, side=None):
    t, d = x.shape
    assert t % tm == 0 and n % tn == 0 and col0 % V7X_LANES == 0 and tn % V7X_LANES == 0 and col0 + n <= w.shape[1]
    lanes = lambda e: (e // V7X_LANES) * V7X_LANES
    blocks = _nbytes((tm, d), F32) + _nbytes((d, tn), w.dtype) + _nbytes((tm, tn), F32)
    in_specs = [pl.BlockSpec((tm, d), lambda i, j: (i, 0)),
                pl.BlockSpec((1, d), lambda i, j: (0, 0)),
                pl.BlockSpec((pl.Element(d), pl.Element(tn)),
                             lambda i, j: (0, (col0 // V7X_LANES + j * (tn // V7X_LANES)) * V7X_LANES))]
    out_shape = [jax.ShapeDtypeStruct((t, n), F32)]
    out_specs = [pl.BlockSpec((tm, tn), lambda i, j: (i, j))]
    operands = [x, g, w]
    if side is not None:
        c0, width = side
        slab = d // (t // tm)
        assert w.dtype == F32 and d % (t // tm) == 0 and slab % 16 == 0 and c0 % V7X_LANES == 0
        in_specs.append(pl.BlockSpec((pl.Element(slab), pl.Element(width)), lambda i, j: (i * slab, lanes(c0))))
        out_shape.append(jax.ShapeDtypeStruct((d, width), BF16))
        out_specs.append(pl.BlockSpec((slab, width), lambda i, j: (i, 0)))
        operands.append(w)
        blocks += _nbytes((slab, width), F32) + _nbytes((slab, width), BF16)
    outs = pl.pallas_call(
        functools.partial(_inproj_kernel, with_side=side is not None),
        out_shape=out_shape,
        grid=(t // tm, n // tn),
        in_specs=in_specs,
        out_specs=out_specs,
        scratch_shapes=[pltpu.VMEM((tm, d), BF16)],
        compiler_params=_params(2, _vmem_limit(blocks, _nbytes((tm, d), BF16),
                                               2 * _nbytes((tm, d), F32) + _nbytes((d, tn), BF16))),
        name=name,
    )(*operands)
    return outs if side is not None else outs[0]


BIAS_VAR_COL0 = KEY_WIN - 2 * V7X_LANES
BIAS_DIAG_W = 512


def _bias_kernel(tab_ref, o_ref, *, mq, off, col_lo, col_hi):
    nvar = KEY_WIN - BIAS_VAR_COL0
    w = BIAS_DIAG_W
    assert nvar + mq <= w
    lane = lax.broadcasted_iota(jnp.int32, (1, w), 1)
    idx = jnp.clip(off - (BIAS_VAR_COL0 - mq + lane), -REL_CLIP, REL_CLIP) + REL_CLIP
    idx_min = max(0, min(REL_CLIP, off - (BIAS_VAR_COL0 - mq + w - 1)) + REL_CLIP)
    heads = range(N_HEADS)

    def body(j, profs):
        hit = idx == j
        return tuple(jnp.where(hit, tab_ref[h, j], prof) for h, prof in zip(heads, profs))

    init = tuple(jnp.full((1, w), tab_ref[h, N_REL - 1], F32) for h in heads)
    profs = lax.fori_loop(idx_min, N_REL - 1, body, init)

    c = lax.broadcasted_iota(jnp.int32, (mq, nvar), 1) + BIAS_VAR_COL0
    valid_var = (c >= col_lo) & (c < col_hi)
    c_far = lax.broadcasted_iota(jnp.int32, (mq, BIAS_VAR_COL0), 1)
    valid_far = (c_far >= col_lo) & (c_far < col_hi)
    for h in heads:
        tile = pltpu.roll(jnp.broadcast_to(profs[h], (mq, w)), w - mq, axis=1, stride=1, stride_axis=0)
        rows = slice((h % 2) * mq, (h % 2 + 1) * mq)
        far = jnp.full((mq, BIAS_VAR_COL0), tab_ref[h, N_REL - 1], F32)
        o_ref[h // 2, rows, 0:BIAS_VAR_COL0] = jnp.where(valid_far, far, NEG_BIG)
        o_ref[h // 2, rows, BIAS_VAR_COL0:KEY_WIN] = jnp.where(valid_var, tile[:, 0:nvar], NEG_BIG)


def _bias_tiles(table, *, mq, off, col_lo, col_hi, name):
    assert off - (BIAS_VAR_COL0 - 1) >= REL_CLIP
    return pl.pallas_call(
        functools.partial(_bias_kernel, mq=mq, off=off, col_lo=col_lo, col_hi=col_hi),
        out_shape=jax.ShapeDtypeStruct((N_PAIRS, 2 * mq, KEY_WIN), F32),
        in_specs=[pl.BlockSpec(memory_space=pltpu.SMEM)],
        out_specs=pl.BlockSpec(memory_space=pltpu.VMEM),
        name=name,
    )(table)


def _lane_lo(rows):
    return lax.broadcasted_iota(jnp.int32, (rows, PAIR_W), 1) < HEAD_DIM


def _attend(q_pair, k_pair, v_pair, bias_ref, o_put, s_scr, p_scr, *, mq, n_win, min_col):
    lo = _lane_lo(mq)
    units = [(w, p) for w in range(n_win) for p in range(N_PAIRS)]
    for u, (w, p) in enumerate(units):
        qs = q_pair(w, p) * (HEAD_DIM ** -0.5)
        q2 = jnp.concatenate([jnp.where(lo, qs, 0.0), jnp.where(lo, 0.0, qs)], axis=0).astype(BF16)
        s = lax.dot_general(q2, k_pair(w, p), (((1,), (1,)), ((), ())), preferred_element_type=F32) + bias_ref[p]
        first = min_col(w)
        if first is not None:
            col = lax.broadcasted_iota(jnp.int32, (1, KEY_WIN), 1)
            s = jnp.where(col >= first, s, NEG_BIG)
        s_scr[u] = s
    inv_l = []
    rb = min(2 * mq, ATTN_SOFTMAX_ROWS)
    for u in range(len(units)):
        parts = []
        for r0 in range(0, 2 * mq, rb):
            s = s_scr[u, r0:r0 + rb, :]
            e = jnp.exp(s - jnp.max(s, axis=-1, keepdims=True))
            parts.append(1.0 / jnp.sum(e, axis=-1, keepdims=True))
            p_scr[u, r0:r0 + rb, :] = e.astype(BF16)
        inv_l.append(jnp.concatenate(parts, axis=0))
    for u, (w, p) in enumerate(units):
        pv = jnp.dot(p_scr[u], v_pair(w, p), preferred_element_type=F32) * inv_l[u]
        o_put(w, p, jnp.where(lo, pv[:mq], pv[mq:]))


def _attn_prompt_kernel(q_ref, kc_ref, vc_ref, bias_ref, o_ref, kbuf, vbuf, s_scr, p_scr):
    i = pl.program_id(0)
    prev0, cur0, end = CHUNK, CHUNK + ATTN_QB, CHUNK + 2 * ATTN_QB

    @pl.when(i == 0)
    def _():
        zeros = jnp.zeros((cur0, ATTN_WIDTH), BF16)
        kbuf[0:cur0, :] = zeros
        vbuf[0:cur0, :] = zeros

    kbuf[cur0:end, :] = kc_ref[...].astype(BF16)
    vbuf[cur0:end, :] = vc_ref[...].astype(BF16)
    chunks_per_step = ATTN_QB // CHUNK
    lead_chunks = N_PAST_CHUNKS + 1
    n_win = ATTN_CHUNKS_PER_ITER

    def iter_body(j, carry, *, masked):
        sl = lambda p: slice(p * PAIR_W, (p + 1) * PAIR_W)
        r0 = lambda w: pl.multiple_of((j * n_win + w) * CHUNK, CHUNK)
        min_col = lambda w: (lead_chunks - (i * chunks_per_step + j * n_win + w)) * CHUNK if masked else None

        def o_put(w, p, o):
            o_ref[pl.ds(r0(w), CHUNK), sl(p)] = o.astype(BF16)

        _attend(lambda w, p: q_ref[pl.ds(r0(w), CHUNK), sl(p)], lambda w, p: kbuf[pl.ds(r0(w), KEY_WIN), sl(p)],
                lambda w, p: vbuf[pl.ds(r0(w), KEY_WIN), sl(p)], bias_ref, o_put, s_scr, p_scr, mq=CHUNK,
                n_win=n_win, min_col=min_col)
        return carry

    masked_steps = -(-lead_chunks // chunks_per_step)

    @pl.when(i < masked_steps)
    def _():
        lax.fori_loop(0, chunks_per_step // n_win, functools.partial(iter_body, masked=True), 0)

    @pl.when(i >= masked_steps)
    def _():
        lax.fori_loop(0, chunks_per_step // n_win, functools.partial(iter_body, masked=False), 0)

    kbuf[prev0:cur0, :] = kbuf[cur0:end, :]
    vbuf[prev0:cur0, :] = vbuf[cur0:end, :]


def _attn_prompt(z_attn, bias):
    t = z_attn.shape[0]
    assert t % ATTN_QB == 0 and (ATTN_QB // CHUNK) % ATTN_CHUNKS_PER_ITER == 0 and ATTN_QB >= BAND_PAST
    blk = (ATTN_QB, ATTN_WIDTH)
    buf_rows = CHUNK + 2 * ATTN_QB
    s_shape = (ATTN_CHUNKS_PER_ITER * N_PAIRS, 2 * CHUNK, KEY_WIN)
    blocks = 3 * _nbytes(blk, F32) + _nbytes(bias.shape, F32) + _nbytes(blk, BF16)
    scratch = 2 * _nbytes((buf_rows, ATTN_WIDTH), BF16) + _nbytes(s_shape, F32) + _nbytes(s_shape, BF16)
    return pl.pallas_call(
        _attn_prompt_kernel,
        out_shape=jax.ShapeDtypeStruct((t, ATTN_WIDTH), BF16),
        grid=(t // ATTN_QB,),
        in_specs=[pl.BlockSpec(blk, lambda i: (i, 0)),
                  pl.BlockSpec(blk, lambda i: (i, 1)),
                  pl.BlockSpec(blk, lambda i: (i, 2)),
                  pl.BlockSpec(bias.shape, lambda i: (0, 0, 0))],
        out_specs=pl.BlockSpec(blk, lambda i: (i, 0)),
        scratch_shapes=[pltpu.VMEM((buf_rows, ATTN_WIDTH), BF16), pltpu.VMEM((buf_rows, ATTN_WIDTH), BF16),
                        pltpu.VMEM(s_shape, F32), pltpu.VMEM(s_shape, BF16)],
        compiler_params=_params(1, _vmem_limit(blocks, scratch, 4 * _nbytes((2 * CHUNK, KEY_WIN), F32))),
        name="attn_prompt",
    )(z_attn, z_attn, z_attn, bias)


def _attn_sample_kernel(q_ref, kn_ref, vn_ref, kc_ref, vc_ref, bias_ref, o_ref, kbuf, vbuf, s_scr, p_scr, *,
                        s, past):
    tail = jnp.zeros((KEY_WIN - past - s, ATTN_WIDTH), BF16)
    kbuf[0:past, :] = kc_ref[...].astype(BF16)
    vbuf[0:past, :] = vc_ref[...].astype(BF16)
    kbuf[past:past + s, :] = kn_ref[...].astype(BF16)
    vbuf[past:past + s, :] = vn_ref[...].astype(BF16)
    kbuf[past + s:KEY_WIN, :] = tail
    vbuf[past + s:KEY_WIN, :] = tail
    sl = lambda p: slice(p * PAIR_W, (p + 1) * PAIR_W)

    def o_put(w, p, o):
        o_ref[:, sl(p)] = o.astype(BF16)

    _attend(lambda w, p: q_ref[:, sl(p)], lambda w, p: kbuf[:, sl(p)], lambda w, p: vbuf[:, sl(p)], bias_ref,
            o_put, s_scr, p_scr, mq=s, n_win=1, min_col=lambda w: None)


def _attn_sample(z_attn, k_cache, v_cache, bias, *, nb, s):
    past = k_cache.shape[1]
    assert past + s <= KEY_WIN and z_attn.shape[0] == nb * s
    blk = (s, ATTN_WIDTH)
    cblk = (None, past, ATTN_WIDTH)
    s_shape = (N_PAIRS, 2 * s, KEY_WIN)
    blocks = (3 * _nbytes(blk, F32) + 2 * _nbytes((past, ATTN_WIDTH), F32) + _nbytes(bias.shape, F32)
              + _nbytes(blk, BF16))
    scratch = 2 * _nbytes((KEY_WIN, ATTN_WIDTH), BF16) + _nbytes(s_shape, F32) + _nbytes(s_shape, BF16)
    return pl.pallas_call(
        functools.partial(_attn_sample_kernel, s=s, past=past),
        out_shape=jax.ShapeDtypeStruct((nb * s, ATTN_WIDTH), BF16),
        grid=(nb,),
        in_specs=[pl.BlockSpec(blk, lambda b: (b, 0)),
                  pl.BlockSpec(blk, lambda b: (b, 1)),
                  pl.BlockSpec(blk, lambda b: (b, 2)),
                  pl.BlockSpec(cblk, lambda b: (b, 0, 0)),
                  pl.BlockSpec(cblk, lambda b: (b, 0, 0)),
                  pl.BlockSpec(bias.shape, lambda b: (0, 0, 0))],
        out_specs=pl.BlockSpec(blk, lambda b: (b, 0)),
        scratch_shapes=[pltpu.VMEM((KEY_WIN, ATTN_WIDTH), BF16), pltpu.VMEM((KEY_WIN, ATTN_WIDTH), BF16),
                        pltpu.VMEM(s_shape, F32), pltpu.VMEM(s_shape, BF16)],
        compiler_params=_params(1, _vmem_limit(blocks, scratch, 4 * _nbytes((2 * s, KEY_WIN), F32))),
        name="attn_sample",
    )(z_attn, z_attn, z_attn, k_cache, v_cache, bias)


def _bf(x):
    return x.astype(BF16)


def _split2(x):
    hi = _bf(x)
    return hi, _bf(x - hi.astype(F32))


_NN = (((1,), (0,)), ((), ()))
_NT = (((1,), (1,)), ((), ()))


def _mm(x, y, dims=_NN):
    return lax.dot_general(x, y, dims, preferred_element_type=F32)


def _bd(y, lo):
    zero = jnp.zeros_like(y)
    return jnp.concatenate([jnp.where(lo, y, zero), jnp.where(lo, zero, y)], axis=0)


def _seg_sum(x, ones_seg, *, passes):
    r = x.shape[0]
    half = x.shape[1] // (2 * PAIR_W)
    slab = lambda p: x[:, p * PAIR_W:(p + 1) * PAIR_W]
    xx = jnp.concatenate([jnp.concatenate([slab(p) for p in range(half)], axis=0),
                          jnp.concatenate([slab(half + p) for p in range(half)], axis=0)], axis=1)
    hi, lo = _split2(xx)
    s = _mm(hi, ones_seg)
    if passes == 2:
        s = s + _mm(lo, ones_seg)
    return jnp.concatenate([s[p * r:(p + 1) * r, 0:PAIR_W] for p in range(half)]
                           + [s[p * r:(p + 1) * r, PAIR_W:2 * PAIR_W] for p in range(half)], axis=1)


def _sigmoid(x):
    return 1.0 / (1.0 + jnp.exp(-x))


def _sigmoid_tanh(x):
    return 0.5 + 0.5 * jnp.tanh(0.5 * x)


_TOK_F32 = ("at", "rt", "v", "bonus", "g")
_TOK_BF16 = ("at", "bt", "kt", "rt", "v", "bp", "kp")


_RWKV_N_IN = 14
_RWKV_N_OUT = 2
_RWKV_N_SCRATCH = 6


def _rwkv_kernel(*refs, rows, cps, nck, cast_periods):
    n_cast = len(cast_periods)
    (zf_ref, zn_ref, sh_ref, s0_ref, mu_ref, w0_ref, w2cat_ref, a0_ref, gw2_ref, kk_ref, ka_ref, rk_ref, lnw_ref,
     lnb_ref) = refs[:_RWKV_N_IN]
    cast_in = refs[_RWKV_N_IN:_RWKV_N_IN + n_cast]
    o_ref, sfin_ref = refs[_RWKV_N_IN + n_cast:_RWKV_N_IN + n_cast + _RWKV_N_OUT]
    cast_out = refs[_RWKV_N_IN + n_cast + _RWKV_N_OUT:_RWKV_N_IN + 2 * n_cast + _RWKV_N_OUT]
    s_scr, carry_scr, y_scr, f32_scr, bf_scr, pc_scr = refs[-_RWKV_N_SCRATCH:]
    ct = CHUNK_T
    pw = PAIR_W
    tt = cps * ct
    assert rows == ct or cps == 1
    c = pl.program_id(1)

    seg = 2 * pw
    ones_seg = _bf(lax.broadcasted_iota(jnp.int32, (seg, seg), 0) // HEAD_DIM
                   == lax.broadcasted_iota(jnp.int32, (seg, seg), 1) // HEAD_DIM)
    row = lax.broadcasted_iota(jnp.int32, (ct, 1), 0)
    row_all = lax.broadcasted_iota(jnp.int32, (tt, 1), 0)
    tri = _bf(lax.broadcasted_iota(jnp.int32, (ct, ct), 1) <= lax.broadcasted_iota(jnp.int32, (ct, ct), 0))
    t_lane = lax.broadcasted_iota(jnp.int32, (ct, 2 * ct), 1) % ct
    strict = t_lane < row
    incl = t_lane <= row
    lo_t = _lane_lo(ct)
    lo_p = lax.broadcasted_iota(jnp.int32, (ct, 2 * ct), 1) < ct
    lo_t2 = lax.broadcasted_iota(jnp.int32, (ct, 2 * pw), 1) % pw < HEAD_DIM
    diag = (lax.broadcasted_iota(jnp.int32, (HEAD_DIM, pw), 1) % HEAD_DIM
            == lax.broadcasted_iota(jnp.int32, (HEAD_DIM, pw), 0))
    lo_s = _lane_lo(HEAD_DIM)
    n_dbl = int(math.log2(ct))
    assert 1 << n_dbl == ct and 2 * ct == pw
    pairs = range(N_PAIRS)
    cat0 = lambda xs: jnp.concatenate(xs, axis=0)
    cat1 = lambda xs: jnp.concatenate(xs, axis=1)

    def tn(xv, yv):
        zz = _mm(_bf(xv.T), yv)
        return jnp.where(lo_s, zz[:HEAD_DIM], zz[HEAD_DIM:])

    def chunk_pre(zc):
        r = zc[:, 0:RWKV_WIDTH]
        k = zc[:, RWKV_WIDTH:2 * RWKV_WIDTH]
        v = zc[:, 2 * RWKV_WIDTH:3 * RWKV_WIDTH]
        zwa = zc[:, 3 * RWKV_WIDTH:3 * RWKV_WIDTH + DECAY_RANK + AAA_RANK]
        zg = zc[:, 3 * RWKV_WIDTH + DECAY_RANK + AAA_RANK:SHIFT_W]
        lora_in = cat0([jnp.where(lo_t, jnp.tanh(zwa), 0.0), jnp.where(lo_t, 0.0, zwa)])
        lora = _mm(_bf(lora_in), w2cat_ref[...])
        yield
        lw = -math.exp(-0.5) * _sigmoid_tanh(w0_ref[...] + lora[:ct])
        if rows < ct:
            live = row < rows
            lw = jnp.where(live, lw, 0.0)
        l_hi, l_lo = _split2(lw)
        cl = _mm(tri, l_hi) + _mm(tri, l_lo)
        yield
        a = _sigmoid_tanh(a0_ref[...] + lora[ct:])
        g = _mm(_bf(_sigmoid_tanh(zg)), gw2_ref[...])
        yield
        kkr = k * kk_ref[...]
        kn = kkr * lax.rsqrt(jnp.maximum(_seg_sum(kkr * kkr, ones_seg, passes=1), 1e-24))
        yield
        k2 = k * (1.0 + (a - 1.0) * ka_ref[...])
        bonus = _seg_sum(r * k2 * rk_ref[...], ones_seg, passes=1) * v
        yield
        e_in = jnp.exp(cl)
        rt = r * e_in
        pc = e_in[ct - 1:ct, :]
        yield
        at = -kn * jnp.exp(cl - lw)
        yield
        e_inv = jnp.exp(-cl)
        bt = kn * a * e_inv
        kt = k2 * e_inv
        if rows < ct:
            bt = jnp.where(live, bt, 0.0)
            kt = jnp.where(live, kt, 0.0)
        yield
        return dict(at=at, bt=bt, kt=kt, rt=rt, v=v, bp=bt * pc, kp=kt * pc, pc=pc, bonus=bonus, g=g)

    def pre_block(z_ref, carry, wr):
        z = z_ref[...]
        if rows < ct:
            z = jnp.concatenate([z, jnp.zeros((ct - rows, SHIFT_W), F32)], axis=0)
        zprev = jnp.where(row_all == 0, carry, pltpu.roll(z, 1, axis=0))
        last = (cps - 1) * ct + rows - 1
        carry_scr[0:1, :] = z[last:last + 1, :]
        zs = z + (zprev - z) * mu_ref[...]
        yield
        for gi in range(cps):
            t = yield from chunk_pre(zs[gi * ct:(gi + 1) * ct])
            rs = slice(gi * ct, (gi + 1) * ct)
            for i, name in enumerate(_TOK_F32):
                f32_scr[wr, i, rs, :] = t[name]
            yield
            for i, name in enumerate(_TOK_BF16):
                bf_scr[wr, i, rs, :] = _bf(t[name])
            pc_scr[wr, gi:gi + 1, :] = t["pc"]
            yield

    def chunk_stages(rd, gi):
        rs = slice(gi * ct, (gi + 1) * ct)
        f32 = lambda name, p: f32_scr[rd, _TOK_F32.index(name), rs, p * pw:(p + 1) * pw]
        b16 = lambda name, p: bf_scr[rd, _TOK_BF16.index(name), rs, p * pw:(p + 1) * pw]
        v_bd = [_bd(b16("v", p), lo_t) for p in pairs]
        amat = [_mm(cat0([b16("at", p), b16("rt", p)]), cat0([_bd(b16("bt", p), lo_t), _bd(b16("kt", p), lo_t)]),
                    _NT) for p in pairs]
        yield
        a_ab = [jnp.where(strict, amat[p][:ct, :pw], 0.0) for p in pairs]
        a_ak = [_bf(jnp.where(strict, amat[p][:ct, pw:], 0.0)) for p in pairs]
        a_rb = [_bf(jnp.where(incl, amat[p][ct:, :pw], 0.0)) for p in pairs]
        a_rk = [_bf(jnp.where(incl, amat[p][ct:, pw:], 0.0)) for p in pairs]
        x = [cat1([f32("at", p), _mm(a_ak[p], v_bd[p])]) for p in pairs]
        npow = [_bf(a_ab[p]) for p in pairs]
        yield
        for it in range(n_dbl):
            if it + 1 < n_dbl:
                res = [_mm(npow[p], cat1([_bd(_bf(x[p]), lo_t2), _bd(npow[p], lo_p)])) for p in pairs]
                npow = [_bf(res[p][:, 2 * pw:]) for p in pairs]
            else:
                res = [_mm(npow[p], _bd(_bf(x[p]), lo_t2)) for p in pairs]
            x = [x[p] + res[p][:, :2 * pw] for p in pairs]
            yield
        ry = [_mm(a_rb[p], _bd(_bf(x[p]), lo_t2)) for p in pairs]
        rp = [_bf(f32("rt", p) + ry[p][:, :pw]) for p in pairs]
        y0 = [ry[p][:, pw:] + _mm(a_rk[p], v_bd[p]) for p in pairs]
        yield
        m_mat = [_bf(jnp.where(diag, pc_scr[rd, gi:gi + 1, p * pw:(p + 1) * pw], 0.0)
                     + tn(x[p][:, :pw], b16("bp", p))) for p in pairs]
        yield
        n_mat = [tn(cat0([x[p][:, pw:], f32("v", p)]), cat0([b16("bp", p), b16("kp", p)])) for p in pairs]
        return rp, y0, m_mat, n_mat

    def lockstep(*gens):
        done = [None] * len(gens)
        live = list(range(len(gens)))
        while live:
            for i in list(live):
                try:
                    next(gens[i])
                except StopIteration as stop:
                    done[i] = stop.value
                    live.remove(i)
            yield
        return done

    def main_block(rd):
        terms = yield from lockstep(*[chunk_stages(rd, gi) for gi in range(cps)])
        state = [s_scr[:, p * pw:(p + 1) * pw] for p in pairs]
        for gi, (rp, y0, m_mat, n_mat) in enumerate(terms):
            for p in pairs:
                s_old = _bf(state[p])
                y_scr[gi * ct:(gi + 1) * ct, p * pw:(p + 1) * pw] = _mm(rp[p], _bd(s_old, lo_s), _NT) + y0[p]
                state[p] = _mm(s_old, _bd(m_mat[p], lo_s)) + n_mat[p]
            yield
        for p in pairs:
            s_scr[:, p * pw:(p + 1) * pw] = state[p]
        y = y_scr[...]
        mean = _seg_sum(y, ones_seg, passes=2) * (1.0 / HEAD_DIM)
        dv = y - mean
        yield
        var = _seg_sum(dv * dv, ones_seg, passes=1) * (1.0 / HEAD_DIM)
        yn = dv * lax.rsqrt(var + GN_EPS) * lnw_ref[...] + lnb_ref[...]
        out = (yn + f32_scr[rd, _TOK_F32.index("bonus")]) * f32_scr[rd, _TOK_F32.index("g")]
        o_ref[...] = out[:(cps - 1) * ct + rows].astype(BF16)

    def cast_block(parity):
        for src, dst, period in zip(cast_in, cast_out, cast_periods):
            if parity % period == 0:
                dst[...] = _bf(src[...])
            yield

    def run(*gens):
        for _ in lockstep(*gens):
            pass

    @pl.when(c == 0)
    def _():
        for p in pairs:
            s_scr[:, p * pw:(p + 1) * pw] = jnp.concatenate([s0_ref[2 * p], s0_ref[2 * p + 1]], axis=1)
        run(pre_block(zf_ref, sh_ref[...], 0))

    if nck == 1:
        assert n_cast == 0
        run(main_block(0))
    else:
        for parity in (0, 1):
            @pl.when(c % 2 == parity)
            def _(parity=parity):
                run(main_block(parity), pre_block(zn_ref, carry_scr[0:1, :], 1 - parity), cast_block(parity))

    @pl.when(c == nck - 1)
    def _():
        for h in range(N_HEADS):
            sfin_ref[h] = s_scr[:, h * HEAD_DIM:(h + 1) * HEAD_DIM]


def _rwkv(z_rwkv, shift0, s0, lp, *, nb, seq, name, cast_weights=()):
    rows = min(seq, CHUNK_T)
    cps = min(seq // rows, RWKV_CHUNKS_PER_STEP)
    blk_rows = cps * rows
    assert seq % blk_rows == 0 and z_rwkv.shape == (nb * seq, SHIFT_W)
    nck = seq // blk_rows
    tt = cps * CHUNK_T
    row1 = lambda n: pl.BlockSpec((1, n), lambda b, c: (0, 0))
    full = lambda r, n: pl.BlockSpec((r, n), lambda b, c: (0, 0))
    st_blk = pl.BlockSpec((None, N_HEADS, HEAD_DIM, HEAD_DIM), lambda b, c: (b, 0, 0, 0))
    lora_rows = DECAY_RANK + AAA_RANK
    blocks = (2 * _nbytes((blk_rows, SHIFT_W), F32) + 2 * _nbytes((HEAD_DIM, RWKV_WIDTH), F32)
              + _nbytes((lora_rows + GATE_RANK, RWKV_WIDTH), BF16) + _nbytes((16, SHIFT_W), F32)
              + _nbytes((blk_rows, RWKV_WIDTH), BF16))
    stage_shapes = [((2, len(_TOK_F32), tt, RWKV_WIDTH), F32), ((2, len(_TOK_BF16), tt, RWKV_WIDTH), BF16),
                    ((2, 8, RWKV_WIDTH), F32)]
    scratch = (_nbytes((HEAD_DIM, RWKV_WIDTH), F32) + _nbytes((8, SHIFT_W), F32)
               + _nbytes((tt, RWKV_WIDTH), F32) + sum(_nbytes(s, d) for s, d in stage_shapes))
    assert cps <= 8
    bf16_rows = 16
    cast_periods, cast_specs, cast_shapes = [], [], []
    for w in cast_weights:
        assert nb == 1 and nck % 2 == 0
        period = next(p for p in (1, 2) if w.shape[0] % (nck // p) == 0 and (w.shape[0] // (nck // p)) % bf16_rows == 0)
        slab = (w.shape[0] // (nck // period), w.shape[1])
        cast_periods.append(period)
        cast_specs.append(pl.BlockSpec(slab, lambda b, c, period=period: (c // period, 0)))
        cast_shapes.append(jax.ShapeDtypeStruct(w.shape, BF16))
        blocks += _nbytes(slab, F32) + _nbytes(slab, BF16)
    outs = pl.pallas_call(
        functools.partial(_rwkv_kernel, rows=rows, cps=cps, nck=nck, cast_periods=tuple(cast_periods)),
        out_shape=(jax.ShapeDtypeStruct((nb * seq, RWKV_WIDTH), BF16),
                   jax.ShapeDtypeStruct((nb, N_HEADS, HEAD_DIM, HEAD_DIM), F32), *cast_shapes),
        grid=(nb, nck),
        in_specs=[pl.BlockSpec((blk_rows, SHIFT_W), lambda b, c: (b * nck, 0)),
                  pl.BlockSpec((blk_rows, SHIFT_W), lambda b, c: (b * nck + jnp.minimum(c + 1, nck - 1), 0)),
                  pl.BlockSpec((None, 1, SHIFT_W), lambda b, c: (b, 0, 0)),
                  st_blk,
                  row1(SHIFT_W), row1(RWKV_WIDTH), full(lora_rows, RWKV_WIDTH), row1(RWKV_WIDTH),
                  full(GATE_RANK, RWKV_WIDTH), row1(RWKV_WIDTH),
                  row1(RWKV_WIDTH), row1(RWKV_WIDTH), row1(RWKV_WIDTH), row1(RWKV_WIDTH), *cast_specs],
        out_specs=(pl.BlockSpec((blk_rows, RWKV_WIDTH), lambda b, c: (b * nck + c, 0)), st_blk, *cast_specs),
        scratch_shapes=[pltpu.VMEM((HEAD_DIM, RWKV_WIDTH), F32), pltpu.VMEM((8, SHIFT_W), F32),
                        pltpu.VMEM((tt, RWKV_WIDTH), F32)] + [pltpu.VMEM(s, d) for s, d in stage_shapes],
        compiler_params=_params(2, _vmem_limit(blocks, scratch, 64 * _nbytes((tt, RWKV_WIDTH), F32))),
        name=name,
    )(z_rwkv, z_rwkv, shift0, s0, lp["mu"], lp["w0"], lp["w2cat"], lp["a0"], lp["g_w2"],
      lp["k_k"], lp["k_a"], lp["r_k"], lp["lnx_w"], lp["lnx_b"], *cast_weights)
    return outs[0], outs[1], list(outs[2:])


def _oproj_kernel(a_ref, r_ref, wa_ref, wr_ref, x_ref, g_ref, h_ref, n_ref):
    h = (x_ref[...] + jnp.dot(a_ref[...], wa_ref[...], preferred_element_type=F32)
         + jnp.dot(r_ref[...], wr_ref[...], preferred_element_type=F32))
    h_ref[...] = h
    n_ref[...] = _rms(h, g_ref[...]).astype(BF16)


def _oproj(attn, rwkv, w_o, x, g, *, tm, name):
    t = x.shape[0]
    assert t % tm == 0
    half = (ATTN_WIDTH, D_MODEL)
    blocks = (2 * _nbytes((tm, ATTN_WIDTH), BF16) + 2 * _nbytes(half, BF16) + 2 * _nbytes((tm, D_MODEL), F32)
              + _nbytes((tm, D_MODEL), BF16))
    return pl.pallas_call(
        _oproj_kernel,
        out_shape=(jax.ShapeDtypeStruct((t, D_MODEL), F32), jax.ShapeDtypeStruct((t, D_MODEL), BF16)),
        grid=(t // tm,),
        in_specs=[pl.BlockSpec((tm, ATTN_WIDTH), lambda i: (i, 0)),
                  pl.BlockSpec((tm, RWKV_WIDTH), lambda i: (i, 0)),
                  pl.BlockSpec(half, lambda i: (0, 0)),
                  pl.BlockSpec(half, lambda i: (1, 0)),
                  pl.BlockSpec((tm, D_MODEL), lambda i: (i, 0)),
                  pl.BlockSpec((1, D_MODEL), lambda i: (0, 0))],
        out_specs=(pl.BlockSpec((tm, D_MODEL), lambda i: (i, 0)), pl.BlockSpec((tm, D_MODEL), lambda i: (i, 0))),
        compiler_params=_params(1, _vmem_limit(blocks, 0, 2 * _nbytes((tm, D_MODEL), F32))),
        name=name,
    )(attn, rwkv, w_o, w_o, x, g)


def _ffn_kernel(n_ref, h_ref, wg_ref, wu_ref, wd_ref, g_ref, o_ref):
    j = pl.program_id(1)

    @pl.when(j == 0)
    def _():
        o_ref[...] = h_ref[...]

    n = n_ref[...]
    gate = jnp.dot(n, wg_ref[...], preferred_element_type=F32)
    up = jnp.dot(n, wu_ref[...], preferred_element_type=F32)
    act = (gate * _sigmoid(gate) * up).astype(BF16)
    o_ref[...] += jnp.dot(act, wd_ref[...], preferred_element_type=F32)

    @pl.when(j == pl.num_programs(1) - 1)
    def _():
        o_ref[...] = _rms(o_ref[...], g_ref[...])


def _ffn(n2, h, wg, wu, wd, g, *, tm, tf, name):
    t = h.shape[0]
    assert t % tm == 0 and D_FF % tf == 0
    blocks = (_nbytes((tm, D_MODEL), BF16) + 2 * _nbytes((tm, D_MODEL), F32) + 3 * _nbytes((D_MODEL, tf), BF16))
    temps = 3 * _nbytes((tm, tf), F32) + _nbytes((tm, tf), BF16)
    return pl.pallas_call(
        _ffn_kernel,
        out_shape=jax.ShapeDtypeStruct((t, D_MODEL), F32),
        grid=(t // tm, D_FF // tf),
        in_specs=[pl.BlockSpec((tm, D_MODEL), lambda i, j: (i, 0)),
                  pl.BlockSpec((tm, D_MODEL), lambda i, j: (i, 0)),
                  pl.BlockSpec((D_MODEL, tf), lambda i, j: (0, j)),
                  pl.BlockSpec((D_MODEL, tf), lambda i, j: (0, j)),
                  pl.BlockSpec((tf, D_MODEL), lambda i, j: (j, 0)),
                  pl.BlockSpec((1, D_MODEL), lambda i, j: (0, 0))],
        out_specs=pl.BlockSpec((tm, D_MODEL), lambda i, j: (i, 0)),
        compiler_params=_params(2, _vmem_limit(blocks, 0, temps)),
        name=name,
    )(n2, h, wg, wu, wd, g)


_LATE_WEIGHTS = ("w_o", "w_gate", "w_up", "w_down")


def _trunk(x, lp, wts, attn_fn, shift0, s0, *, nb, seq, tag):
    t = nb * seq
    tm = min(t, 1024)
    qkv_w = 3 * ATTN_WIDTH
    if "w_rwkv" not in wts:
        z_attn, wts["w_rwkv"] = _inproj(x, lp["ln1"], wts["w_in"], col0=0, n=qkv_w, tm=tm, tn=qkv_w // 4,
                                        name=f"inproj_attn_{tag}", side=(qkv_w, SHIFT_W))
    else:
        z_attn = _inproj(x, lp["ln1"], wts["w_in"], col0=0, n=qkv_w, tm=tm, tn=1024, name=f"inproj_attn_{tag}")
    z_rwkv = _inproj(x, lp["ln1"], wts["w_rwkv"], col0=0, n=SHIFT_W, tm=tm, tn=SHIFT_W // 2,
                     name=f"inproj_rwkv_{tag}")
    attn = attn_fn(z_attn)
    pending = [k for k in _LATE_WEIGHTS if wts[k].dtype != BF16]
    rwkv, s_fin, converted = _rwkv(z_rwkv, shift0, s0, lp, nb=nb, seq=seq, name=f"rwkv_{tag}",
                                   cast_weights=[wts[k] for k in pending])
    wts.update(zip(pending, converted))
    tm2 = min(t, 512)
    h, n2 = _oproj(attn, rwkv, wts["w_o"], x, lp["ln2"], tm=tm2, name=f"oproj_{tag}")
    y = _ffn(n2, h, wts["w_gate"], wts["w_up"], wts["w_down"], lp["final_norm"], tm=min(t, 1024), tf=512,
             name=f"ffn_{tag}")
    return y, z_attn, z_rwkv, s_fin


def kernel(x_prompt, x_sample, cache_attn_k, cache_attn_v, state_rwkv_wkv, state_rwkv_shift, ln1, w_in, rel_table,
           mu, w0, w_w2, a0, a_w2, g_w2, k_k, k_a, r_k, lnx_w, lnx_b, w_o, ln2, w_gate, w_up, w_down, final_norm):
    depth = w_in.shape[0]
    assert depth == 1, "final RMSNorm is fused into the (single) layer's FFN call"
    b, t, _ = x_prompt.shape
    bd, s, _ = x_sample.shape
    past = cache_attn_k.shape[2]
    n_keep = min(BAND_PAST, t)
    l = 0
    row = lambda p: p[l].reshape(1, -1)
    lp = dict(ln1=row(ln1), mu=row(mu), w0=row(w0), a0=row(a0),
              w2cat=jnp.concatenate([w_w2[l], a_w2[l]], axis=0).astype(BF16), g_w2=g_w2[l].astype(BF16),
              k_k=row(k_k), k_a=row(k_a), r_k=row(r_k), lnx_w=row(lnx_w), lnx_b=row(lnx_b), ln2=row(ln2),
              final_norm=final_norm.reshape(1, -1))
    wts = dict(w_in=w_in[l], w_o=w_o[l], w_gate=w_gate[l], w_up=w_up[l], w_down=w_down[l])
    table = rel_table[l]

    bias_p = _bias_tiles(table, mq=CHUNK, off=BAND_PAST + CHUNK, col_lo=CHUNK, col_hi=KEY_WIN, name="bias_prompt")
    bias_s = _bias_tiles(table, mq=s, off=past, col_lo=0, col_hi=past + s, name="bias_sample")

    assert b == 1
    yp, zap, zrp, wkv_p = _trunk(
        x_prompt.reshape(b * t, D_MODEL), lp, wts, lambda z: _attn_prompt(z, bias_p),
        jnp.zeros((b, 1, SHIFT_W), F32), jnp.zeros((b, N_HEADS, HEAD_DIM, HEAD_DIM), F32), nb=b, seq=t, tag="p")
    kc = cache_attn_k[l].reshape(bd, past, ATTN_WIDTH)
    vc = cache_attn_v[l].reshape(bd, past, ATTN_WIDTH)
    ys, zas, zrs, wkv_s = _trunk(
        x_sample.reshape(bd * s, D_MODEL), lp, wts, lambda z: _attn_sample(z, kc, vc, bias_s, nb=bd, s=s),
        state_rwkv_shift[l], state_rwkv_wkv[l], nb=bd, seq=s, tag="s")

    heads = lambda z, n, tt: z.reshape(n, tt, N_HEADS, HEAD_DIM)
    zap = zap.reshape(b, t, 3 * ATTN_WIDTH)
    zas = zas.reshape(bd, s, 3 * ATTN_WIDTH)
    k_p = heads(zap[:, t - n_keep:, ATTN_WIDTH:2 * ATTN_WIDTH], b, n_keep)
    v_p = heads(zap[:, t - n_keep:, 2 * ATTN_WIDTH:], b, n_keep)
    k_s = heads(zas[:, :, ATTN_WIDTH:2 * ATTN_WIDTH], bd, s)
    v_s = heads(zas[:, :, 2 * ATTN_WIDTH:], bd, s)
    sh_p = zrp.reshape(b, t, SHIFT_W)[:, t - 1:]
    sh_s = zrs.reshape(bd, s, SHIFT_W)[:, s - 1:]
    return (yp.reshape(b, t, D_MODEL), ys.reshape(bd, s, D_MODEL),
            k_p[None], v_p[None], wkv_p[None], sh_p[None],
            k_s[None], v_s[None], wkv_s[None], sh_s[None])
```

```python
import functools
import math

import jax
import jax.numpy as jnp
from jax import lax
from jax.experimental import pallas as pl
from jax.experimental.pallas import tpu as pltpu

F32 = jnp.float32
BF16 = jnp.bfloat16

D_MODEL = 2048
CHUNK = 64
N_PAST_CHUNKS = 8
BAND_PAST = N_PAST_CHUNKS * CHUNK
ATTN_WIDTH = 1024
HEAD_DIM = 64
N_HEADS = 16
REL_CLIP = 128
N_REL = 2 * REL_CLIP + 1
RWKV_WIDTH = 1024
DECAY_RANK = 64
AAA_RANK = 64
GATE_RANK = 128
SHIFT_W = 3 * RWKV_WIDTH + DECAY_RANK + AAA_RANK + GATE_RANK
D_FF = 5632
RMS_EPS = 1e-6
GN_EPS = 64e-5

V7X_LANES = 128
V7X_VMEM_BYTES = 64 * 1024 * 1024
V7X_SCOPED_VMEM_CAP_BYTES = 60000 * 1024

PAIR_W = 2 * HEAD_DIM
N_PAIRS = N_HEADS // 2
CHUNK_T = 64
KEY_WIN = (N_PAST_CHUNKS + 2) * CHUNK
NEG_BIG = -1e30
ATTN_QB = 512
ATTN_CHUNKS_PER_ITER = 2
ATTN_SOFTMAX_ROWS = 128
RWKV_CHUNKS_PER_STEP = 2

assert PAIR_W == V7X_LANES


def _vmem_limit(pipelined_bytes, scratch_bytes, temp_bytes):
    del pipelined_bytes, scratch_bytes, temp_bytes
    return V7X_SCOPED_VMEM_CAP_BYTES


def _nbytes(shape, dtype):
    return math.prod(shape) * jnp.dtype(dtype).itemsize


def _params(n_grid, vmem, flags=None):
    return pltpu.CompilerParams(dimension_semantics=("arbitrary",) * n_grid, vmem_limit_bytes=vmem, flags=flags)


def _rms(x, g):
    return x * lax.rsqrt(jnp.mean(x * x, axis=-1, keepdims=True) + RMS_EPS) * g


def _inproj_norm_kernel(x_ref, g_ref, w_ref, o_ref, nx_ref):
    @pl.when(pl.program_id(1) == 0)
    def _():
        nx_ref[...] = _rms(x_ref[...], g_ref[...]).astype(BF16)

    o_ref[...] = jnp.dot(nx_ref[...], w_ref[...], preferred_element_type=F32)


def _inproj_kernel(nx_ref, w_ref, o_ref):
    o_ref[...] = jnp.dot(nx_ref[...], w_ref[...], preferred_element_type=F32)


def _inproj(x, g, w, *, col0, n, tm, tn, name):
    t, d = x.shape
    assert t % tm == 0 and n % tn == 0 and col0 % V7X_LANES == 0 and tn % V7X_LANES == 0 and col0 + n <= w.shape[1]
    normalise = x.dtype != BF16
    assert normalise == (g is not None)
    w_spec = pl.BlockSpec((pl.Element(d), pl.Element(tn)),
                          lambda i, j: (0, (col0 // V7X_LANES + j * (tn // V7X_LANES)) * V7X_LANES))
    row_blk = pl.BlockSpec((tm, d), lambda i, j: (i, 0))
    z_shape = jax.ShapeDtypeStruct((t, n), F32)
    z_spec = pl.BlockSpec((tm, tn), lambda i, j: (i, j))
    blocks = _nbytes((tm, d), x.dtype) + _nbytes((d, tn), BF16) + _nbytes((tm, tn), F32)
    if not normalise:
        return pl.pallas_call(
            _inproj_kernel, out_shape=z_shape, grid=(t // tm, n // tn), in_specs=[row_blk, w_spec], out_specs=z_spec,
            compiler_params=_params(2, _vmem_limit(blocks, 0, _nbytes((tm, tn), F32))), name=name,
        )(x, w)
    return pl.pallas_call(
        _inproj_norm_kernel,
        out_shape=(z_shape, jax.ShapeDtypeStruct((t, d), BF16)),
        grid=(t // tm, n // tn),
        in_specs=[row_blk, pl.BlockSpec((1, d), lambda i, j: (0, 0)), w_spec],
        out_specs=(z_spec, row_blk),
        compiler_params=_params(2, _vmem_limit(blocks + _nbytes((tm, d), BF16), 0, 2 * _nbytes((tm, d), F32))),
        name=name,
    )(x, g, w)


BIAS_VAR_COL0 = KEY_WIN - 2 * V7X_LANES
BIAS_DIAG_W = 512


def _bias_kernel(tab_ref, o_ref, *, mq, off, col_lo, col_hi):
    nvar = KEY_WIN - BIAS_VAR_COL0
    w = BIAS_DIAG_W
    assert nvar + mq <= w
    lane = lax.broadcasted_iota(jnp.int32, (1, w), 1)
    idx = jnp.clip(off - (BIAS_VAR_COL0 - mq + lane), -REL_CLIP, REL_CLIP) + REL_CLIP
    idx_min = max(0, min(REL_CLIP, off - (BIAS_VAR_COL0 - mq + w - 1)) + REL_CLIP)
    heads = range(N_HEADS)

    def body(j, profs):
        hit = idx == j
        return tuple(jnp.where(hit, tab_ref[h, j], prof) for h, prof in zip(heads, profs))

    init = tuple(jnp.full((1, w), tab_ref[h, N_REL - 1], F32) for h in heads)
    profs = lax.fori_loop(idx_min, N_REL - 1, body, init)

    c = lax.broadcasted_iota(jnp.int32, (mq, nvar), 1) + BIAS_VAR_COL0
    valid_var = (c >= col_lo) & (c < col_hi)
    c_far = lax.broadcasted_iota(jnp.int32, (mq, BIAS_VAR_COL0), 1)
    valid_far = (c_far >= col_lo) & (c_far < col_hi)
    for h in heads:
        tile = pltpu.roll(jnp.broadcast_to(profs[h], (mq, w)), w - mq, axis=1, stride=1, stride_axis=0)
        rows = slice((h % 2) * mq, (h % 2 + 1) * mq)
        far = jnp.full((mq, BIAS_VAR_COL0), tab_ref[h, N_REL - 1], F32)
        o_ref[h // 2, rows, 0:BIAS_VAR_COL0] = jnp.where(valid_far, far, NEG_BIG)
        o_ref[h // 2, rows, BIAS_VAR_COL0:KEY_WIN] = jnp.where(valid_var, tile[:, 0:nvar], NEG_BIG)


def _bias_tiles(table, *, mq, off, col_lo, col_hi, name):
    assert off - (BIAS_VAR_COL0 - 1) >= REL_CLIP
    return pl.pallas_call(
        functools.partial(_bias_kernel, mq=mq, off=off, col_lo=col_lo, col_hi=col_hi),
        out_shape=jax.ShapeDtypeStruct((N_PAIRS, 2 * mq, KEY_WIN), F32),
        in_specs=[pl.BlockSpec(memory_space=pltpu.SMEM)],
        out_specs=pl.BlockSpec(memory_space=pltpu.VMEM),
        name=name,
    )(table)


def _lane_lo(rows):
    return lax.broadcasted_iota(jnp.int32, (rows, PAIR_W), 1) < HEAD_DIM


def _attend(q_pair, k_pair, v_pair, bias_ref, o_put, s_scr, p_scr, *, mq, n_win, min_col):
    lo = _lane_lo(mq)
    units = [(w, p) for w in range(n_win) for p in range(N_PAIRS)]
    for u, (w, p) in enumerate(units):
        qs = q_pair(w, p) * (HEAD_DIM ** -0.5)
        q2 = jnp.concatenate([jnp.where(lo, qs, 0.0), jnp.where(lo, 0.0, qs)], axis=0).astype(BF16)
        s = lax.dot_general(q2, k_pair(w, p), (((1,), (1,)), ((), ())), preferred_element_type=F32) + bias_ref[p]
        first = min_col(w)
        if first is not None:
            col = lax.broadcasted_iota(jnp.int32, (1, KEY_WIN), 1)
            s = jnp.where(col >= first, s, NEG_BIG)
        s_scr[u] = s
    inv_l = []
    rb = min(2 * mq, ATTN_SOFTMAX_ROWS)
    for u in range(len(units)):
        parts = []
        for r0 in range(0, 2 * mq, rb):
            s = s_scr[u, r0:r0 + rb, :]
            e = jnp.exp(s - jnp.max(s, axis=-1, keepdims=True))
            parts.append(1.0 / jnp.sum(e, axis=-1, keepdims=True))
            p_scr[u, r0:r0 + rb, :] = e.astype(BF16)
        inv_l.append(jnp.concatenate(parts, axis=0))
    for u, (w, p) in enumerate(units):
        pv = jnp.dot(p_scr[u], v_pair(w, p), preferred_element_type=F32) * inv_l[u]
        o_put(w, p, jnp.where(lo, pv[:mq], pv[mq:]))


def _attn_prompt_kernel(q_ref, kc_ref, vc_ref, bias_ref, o_ref, kbuf, vbuf, s_scr, p_scr):
    i = pl.program_id(0)
    prev0, cur0, end = CHUNK, CHUNK + ATTN_QB, CHUNK + 2 * ATTN_QB

    @pl.when(i == 0)
    def _():
        zeros = jnp.zeros((cur0, ATTN_WIDTH), BF16)
        kbuf[0:cur0, :] = zeros
        vbuf[0:cur0, :] = zeros

    kbuf[cur0:end, :] = kc_ref[...].astype(BF16)
    vbuf[cur0:end, :] = vc_ref[...].astype(BF16)
    chunks_per_step = ATTN_QB // CHUNK
    lead_chunks = N_PAST_CHUNKS + 1
    n_win = ATTN_CHUNKS_PER_ITER

    def iter_body(j, carry, *, masked):
        sl = lambda p: slice(p * PAIR_W, (p + 1) * PAIR_W)
        r0 = lambda w: pl.multiple_of((j * n_win + w) * CHUNK, CHUNK)
        min_col = lambda w: (lead_chunks - (i * chunks_per_step + j * n_win + w)) * CHUNK if masked else None

        def o_put(w, p, o):
            o_ref[pl.ds(r0(w), CHUNK), sl(p)] = o.astype(BF16)

        _attend(lambda w, p: q_ref[pl.ds(r0(w), CHUNK), sl(p)], lambda w, p: kbuf[pl.ds(r0(w), KEY_WIN), sl(p)],
                lambda w, p: vbuf[pl.ds(r0(w), KEY_WIN), sl(p)], bias_ref, o_put, s_scr, p_scr, mq=CHUNK,
                n_win=n_win, min_col=min_col)
        return carry

    masked_steps = -(-lead_chunks // chunks_per_step)

    @pl.when(i < masked_steps)
    def _():
        lax.fori_loop(0, chunks_per_step // n_win, functools.partial(iter_body, masked=True), 0)

    @pl.when(i >= masked_steps)
    def _():
        lax.fori_loop(0, chunks_per_step // n_win, functools.partial(iter_body, masked=False), 0)

    kbuf[prev0:cur0, :] = kbuf[cur0:end, :]
    vbuf[prev0:cur0, :] = vbuf[cur0:end, :]


def _attn_prompt(z_attn, bias):
    t = z_attn.shape[0]
    assert t % ATTN_QB == 0 and (ATTN_QB // CHUNK) % ATTN_CHUNKS_PER_ITER == 0 and ATTN_QB >= BAND_PAST
    blk = (ATTN_QB, ATTN_WIDTH)
    buf_rows = CHUNK + 2 * ATTN_QB
    s_shape = (ATTN_CHUNKS_PER_ITER * N_PAIRS, 2 * CHUNK, KEY_WIN)
    blocks = 3 * _nbytes(blk, F32) + _nbytes(bias.shape, F32) + _nbytes(blk, BF16)
    scratch = 2 * _nbytes((buf_rows, ATTN_WIDTH), BF16) + _nbytes(s_shape, F32) + _nbytes(s_shape, BF16)
    return pl.pallas_call(
        _attn_prompt_kernel,
        out_shape=jax.ShapeDtypeStruct((t, ATTN_WIDTH), BF16),
        grid=(t // ATTN_QB,),
        in_specs=[pl.BlockSpec(blk, lambda i: (i, 0)),
                  pl.BlockSpec(blk, lambda i: (i, 1)),
                  pl.BlockSpec(blk, lambda i: (i, 2)),
                  pl.BlockSpec(bias.shape, lambda i: (0, 0, 0))],
        out_specs=pl.BlockSpec(blk, lambda i: (i, 0)),
        scratch_shapes=[pltpu.VMEM((buf_rows, ATTN_WIDTH), BF16), pltpu.VMEM((buf_rows, ATTN_WIDTH), BF16),
                        pltpu.VMEM(s_shape, F32), pltpu.VMEM(s_shape, BF16)],
        compiler_params=_params(1, _vmem_limit(blocks, scratch, 4 * _nbytes((2 * CHUNK, KEY_WIN), F32))),
        name="attn_prompt",
    )(z_attn, z_attn, z_attn, bias)


def _attn_sample_kernel(q_ref, kn_ref, vn_ref, kc_ref, vc_ref, bias_ref, o_ref, kbuf, vbuf, s_scr, p_scr, *,
                        s, past):
    tail = jnp.zeros((KEY_WIN - past - s, ATTN_WIDTH), BF16)
    kbuf[0:past, :] = kc_ref[...].astype(BF16)
    vbuf[0:past, :] = vc_ref[...].astype(BF16)
    kbuf[past:past + s, :] = kn_ref[...].astype(BF16)
    vbuf[past:past + s, :] = vn_ref[...].astype(BF16)
    kbuf[past + s:KEY_WIN, :] = tail
    vbuf[past + s:KEY_WIN, :] = tail
    sl = lambda p: slice(p * PAIR_W, (p + 1) * PAIR_W)

    def o_put(w, p, o):
        o_ref[:, sl(p)] = o.astype(BF16)

    _attend(lambda w, p: q_ref[:, sl(p)], lambda w, p: kbuf[:, sl(p)], lambda w, p: vbuf[:, sl(p)], bias_ref,
            o_put, s_scr, p_scr, mq=s, n_win=1, min_col=lambda w: None)


def _attn_sample(z_attn, k_cache, v_cache, bias, *, nb, s):
    past = k_cache.shape[1]
    assert past + s <= KEY_WIN and z_attn.shape[0] == nb * s
    blk = (s, ATTN_WIDTH)
    cblk = (None, past, ATTN_WIDTH)
    s_shape = (N_PAIRS, 2 * s, KEY_WIN)
    blocks = (3 * _nbytes(blk, F32) + 2 * _nbytes((past, ATTN_WIDTH), F32) + _nbytes(bias.shape, F32)
              + _nbytes(blk, BF16))
    scratch = 2 * _nbytes((KEY_WIN, ATTN_WIDTH), BF16) + _nbytes(s_shape, F32) + _nbytes(s_shape, BF16)
    return pl.pallas_call(
        functools.partial(_attn_sample_kernel, s=s, past=past),
        out_shape=jax.ShapeDtypeStruct((nb * s, ATTN_WIDTH), BF16),
        grid=(nb,),
        in_specs=[pl.BlockSpec(blk, lambda b: (b, 0)),
                  pl.BlockSpec(blk, lambda b: (b, 1)),
                  pl.BlockSpec(blk, lambda b: (b, 2)),
                  pl.BlockSpec(cblk, lambda b: (b, 0, 0)),
                  pl.BlockSpec(cblk, lambda b: (b, 0, 0)),
                  pl.BlockSpec(bias.shape, lambda b: (0, 0, 0))],
        out_specs=pl.BlockSpec(blk, lambda b: (b, 0)),
        scratch_shapes=[pltpu.VMEM((KEY_WIN, ATTN_WIDTH), BF16), pltpu.VMEM((KEY_WIN, ATTN_WIDTH), BF16),
                        pltpu.VMEM(s_shape, F32), pltpu.VMEM(s_shape, BF16)],
        compiler_params=_params(1, _vmem_limit(blocks, scratch, 4 * _nbytes((2 * s, KEY_WIN), F32))),
        name="attn_sample",
    )(z_attn, z_attn, z_attn, k_cache, v_cache, bias)


def _bf(x):
    return x.astype(BF16)


def _split2(x):
    hi = _bf(x)
    return hi, _bf(x - hi.astype(F32))


_NN = (((1,), (0,)), ((), ()))
_NT = (((1,), (1,)), ((), ()))


def _mm(x, y, dims=_NN):
    return lax.dot_general(x, y, dims, preferred_element_type=F32)


def _bd(y, lo):
    zero = jnp.zeros_like(y)
    return jnp.concatenate([jnp.where(lo, y, zero), jnp.where(lo, zero, y)], axis=0)


def _seg_sum(x, ones_seg, *, passes):
    r = x.shape[0]
    half = x.shape[1] // (2 * PAIR_W)
    slab = lambda p: x[:, p * PAIR_W:(p + 1) * PAIR_W]
    xx = jnp.concatenate([jnp.concatenate([slab(p) for p in range(half)], axis=0),
                          jnp.concatenate([slab(half + p) for p in range(half)], axis=0)], axis=1)
    hi, lo = _split2(xx)
    s = _mm(hi, ones_seg)
    if passes == 2:
        s = s + _mm(lo, ones_seg)
    return jnp.concatenate([s[p * r:(p + 1) * r, 0:PAIR_W] for p in range(half)]
                           + [s[p * r:(p + 1) * r, PAIR_W:2 * PAIR_W] for p in range(half)], axis=1)


def _sigmoid(x):
    return 1.0 / (1.0 + jnp.exp(-x))


def _sigmoid_tanh(x):
    return 0.5 + 0.5 * jnp.tanh(0.5 * x)


_TOK_F32 = ("at", "rt", "v", "bonus", "g")
_TOK_BF16 = ("at", "bt", "kt", "rt", "v", "bp", "kp")


_RWKV_N_IN = 14
_RWKV_N_OUT = 2
_RWKV_N_SCRATCH = 6


def _rwkv_kernel(*refs, rows, cps, nck, cast_periods):
    n_cast = len(cast_periods)
    (zf_ref, zn_ref, sh_ref, s0_ref, mu_ref, w0_ref, w2cat_ref, a0_ref, gw2_ref, kk_ref, ka_ref, rk_ref, lnw_ref,
     lnb_ref) = refs[:_RWKV_N_IN]
    cast_in = refs[_RWKV_N_IN:_RWKV_N_IN + n_cast]
    o_ref, sfin_ref = refs[_RWKV_N_IN + n_cast:_RWKV_N_IN + n_cast + _RWKV_N_OUT]
    cast_out = refs[_RWKV_N_IN + n_cast + _RWKV_N_OUT:_RWKV_N_IN + 2 * n_cast + _RWKV_N_OUT]
    s_scr, carry_scr, y_scr, f32_scr, bf_scr, pc_scr = refs[-_RWKV_N_SCRATCH:]
    ct = CHUNK_T
    pw = PAIR_W
    tt = cps * ct
    assert rows == ct or cps == 1
    c = pl.program_id(1)

    seg = 2 * pw
    ones_seg = _bf(lax.broadcasted_iota(jnp.int32, (seg, seg), 0) // HEAD_DIM
                   == lax.broadcasted_iota(jnp.int32, (seg, seg), 1) // HEAD_DIM)
    row = lax.broadcasted_iota(jnp.int32, (ct, 1), 0)
    row_all = lax.broadcasted_iota(jnp.int32, (tt, 1), 0)
    tri = _bf(lax.broadcasted_iota(jnp.int32, (ct, ct), 1) <= lax.broadcasted_iota(jnp.int32, (ct, ct), 0))
    t_lane = lax.broadcasted_iota(jnp.int32, (ct, 2 * ct), 1) % ct
    strict = t_lane < row
    incl = t_lane <= row
    lo_t = _lane_lo(ct)
    lo_p = lax.broadcasted_iota(jnp.int32, (ct, 2 * ct), 1) < ct
    lo_t2 = lax.broadcasted_iota(jnp.int32, (ct, 2 * pw), 1) % pw < HEAD_DIM
    diag = (lax.broadcasted_iota(jnp.int32, (HEAD_DIM, pw), 1) % HEAD_DIM
            == lax.broadcasted_iota(jnp.int32, (HEAD_DIM, pw), 0))
    lo_s = _lane_lo(HEAD_DIM)
    n_dbl = int(math.log2(ct))
    assert 1 << n_dbl == ct and 2 * ct == pw
    pairs = range(N_PAIRS)
    cat0 = lambda xs: jnp.concatenate(xs, axis=0)
    cat1 = lambda xs: jnp.concatenate(xs, axis=1)

    def tn(xv, yv):
        zz = _mm(_bf(xv.T), yv)
        return jnp.where(lo_s, zz[:HEAD_DIM], zz[HEAD_DIM:])

    def chunk_pre(zc):
        r = zc[:, 0:RWKV_WIDTH]
        k = zc[:, RWKV_WIDTH:2 * RWKV_WIDTH]
        v = zc[:, 2 * RWKV_WIDTH:3 * RWKV_WIDTH]
        zwa = zc[:, 3 * RWKV_WIDTH:3 * RWKV_WIDTH + DECAY_RANK + AAA_RANK]
        zg = zc[:, 3 * RWKV_WIDTH + DECAY_RANK + AAA_RANK:SHIFT_W]
        lora_in = cat0([jnp.where(lo_t, jnp.tanh(zwa), 0.0), jnp.where(lo_t, 0.0, zwa)])
        lora = _mm(_bf(lora_in), w2cat_ref[...])
        yield
        lw = -math.exp(-0.5) * _sigmoid_tanh(w0_ref[...] + lora[:ct])
        if rows < ct:
            live = row < rows
            lw = jnp.where(live, lw, 0.0)
        l_hi, l_lo = _split2(lw)
        cl = _mm(tri, l_hi) + _mm(tri, l_lo)
        yield
        a = _sigmoid_tanh(a0_ref[...] + lora[ct:])
        g = _mm(_bf(_sigmoid_tanh(zg)), gw2_ref[...])
        yield
        kkr = k * kk_ref[...]
        kn = kkr * lax.rsqrt(jnp.maximum(_seg_sum(kkr * kkr, ones_seg, passes=1), 1e-24))
        yield
        k2 = k * (1.0 + (a - 1.0) * ka_ref[...])
        bonus = _seg_sum(r * k2 * rk_ref[...], ones_seg, passes=1) * v
        yield
        e_in = jnp.exp(cl)
        rt = r * e_in
        pc = e_in[ct - 1:ct, :]
        yield
        at = -kn * jnp.exp(cl - lw)
        yield
        e_inv = jnp.exp(-cl)
        bt = kn * a * e_inv
        kt = k2 * e_inv
        if rows < ct:
            bt = jnp.where(live, bt, 0.0)
            kt = jnp.where(live, kt, 0.0)
        yield
        return dict(at=at, bt=bt, kt=kt, rt=rt, v=v, bp=bt * pc, kp=kt * pc, pc=pc, bonus=bonus, g=g)

    def pre_block(z_ref, carry, wr):
        z = z_ref[...]
        if rows < ct:
            z = jnp.concatenate([z, jnp.zeros((ct - rows, SHIFT_W), F32)], axis=0)
        zprev = jnp.where(row_all == 0, carry, pltpu.roll(z, 1, axis=0))
        last = (cps - 1) * ct + rows - 1
        carry_scr[0:1, :] = z[last:last + 1, :]
        zs = z + (zprev - z) * mu_ref[...]
        yield
        for gi in range(cps):
            t = yield from chunk_pre(zs[gi * ct:(gi + 1) * ct])
            rs = slice(gi * ct, (gi + 1) * ct)
            for i, name in enumerate(_TOK_F32):
                f32_scr[wr, i, rs, :] = t[name]
            yield
            for i, name in enumerate(_TOK_BF16):
                bf_scr[wr, i, rs, :] = _bf(t[name])
            pc_scr[wr, gi:gi + 1, :] = t["pc"]
            yield

    def chunk_stages(rd, gi):
        rs = slice(gi * ct, (gi + 1) * ct)
        f32 = lambda name, p: f32_scr[rd, _TOK_F32.index(name), rs, p * pw:(p + 1) * pw]
        b16 = lambda name, p: bf_scr[rd, _TOK_BF16.index(name), rs, p * pw:(p + 1) * pw]
        v_bd = [_bd(b16("v", p), lo_t) for p in pairs]
        amat = [_mm(cat0([b16("at", p), b16("rt", p)]), cat0([_bd(b16("bt", p), lo_t), _bd(b16("kt", p), lo_t)]),
                    _NT) for p in pairs]
        yield
        a_ab = [jnp.where(strict, amat[p][:ct, :pw], 0.0) for p in pairs]
        a_ak = [_bf(jnp.where(strict, amat[p][:ct, pw:], 0.0)) for p in pairs]
        a_rb = [_bf(jnp.where(incl, amat[p][ct:, :pw], 0.0)) for p in pairs]
        a_rk = [_bf(jnp.where(incl, amat[p][ct:, pw:], 0.0)) for p in pairs]
        x = [cat1([f32("at", p), _mm(a_ak[p], v_bd[p])]) for p in pairs]
        npow = [_bf(a_ab[p]) for p in pairs]
        yield
        for it in range(n_dbl):
            if it + 1 < n_dbl:
                res = [_mm(npow[p], cat1([_bd(_bf(x[p]), lo_t2), _bd(npow[p], lo_p)])) for p in pairs]
                npow = [_bf(res[p][:, 2 * pw:]) for p in pairs]
            else:
                res = [_mm(npow[p], _bd(_bf(x[p]), lo_t2)) for p in pairs]
            x = [x[p] + res[p][:, :2 * pw] for p in pairs]
            yield
        ry = [_mm(a_rb[p], _bd(_bf(x[p]), lo_t2)) for p in pairs]
        rp = [_bf(f32("rt", p) + ry[p][:, :pw]) for p in pairs]
        y0 = [ry[p][:, pw:] + _mm(a_rk[p], v_bd[p]) for p in pairs]
        yield
        m_mat = [_bf(jnp.where(diag, pc_scr[rd, gi:gi + 1, p * pw:(p + 1) * pw], 0.0)
                     + tn(x[p][:, :pw], b16("bp", p))) for p in pairs]
        yield
        n_mat = [tn(cat0([x[p][:, pw:], f32("v", p)]), cat0([b16("bp", p), b16("kp", p)])) for p in pairs]
        return rp, y0, m_mat, n_mat

    def lockstep(*gens):
        done = [None] * len(gens)
        live = list(range(len(gens)))
        while live:
            for i in list(live):
                try:
                    next(gens[i])
                except StopIteration as stop:
                    done[i] = stop.value
                    live.remove(i)
            yield
        return done

    def main_block(rd):
        terms = yield from lockstep(*[chunk_stages(rd, gi) for gi in range(cps)])
        state = [s_scr[:, p * pw:(p + 1) * pw] for p in pairs]
        for gi, (rp, y0, m_mat, n_mat) in enumerate(terms):
            for p in pairs:
                s_old = _bf(state[p])
                y_scr[gi * ct:(gi + 1) * ct, p * pw:(p + 1) * pw] = _mm(rp[p], _bd(s_old, lo_s), _NT) + y0[p]
                state[p] = _mm(s_old, _bd(m_mat[p], lo_s)) + n_mat[p]
            yield
        for p in pairs:
            s_scr[:, p * pw:(p + 1) * pw] = state[p]
        y = y_scr[...]
        mean = _seg_sum(y, ones_seg, passes=2) * (1.0 / HEAD_DIM)
        dv = y - mean
        yield
        var = _seg_sum(dv * dv, ones_seg, passes=1) * (1.0 / HEAD_DIM)
        yn = dv * lax.rsqrt(var + GN_EPS) * lnw_ref[...] + lnb_ref[...]
        out = (yn + f32_scr[rd, _TOK_F32.index("bonus")]) * f32_scr[rd, _TOK_F32.index("g")]
        o_ref[...] = out[:(cps - 1) * ct + rows].astype(BF16)

    def cast_block(parity):
        for src, dst, period in zip(cast_in, cast_out, cast_periods):
            if parity % period == 0:
                dst[...] = _bf(src[...])
            yield

    def run(*gens):
        for _ in lockstep(*gens):
            pass

    @pl.when(c == 0)
    def _():
        for p in pairs:
            s_scr[:, p * pw:(p + 1) * pw] = jnp.concatenate([s0_ref[2 * p], s0_ref[2 * p + 1]], axis=1)
        run(pre_block(zf_ref, sh_ref[...], 0))

    if nck == 1:
        assert n_cast == 0
        run(main_block(0))
    else:
        for parity in (0, 1):
            @pl.when(c % 2 == parity)
            def _(parity=parity):
                run(main_block(parity), pre_block(zn_ref, carry_scr[0:1, :], 1 - parity), cast_block(parity))

    @pl.when(c == nck - 1)
    def _():
        for h in range(N_HEADS):
            sfin_ref[h] = s_scr[:, h * HEAD_DIM:(h + 1) * HEAD_DIM]


def _rwkv(z_rwkv, shift0, s0, lp, *, nb, seq, name, cast_weights=()):
    rows = min(seq, CHUNK_T)
    cps = min(seq // rows, RWKV_CHUNKS_PER_STEP)
    blk_rows = cps * rows
    assert seq % blk_rows == 0 and z_rwkv.shape == (nb * seq, SHIFT_W)
    nck = seq // blk_rows
    tt = cps * CHUNK_T
    row1 = lambda n: pl.BlockSpec((1, n), lambda b, c: (0, 0))
    full = lambda r, n: pl.BlockSpec((r, n), lambda b, c: (0, 0))
    st_blk = pl.BlockSpec((None, N_HEADS, HEAD_DIM, HEAD_DIM), lambda b, c: (b, 0, 0, 0))
    lora_rows = DECAY_RANK + AAA_RANK
    blocks = (2 * _nbytes((blk_rows, SHIFT_W), F32) + 2 * _nbytes((HEAD_DIM, RWKV_WIDTH), F32)
              + _nbytes((lora_rows + GATE_RANK, RWKV_WIDTH), BF16) + _nbytes((16, SHIFT_W), F32)
              + _nbytes((blk_rows, RWKV_WIDTH), BF16))
    stage_shapes = [((2, len(_TOK_F32), tt, RWKV_WIDTH), F32), ((2, len(_TOK_BF16), tt, RWKV_WIDTH), BF16),
                    ((2, 8, RWKV_WIDTH), F32)]
    scratch = (_nbytes((HEAD_DIM, RWKV_WIDTH), F32) + _nbytes((8, SHIFT_W), F32)
               + _nbytes((tt, RWKV_WIDTH), F32) + sum(_nbytes(s, d) for s, d in stage_shapes))
    assert cps <= 8
    bf16_rows = 16
    cast_periods, cast_specs, cast_shapes = [], [], []
    for w in cast_weights:
        assert nb == 1 and nck % 2 == 0
        period = next(p for p in (1, 2) if w.shape[0] % (nck // p) == 0 and (w.shape[0] // (nck // p)) % bf16_rows == 0)
        slab = (w.shape[0] // (nck // period), w.shape[1])
        cast_periods.append(period)
        cast_specs.append(pl.BlockSpec(slab, lambda b, c, period=period: (c // period, 0)))
        cast_shapes.append(jax.ShapeDtypeStruct(w.shape, BF16))
        blocks += _nbytes(slab, F32) + _nbytes(slab, BF16)
    outs = pl.pallas_call(
        functools.partial(_rwkv_kernel, rows=rows, cps=cps, nck=nck, cast_periods=tuple(cast_periods)),
        out_shape=(jax.ShapeDtypeStruct((nb * seq, RWKV_WIDTH), BF16),
                   jax.ShapeDtypeStruct((nb, N_HEADS, HEAD_DIM, HEAD_DIM), F32), *cast_shapes),
        grid=(nb, nck),
        in_specs=[pl.BlockSpec((blk_rows, SHIFT_W), lambda b, c: (b * nck, 0)),
                  pl.BlockSpec((blk_rows, SHIFT_W), lambda b, c: (b * nck + jnp.minimum(c + 1, nck - 1), 0)),
                  pl.BlockSpec((None, 1, SHIFT_W), lambda b, c: (b, 0, 0)),
                  st_blk,
                  row1(SHIFT_W), row1(RWKV_WIDTH), full(lora_rows, RWKV_WIDTH), row1(RWKV_WIDTH),
                  full(GATE_RANK, RWKV_WIDTH), row1(RWKV_WIDTH),
                  row1(RWKV_WIDTH), row1(RWKV_WIDTH), row1(RWKV_WIDTH), row1(RWKV_WIDTH), *cast_specs],
        out_specs=(pl.BlockSpec((blk_rows, RWKV_WIDTH), lambda b, c: (b * nck + c, 0)), st_blk, *cast_specs),
        scratch_shapes=[pltpu.VMEM((HEAD_DIM, RWKV_WIDTH), F32), pltpu.VMEM((8, SHIFT_W), F32),
                        pltpu.VMEM((tt, RWKV_WIDTH), F32)] + [pltpu.VMEM(s, d) for s, d in stage_shapes],
        compiler_params=_params(2, _vmem_limit(blocks, scratch, 64 * _nbytes((tt, RWKV_WIDTH), F32))),
        name=name,
    )(z_rwkv, z_rwkv, shift0, s0, lp["mu"], lp["w0"], lp["w2cat"], lp["a0"], lp["g_w2"],
      lp["k_k"], lp["k_a"], lp["r_k"], lp["lnx_w"], lp["lnx_b"], *cast_weights)
    return outs[0], outs[1], list(outs[2:])


def _oproj_kernel(a_ref, r_ref, wa_ref, wr_ref, x_ref, g_ref, h_ref, n_ref):
    h = (x_ref[...] + jnp.dot(a_ref[...], wa_ref[...], preferred_element_type=F32)
         + jnp.dot(r_ref[...], wr_ref[...], preferred_element_type=F32))
    h_ref[...] = h
    n_ref[...] = _rms(h, g_ref[...]).astype(BF16)


def _oproj(attn, rwkv, w_o, x, g, *, tm, name):
    t = x.shape[0]
    assert t % tm == 0
    half = (ATTN_WIDTH, D_MODEL)
    blocks = (2 * _nbytes((tm, ATTN_WIDTH), BF16) + 2 * _nbytes(half, BF16) + 2 * _nbytes((tm, D_MODEL), F32)
              + _nbytes((tm, D_MODEL), BF16))
    return pl.pallas_call(
        _oproj_kernel,
        out_shape=(jax.ShapeDtypeStruct((t, D_MODEL), F32), jax.ShapeDtypeStruct((t, D_MODEL), BF16)),
        grid=(t // tm,),
        in_specs=[pl.BlockSpec((tm, ATTN_WIDTH), lambda i: (i, 0)),
                  pl.BlockSpec((tm, RWKV_WIDTH), lambda i: (i, 0)),
                  pl.BlockSpec(half, lambda i: (0, 0)),
                  pl.BlockSpec(half, lambda i: (1, 0)),
                  pl.BlockSpec((tm, D_MODEL), lambda i: (i, 0)),
                  pl.BlockSpec((1, D_MODEL), lambda i: (0, 0))],
        out_specs=(pl.BlockSpec((tm, D_MODEL), lambda i: (i, 0)), pl.BlockSpec((tm, D_MODEL), lambda i: (i, 0))),
        compiler_params=_params(1, _vmem_limit(blocks, 0, 2 * _nbytes((tm, D_MODEL), F32))),
        name=name,
    )(attn, rwkv, w_o, w_o, x, g)


def _ffn_kernel(n_ref, h_ref, wg_ref, wu_ref, wd_ref, g_ref, o_ref):
    j = pl.program_id(1)

    @pl.when(j == 0)
    def _():
        o_ref[...] = h_ref[...]

    n = n_ref[...]
    gate = jnp.dot(n, wg_ref[...], preferred_element_type=F32)
    up = jnp.dot(n, wu_ref[...], preferred_element_type=F32)
    act = (gate * _sigmoid(gate) * up).astype(BF16)
    o_ref[...] += jnp.dot(act, wd_ref[...], preferred_element_type=F32)

    @pl.when(j == pl.num_programs(1) - 1)
    def _():
        o_ref[...] = _rms(o_ref[...], g_ref[...])


def _ffn(n2, h, wg, wu, wd, g, *, tm, tf, name):
    t = h.shape[0]
    assert t % tm == 0 and D_FF % tf == 0
    blocks = (_nbytes((tm, D_MODEL), BF16) + 2 * _nbytes((tm, D_MODEL), F32) + 3 * _nbytes((D_MODEL, tf), BF16))
    temps = 3 * _nbytes((tm, tf), F32) + _nbytes((tm, tf), BF16)
    return pl.pallas_call(
        _ffn_kernel,
        out_shape=jax.ShapeDtypeStruct((t, D_MODEL), F32),
        grid=(t // tm, D_FF // tf),
        in_specs=[pl.BlockSpec((tm, D_MODEL), lambda i, j: (i, 0)),
                  pl.BlockSpec((tm, D_MODEL), lambda i, j: (i, 0)),
                  pl.BlockSpec((D_MODEL, tf), lambda i, j: (0, j)),
                  pl.BlockSpec((D_MODEL, tf), lambda i, j: (0, j)),
                  pl.BlockSpec((tf, D_MODEL), lambda i, j: (j, 0)),
                  pl.BlockSpec((1, D_MODEL), lambda i, j: (0, 0))],
        out_specs=pl.BlockSpec((tm, D_MODEL), lambda i, j: (i, 0)),
        compiler_params=_params(2, _vmem_limit(blocks, 0, temps)),
        name=name,
    )(n2, h, wg, wu, wd, g)


_LATE_WEIGHTS = ("w_o", "w_gate", "w_up", "w_down")


def _trunk(x, lp, wts, attn_fn, shift0, s0, *, nb, seq, tag):
    t = nb * seq
    tm = min(t, 1024)
    z_attn, nx = _inproj(x, lp["ln1"], wts["w_in"], col0=0, n=3 * ATTN_WIDTH, tm=tm, tn=1024,
                         name=f"inproj_attn_{tag}")
    z_rwkv = _inproj(nx, None, wts["w_in"], col0=3 * ATTN_WIDTH, n=SHIFT_W, tm=tm, tn=SHIFT_W // 2,
                     name=f"inproj_rwkv_{tag}")
    attn = attn_fn(z_attn)
    pending = [k for k in _LATE_WEIGHTS if wts[k].dtype != BF16]
    rwkv, s_fin, converted = _rwkv(z_rwkv, shift0, s0, lp, nb=nb, seq=seq, name=f"rwkv_{tag}",
                                   cast_weights=[wts[k] for k in pending])
    wts.update(zip(pending, converted))
    tm2 = min(t, 512)
    h, n2 = _oproj(attn, rwkv, wts["w_o"], x, lp["ln2"], tm=tm2, name=f"oproj_{tag}")
    y = _ffn(n2, h, wts["w_gate"], wts["w_up"], wts["w_down"], lp["final_norm"], tm=min(t, 1024), tf=512,
             name=f"ffn_{tag}")
    return y, z_attn, z_rwkv, s_fin


def kernel(x_prompt, x_sample, cache_attn_k, cache_attn_v, state_rwkv_wkv, state_rwkv_shift, ln1, w_in, rel_table,
           mu, w0, w_w2, a0, a_w2, g_w2, k_k, k_a, r_k, lnx_w, lnx_b, w_o, ln2, w_gate, w_up, w_down, final_norm):
    depth = w_in.shape[0]
    assert depth == 1, "final RMSNorm is fused into the (single) layer's FFN call"
    b, t, _ = x_prompt.shape
    bd, s, _ = x_sample.shape
    past = cache_attn_k.shape[2]
    n_keep = min(BAND_PAST, t)
    l = 0
    row = lambda p: p[l].reshape(1, -1)
    lp = dict(ln1=row(ln1), mu=row(mu), w0=row(w0), a0=row(a0),
              w2cat=jnp.concatenate([w_w2[l], a_w2[l]], axis=0).astype(BF16), g_w2=g_w2[l].astype(BF16),
              k_k=row(k_k), k_a=row(k_a), r_k=row(r_k), lnx_w=row(lnx_w), lnx_b=row(lnx_b), ln2=row(ln2),
              final_norm=final_norm.reshape(1, -1))
    wts = dict(w_in=w_in[l].astype(BF16), w_o=w_o[l], w_gate=w_gate[l], w_up=w_up[l], w_down=w_down[l])
    table = rel_table[l]

    bias_p = _bias_tiles(table, mq=CHUNK, off=BAND_PAST + CHUNK, col_lo=CHUNK, col_hi=KEY_WIN, name="bias_prompt")
    bias_s = _bias_tiles(table, mq=s, off=past, col_lo=0, col_hi=past + s, name="bias_sample")

    assert b == 1
    yp, zap, zrp, wkv_p = _trunk(
        x_prompt.reshape(b * t, D_MODEL), lp, wts, lambda z: _attn_prompt(z, bias_p),
        jnp.zeros((b, 1, SHIFT_W), F32), jnp.zeros((b, N_HEADS, HEAD_DIM, HEAD_DIM), F32), nb=b, seq=t, tag="p")
    kc = cache_attn_k[l].reshape(bd, past, ATTN_WIDTH)
    vc = cache_attn_v[l].reshape(bd, past, ATTN_WIDTH)
    ys, zas, zrs, wkv_s = _trunk(
        x_sample.reshape(bd * s, D_MODEL), lp, wts, lambda z: _attn_sample(z, kc, vc, bias_s, nb=bd, s=s),
        state_rwkv_shift[l], state_rwkv_wkv[l], nb=bd, seq=s, tag="s")

    heads = lambda z, n, tt: z.reshape(n, tt, N_HEADS, HEAD_DIM)
    zap = zap.reshape(b, t, 3 * ATTN_WIDTH)
    zas = zas.reshape(bd, s, 3 * ATTN_WIDTH)
    k_p = heads(zap[:, t - n_keep:, ATTN_WIDTH:2 * ATTN_WIDTH], b, n_keep)
    v_p = heads(zap[:, t - n_keep:, 2 * ATTN_WIDTH:], b, n_keep)
    k_s = heads(zas[:, :, ATTN_WIDTH:2 * ATTN_WIDTH], bd, s)
    v_s = heads(zas[:, :, 2 * ATTN_WIDTH:], bd, s)
    sh_p = zrp.reshape(b, t, SHIFT_W)[:, t - 1:]
    sh_s = zrs.reshape(bd, s, SHIFT_W)[:, s - 1:]
    return (yp.reshape(b, t, D_MODEL), ys.reshape(bd, s, D_MODEL),
            k_p[None], v_p[None], wkv_p[None], sh_p[None],
            k_s[None], v_s[None], wkv_s[None], sh_s[None])
```

```python
import functools
import math

import jax
import jax.numpy as jnp
from jax import lax
from jax.experimental import pallas as pl
from jax.experimental.pallas import tpu as pltpu

F32 = jnp.float32
BF16 = jnp.bfloat16

D_MODEL = 2048
CHUNK = 64
N_PAST_CHUNKS = 8
BAND_PAST = N_PAST_CHUNKS * CHUNK
ATTN_WIDTH = 1024
HEAD_DIM = 64
N_HEADS = 16
REL_CLIP = 128
N_REL = 2 * REL_CLIP + 1
RWKV_WIDTH = 1024
DECAY_RANK = 64
AAA_RANK = 64
GATE_RANK = 128
SHIFT_W = 3 * RWKV_WIDTH + DECAY_RANK + AAA_RANK + GATE_RANK
D_FF = 5632
RMS_EPS = 1e-6
GN_EPS = 64e-5

V7X_LANES = 128
V7X_VMEM_BYTES = 64 * 1024 * 1024
V7X_SCOPED_VMEM_CAP_BYTES = 60000 * 1024

PAIR_W = 2 * HEAD_DIM
N_PAIRS = N_HEADS // 2
CHUNK_T = 64
KEY_WIN = (N_PAST_CHUNKS + 2) * CHUNK
NEG_BIG = -1e30
ATTN_QB = 512
ATTN_CHUNKS_PER_ITER = 2
ATTN_SOFTMAX_ROWS = 128
RWKV_CHUNKS_PER_STEP = 2

assert PAIR_W == V7X_LANES


def _vmem_limit(pipelined_bytes, scratch_bytes, temp_bytes):
    del pipelined_bytes, scratch_bytes, temp_bytes
    return V7X_SCOPED_VMEM_CAP_BYTES


def _nbytes(shape, dtype):
    return math.prod(shape) * jnp.dtype(dtype).itemsize


def _params(n_grid, vmem, flags=None):
    return pltpu.CompilerParams(dimension_semantics=("arbitrary",) * n_grid, vmem_limit_bytes=vmem, flags=flags)


def _rms(x, g):
    return x * lax.rsqrt(jnp.mean(x * x, axis=-1, keepdims=True) + RMS_EPS) * g


def _inproj_norm_kernel(x_ref, g_ref, w_ref, o_ref, nx_ref):
    @pl.when(pl.program_id(1) == 0)
    def _():
        nx_ref[...] = _rms(x_ref[...], g_ref[...]).astype(BF16)

    o_ref[...] = jnp.dot(nx_ref[...], w_ref[...], preferred_element_type=F32).astype(o_ref.dtype)


def _inproj_kernel(nx_ref, w_ref, o_ref):
    o_ref[...] = jnp.dot(nx_ref[...], w_ref[...], preferred_element_type=F32).astype(o_ref.dtype)


def _inproj(x, g, w, *, col0, n, tm, tn, name, out_dtype=F32):
    t, d = x.shape
    assert t % tm == 0 and n % tn == 0 and col0 % V7X_LANES == 0 and tn % V7X_LANES == 0 and col0 + n <= w.shape[1]
    normalise = x.dtype != BF16
    assert normalise == (g is not None)
    w_spec = pl.BlockSpec((pl.Element(d), pl.Element(tn)),
                          lambda i, j: (0, (col0 // V7X_LANES + j * (tn // V7X_LANES)) * V7X_LANES))
    row_blk = pl.BlockSpec((tm, d), lambda i, j: (i, 0))
    z_shape = jax.ShapeDtypeStruct((t, n), out_dtype)
    z_spec = pl.BlockSpec((tm, tn), lambda i, j: (i, j))
    blocks = _nbytes((tm, d), x.dtype) + _nbytes((d, tn), BF16) + _nbytes((tm, tn), out_dtype)
    if not normalise:
        return pl.pallas_call(
            _inproj_kernel, out_shape=z_shape, grid=(t // tm, n // tn), in_specs=[row_blk, w_spec], out_specs=z_spec,
            compiler_params=_params(2, _vmem_limit(blocks, 0, _nbytes((tm, tn), F32))), name=name,
        )(x, w)
    return pl.pallas_call(
        _inproj_norm_kernel,
        out_shape=(z_shape, jax.ShapeDtypeStruct((t, d), BF16)),
        grid=(t // tm, n // tn),
        in_specs=[row_blk, pl.BlockSpec((1, d), lambda i, j: (0, 0)), w_spec],
        out_specs=(z_spec, row_blk),
        compiler_params=_params(2, _vmem_limit(blocks + _nbytes((tm, d), BF16), 0, 2 * _nbytes((tm, d), F32))),
        name=name,
    )(x, g, w)


BIAS_VAR_COL0 = KEY_WIN - 2 * V7X_LANES
BIAS_DIAG_W = 512


def _bias_kernel(tab_ref, o_ref, *, mq, off, col_lo, col_hi):
    nvar = KEY_WIN - BIAS_VAR_COL0
    w = BIAS_DIAG_W
    assert nvar + mq <= w
    lane = lax.broadcasted_iota(jnp.int32, (1, w), 1)
    idx = jnp.clip(off - (BIAS_VAR_COL0 - mq + lane), -REL_CLIP, REL_CLIP) + REL_CLIP
    idx_min = max(0, min(REL_CLIP, off - (BIAS_VAR_COL0 - mq + w - 1)) + REL_CLIP)
    heads = range(N_HEADS)

    def body(j, profs):
        hit = idx == j
        return tuple(jnp.where(hit, tab_ref[h, j], prof) for h, prof in zip(heads, profs))

    init = tuple(jnp.full((1, w), tab_ref[h, N_REL - 1], F32) for h in heads)
    profs = lax.fori_loop(idx_min, N_REL - 1, body, init)

    c = lax.broadcasted_iota(jnp.int32, (mq, nvar), 1) + BIAS_VAR_COL0
    valid_var = (c >= col_lo) & (c < col_hi)
    c_far = lax.broadcasted_iota(jnp.int32, (mq, BIAS_VAR_COL0), 1)
    valid_far = (c_far >= col_lo) & (c_far < col_hi)
    for h in heads:
        tile = pltpu.roll(jnp.broadcast_to(profs[h], (mq, w)), w - mq, axis=1, stride=1, stride_axis=0)
        rows = slice((h % 2) * mq, (h % 2 + 1) * mq)
        far = jnp.full((mq, BIAS_VAR_COL0), tab_ref[h, N_REL - 1], F32)
        o_ref[h // 2, rows, 0:BIAS_VAR_COL0] = jnp.where(valid_far, far, NEG_BIG)
        o_ref[h // 2, rows, BIAS_VAR_COL0:KEY_WIN] = jnp.where(valid_var, tile[:, 0:nvar], NEG_BIG)


def _bias_tiles(table, *, mq, off, col_lo, col_hi, name):
    assert off - (BIAS_VAR_COL0 - 1) >= REL_CLIP
    return pl.pallas_call(
        functools.partial(_bias_kernel, mq=mq, off=off, col_lo=col_lo, col_hi=col_hi),
        out_shape=jax.ShapeDtypeStruct((N_PAIRS, 2 * mq, KEY_WIN), F32),
        in_specs=[pl.BlockSpec(memory_space=pltpu.SMEM)],
        out_specs=pl.BlockSpec(memory_space=pltpu.VMEM),
        name=name,
    )(table)


def _lane_lo(rows):
    return lax.broadcasted_iota(jnp.int32, (rows, PAIR_W), 1) < HEAD_DIM


def _attend(q_pair, k_pair, v_pair, bias_ref, o_put, s_scr, p_scr, *, mq, n_win, min_col):
    lo = _lane_lo(mq)
    units = [(w, p) for w in range(n_win) for p in range(N_PAIRS)]
    for u, (w, p) in enumerate(units):
        qs = q_pair(w, p) * (HEAD_DIM ** -0.5)
        q2 = jnp.concatenate([jnp.where(lo, qs, 0.0), jnp.where(lo, 0.0, qs)], axis=0).astype(BF16)
        s = lax.dot_general(q2, k_pair(w, p), (((1,), (1,)), ((), ())), preferred_element_type=F32) + bias_ref[p]
        first = min_col(w)
        if first is not None:
            col = lax.broadcasted_iota(jnp.int32, (1, KEY_WIN), 1)
            s = jnp.where(col >= first, s, NEG_BIG)
        s_scr[u] = s
    inv_l = []
    rb = min(2 * mq, ATTN_SOFTMAX_ROWS)
    for u in range(len(units)):
        parts = []
        for r0 in range(0, 2 * mq, rb):
            s = s_scr[u, r0:r0 + rb, :]
            e = jnp.exp(s - jnp.max(s, axis=-1, keepdims=True))
            parts.append(1.0 / jnp.sum(e, axis=-1, keepdims=True))
            p_scr[u, r0:r0 + rb, :] = e.astype(BF16)
        inv_l.append(jnp.concatenate(parts, axis=0))
    for u, (w, p) in enumerate(units):
        pv = jnp.dot(p_scr[u], v_pair(w, p), preferred_element_type=F32) * inv_l[u]
        o_put(w, p, jnp.where(lo, pv[:mq], pv[mq:]))


def _attn_prompt_kernel(q_ref, kc_ref, vc_ref, bias_ref, o_ref, kbuf, vbuf, s_scr, p_scr):
    i = pl.program_id(0)
    prev0, cur0, end = CHUNK, CHUNK + ATTN_QB, CHUNK + 2 * ATTN_QB

    @pl.when(i == 0)
    def _():
        zeros = jnp.zeros((cur0, ATTN_WIDTH), BF16)
        kbuf[0:cur0, :] = zeros
        vbuf[0:cur0, :] = zeros

    kbuf[cur0:end, :] = kc_ref[...].astype(BF16)
    vbuf[cur0:end, :] = vc_ref[...].astype(BF16)
    chunks_per_step = ATTN_QB // CHUNK
    lead_chunks = N_PAST_CHUNKS + 1
    n_win = ATTN_CHUNKS_PER_ITER

    def iter_body(j, carry, *, masked):
        sl = lambda p: slice(p * PAIR_W, (p + 1) * PAIR_W)
        r0 = lambda w: pl.multiple_of((j * n_win + w) * CHUNK, CHUNK)
        min_col = lambda w: (lead_chunks - (i * chunks_per_step + j * n_win + w)) * CHUNK if masked else None

        def o_put(w, p, o):
            o_ref[pl.ds(r0(w), CHUNK), sl(p)] = o.astype(BF16)

        _attend(lambda w, p: q_ref[pl.ds(r0(w), CHUNK), sl(p)], lambda w, p: kbuf[pl.ds(r0(w), KEY_WIN), sl(p)],
                lambda w, p: vbuf[pl.ds(r0(w), KEY_WIN), sl(p)], bias_ref, o_put, s_scr, p_scr, mq=CHUNK,
                n_win=n_win, min_col=min_col)
        return carry

    masked_steps = -(-lead_chunks // chunks_per_step)

    @pl.when(i < masked_steps)
    def _():
        lax.fori_loop(0, chunks_per_step // n_win, functools.partial(iter_body, masked=True), 0)

    @pl.when(i >= masked_steps)
    def _():
        lax.fori_loop(0, chunks_per_step // n_win, functools.partial(iter_body, masked=False), 0)

    kbuf[prev0:cur0, :] = kbuf[cur0:end, :]
    vbuf[prev0:cur0, :] = vbuf[cur0:end, :]


def _attn_prompt(z_attn, bias):
    t = z_attn.shape[0]
    assert t % ATTN_QB == 0 and (ATTN_QB // CHUNK) % ATTN_CHUNKS_PER_ITER == 0 and ATTN_QB >= BAND_PAST
    blk = (ATTN_QB, ATTN_WIDTH)
    buf_rows = CHUNK + 2 * ATTN_QB
    s_shape = (ATTN_CHUNKS_PER_ITER * N_PAIRS, 2 * CHUNK, KEY_WIN)
    blocks = 3 * _nbytes(blk, z_attn.dtype) + _nbytes(bias.shape, F32) + _nbytes(blk, BF16)
    scratch = 2 * _nbytes((buf_rows, ATTN_WIDTH), BF16) + _nbytes(s_shape, F32) + _nbytes(s_shape, BF16)
    return pl.pallas_call(
        _attn_prompt_kernel,
        out_shape=jax.ShapeDtypeStruct((t, ATTN_WIDTH), BF16),
        grid=(t // ATTN_QB,),
        in_specs=[pl.BlockSpec(blk, lambda i: (i, 0)),
                  pl.BlockSpec(blk, lambda i: (i, 1)),
                  pl.BlockSpec(blk, lambda i: (i, 2)),
                  pl.BlockSpec(bias.shape, lambda i: (0, 0, 0))],
        out_specs=pl.BlockSpec(blk, lambda i: (i, 0)),
        scratch_shapes=[pltpu.VMEM((buf_rows, ATTN_WIDTH), BF16), pltpu.VMEM((buf_rows, ATTN_WIDTH), BF16),
                        pltpu.VMEM(s_shape, F32), pltpu.VMEM(s_shape, BF16)],
        compiler_params=_params(1, _vmem_limit(blocks, scratch, 4 * _nbytes((2 * CHUNK, KEY_WIN), F32))),
        name="attn_prompt",
    )(z_attn, z_attn, z_attn, bias)


def _attn_sample_kernel(q_ref, kn_ref, vn_ref, kc_ref, vc_ref, bias_ref, o_ref, kbuf, vbuf, s_scr, p_scr, *,
                        s, past):
    tail = jnp.zeros((KEY_WIN - past - s, ATTN_WIDTH), BF16)
    kbuf[0:past, :] = kc_ref[...].astype(BF16)
    vbuf[0:past, :] = vc_ref[...].astype(BF16)
    kbuf[past:past + s, :] = kn_ref[...].astype(BF16)
    vbuf[past:past + s, :] = vn_ref[...].astype(BF16)
    kbuf[past + s:KEY_WIN, :] = tail
    vbuf[past + s:KEY_WIN, :] = tail
    sl = lambda p: slice(p * PAIR_W, (p + 1) * PAIR_W)

    def o_put(w, p, o):
        o_ref[:, sl(p)] = o.astype(BF16)

    _attend(lambda w, p: q_ref[:, sl(p)], lambda w, p: kbuf[:, sl(p)], lambda w, p: vbuf[:, sl(p)], bias_ref,
            o_put, s_scr, p_scr, mq=s, n_win=1, min_col=lambda w: None)


def _attn_sample(z_attn, k_cache, v_cache, bias, *, nb, s):
    past = k_cache.shape[1]
    assert past + s <= KEY_WIN and z_attn.shape[0] == nb * s
    blk = (s, ATTN_WIDTH)
    cblk = (None, past, ATTN_WIDTH)
    s_shape = (N_PAIRS, 2 * s, KEY_WIN)
    blocks = (3 * _nbytes(blk, F32) + 2 * _nbytes((past, ATTN_WIDTH), F32) + _nbytes(bias.shape, F32)
              + _nbytes(blk, BF16))
    scratch = 2 * _nbytes((KEY_WIN, ATTN_WIDTH), BF16) + _nbytes(s_shape, F32) + _nbytes(s_shape, BF16)
    return pl.pallas_call(
        functools.partial(_attn_sample_kernel, s=s, past=past),
        out_shape=jax.ShapeDtypeStruct((nb * s, ATTN_WIDTH), BF16),
        grid=(nb,),
        in_specs=[pl.BlockSpec(blk, lambda b: (b, 0)),
                  pl.BlockSpec(blk, lambda b: (b, 1)),
                  pl.BlockSpec(blk, lambda b: (b, 2)),
                  pl.BlockSpec(cblk, lambda b: (b, 0, 0)),
                  pl.BlockSpec(cblk, lambda b: (b, 0, 0)),
                  pl.BlockSpec(bias.shape, lambda b: (0, 0, 0))],
        out_specs=pl.BlockSpec(blk, lambda b: (b, 0)),
        scratch_shapes=[pltpu.VMEM((KEY_WIN, ATTN_WIDTH), BF16), pltpu.VMEM((KEY_WIN, ATTN_WIDTH), BF16),
                        pltpu.VMEM(s_shape, F32), pltpu.VMEM(s_shape, BF16)],
        compiler_params=_params(1, _vmem_limit(blocks, scratch, 4 * _nbytes((2 * s, KEY_WIN), F32))),
        name="attn_sample",
    )(z_attn, z_attn, z_attn, k_cache, v_cache, bias)


def _bf(x):
    return x.astype(BF16)


def _split2(x):
    hi = _bf(x)
    return hi, _bf(x - hi.astype(F32))


_NN = (((1,), (0,)), ((), ()))
_NT = (((1,), (1,)), ((), ()))


def _mm(x, y, dims=_NN):
    return lax.dot_general(x, y, dims, preferred_element_type=F32)


def _bd(y, lo):
    zero = jnp.zeros_like(y)
    return jnp.concatenate([jnp.where(lo, y, zero), jnp.where(lo, zero, y)], axis=0)


def _seg_sum(x, ones_seg, *, passes):
    r = x.shape[0]
    half = x.shape[1] // (2 * PAIR_W)
    slab = lambda p: x[:, p * PAIR_W:(p + 1) * PAIR_W]
    xx = jnp.concatenate([jnp.concatenate([slab(p) for p in range(half)], axis=0),
                          jnp.concatenate([slab(half + p) for p in range(half)], axis=0)], axis=1)
    hi, lo = _split2(xx)
    s = _mm(hi, ones_seg)
    if passes == 2:
        s = s + _mm(lo, ones_seg)
    return jnp.concatenate([s[p * r:(p + 1) * r, 0:PAIR_W] for p in range(half)]
                           + [s[p * r:(p + 1) * r, PAIR_W:2 * PAIR_W] for p in range(half)], axis=1)


def _sigmoid(x):
    return 1.0 / (1.0 + jnp.exp(-x))


def _sigmoid_tanh(x):
    return 0.5 + 0.5 * jnp.tanh(0.5 * x)


_TOK_F32 = ("at", "rt", "v", "bonus", "g")
_TOK_BF16 = ("at", "bt", "kt", "rt", "v", "bp", "kp")


_RWKV_N_IN = 14
_RWKV_N_OUT = 2
_RWKV_N_SCRATCH = 6


def _rwkv_kernel(*refs, rows, cps, nck, cast_periods):
    n_cast = len(cast_periods)
    (zf_ref, zn_ref, sh_ref, s0_ref, mu_ref, w0_ref, w2cat_ref, a0_ref, gw2_ref, kk_ref, ka_ref, rk_ref, lnw_ref,
     lnb_ref) = refs[:_RWKV_N_IN]
    cast_in = refs[_RWKV_N_IN:_RWKV_N_IN + n_cast]
    o_ref, sfin_ref = refs[_RWKV_N_IN + n_cast:_RWKV_N_IN + n_cast + _RWKV_N_OUT]
    cast_out = refs[_RWKV_N_IN + n_cast + _RWKV_N_OUT:_RWKV_N_IN + 2 * n_cast + _RWKV_N_OUT]
    s_scr, carry_scr, y_scr, f32_scr, bf_scr, pc_scr = refs[-_RWKV_N_SCRATCH:]
    ct = CHUNK_T
    pw = PAIR_W
    tt = cps * ct
    assert rows == ct or cps == 1
    c = pl.program_id(1)

    seg = 2 * pw
    ones_seg = _bf(lax.broadcasted_iota(jnp.int32, (seg, seg), 0) // HEAD_DIM
                   == lax.broadcasted_iota(jnp.int32, (seg, seg), 1) // HEAD_DIM)
    row = lax.broadcasted_iota(jnp.int32, (ct, 1), 0)
    row_all = lax.broadcasted_iota(jnp.int32, (tt, 1), 0)
    tri = _bf(lax.broadcasted_iota(jnp.int32, (ct, ct), 1) <= lax.broadcasted_iota(jnp.int32, (ct, ct), 0))
    t_lane = lax.broadcasted_iota(jnp.int32, (ct, 2 * ct), 1) % ct
    strict = t_lane < row
    incl = t_lane <= row
    lo_t = _lane_lo(ct)
    lo_p = lax.broadcasted_iota(jnp.int32, (ct, 2 * ct), 1) < ct
    lo_t2 = lax.broadcasted_iota(jnp.int32, (ct, 2 * pw), 1) % pw < HEAD_DIM
    diag = (lax.broadcasted_iota(jnp.int32, (HEAD_DIM, pw), 1) % HEAD_DIM
            == lax.broadcasted_iota(jnp.int32, (HEAD_DIM, pw), 0))
    lo_s = _lane_lo(HEAD_DIM)
    n_dbl = int(math.log2(ct))
    assert 1 << n_dbl == ct and 2 * ct == pw
    pairs = range(N_PAIRS)
    cat0 = lambda xs: jnp.concatenate(xs, axis=0)
    cat1 = lambda xs: jnp.concatenate(xs, axis=1)

    def tn(xv, yv):
        zz = _mm(_bf(xv.T), yv)
        return jnp.where(lo_s, zz[:HEAD_DIM], zz[HEAD_DIM:])

    def chunk_pre(zc):
        r = zc[:, 0:RWKV_WIDTH]
        k = zc[:, RWKV_WIDTH:2 * RWKV_WIDTH]
        v = zc[:, 2 * RWKV_WIDTH:3 * RWKV_WIDTH]
        zwa = zc[:, 3 * RWKV_WIDTH:3 * RWKV_WIDTH + DECAY_RANK + AAA_RANK]
        zg = zc[:, 3 * RWKV_WIDTH + DECAY_RANK + AAA_RANK:SHIFT_W]
        lora_in = cat0([jnp.where(lo_t, jnp.tanh(zwa), 0.0), jnp.where(lo_t, 0.0, zwa)])
        lora = _mm(_bf(lora_in), w2cat_ref[...])
        yield
        lw = -math.exp(-0.5) * _sigmoid_tanh(w0_ref[...] + lora[:ct])
        if rows < ct:
            live = row < rows
            lw = jnp.where(live, lw, 0.0)
        l_hi, l_lo = _split2(lw)
        cl = _mm(tri, l_hi) + _mm(tri, l_lo)
        yield
        a = _sigmoid_tanh(a0_ref[...] + lora[ct:])
        g = _mm(_bf(_sigmoid_tanh(zg)), gw2_ref[...])
        yield
        kkr = k * kk_ref[...]
        kn = kkr * lax.rsqrt(jnp.maximum(_seg_sum(kkr * kkr, ones_seg, passes=1), 1e-24))
        yield
        k2 = k * (1.0 + (a - 1.0) * ka_ref[...])
        bonus = _seg_sum(r * k2 * rk_ref[...], ones_seg, passes=1) * v
        yield
        e_in = jnp.exp(cl)
        rt = r * e_in
        pc = e_in[ct - 1:ct, :]
        yield
        at = -kn * jnp.exp(cl - lw)
        yield
        e_inv = jnp.exp(-cl)
        bt = kn * a * e_inv
        kt = k2 * e_inv
        if rows < ct:
            bt = jnp.where(live, bt, 0.0)
            kt = jnp.where(live, kt, 0.0)
        yield
        return dict(at=at, bt=bt, kt=kt, rt=rt, v=v, bp=bt * pc, kp=kt * pc, pc=pc, bonus=bonus, g=g)

    def pre_block(z_ref, carry, wr):
        z = z_ref[...]
        if rows < ct:
            z = jnp.concatenate([z, jnp.zeros((ct - rows, SHIFT_W), F32)], axis=0)
        zprev = jnp.where(row_all == 0, carry, pltpu.roll(z, 1, axis=0))
        last = (cps - 1) * ct + rows - 1
        carry_scr[0:1, :] = z[last:last + 1, :]
        zs = z + (zprev - z) * mu_ref[...]
        yield
        for gi in range(cps):
            t = yield from chunk_pre(zs[gi * ct:(gi + 1) * ct])
            rs = slice(gi * ct, (gi + 1) * ct)
            for i, name in enumerate(_TOK_F32):
                f32_scr[wr, i, rs, :] = t[name]
            yield
            for i, name in enumerate(_TOK_BF16):
                bf_scr[wr, i, rs, :] = _bf(t[name])
            pc_scr[wr, gi:gi + 1, :] = t["pc"]
            yield

    def chunk_stages(rd, gi):
        rs = slice(gi * ct, (gi + 1) * ct)
        f32 = lambda name, p: f32_scr[rd, _TOK_F32.index(name), rs, p * pw:(p + 1) * pw]
        b16 = lambda name, p: bf_scr[rd, _TOK_BF16.index(name), rs, p * pw:(p + 1) * pw]
        v_bd = [_bd(b16("v", p), lo_t) for p in pairs]
        amat = [_mm(cat0([b16("at", p), b16("rt", p)]), cat0([_bd(b16("bt", p), lo_t), _bd(b16("kt", p), lo_t)]),
                    _NT) for p in pairs]
        yield
        a_ab = [jnp.where(strict, amat[p][:ct, :pw], 0.0) for p in pairs]
        a_ak = [_bf(jnp.where(strict, amat[p][:ct, pw:], 0.0)) for p in pairs]
        a_rb = [_bf(jnp.where(incl, amat[p][ct:, :pw], 0.0)) for p in pairs]
        a_rk = [_bf(jnp.where(incl, amat[p][ct:, pw:], 0.0)) for p in pairs]
        x = [cat1([f32("at", p), _mm(a_ak[p], v_bd[p])]) for p in pairs]
        npow = [_bf(a_ab[p]) for p in pairs]
        yield
        for it in range(n_dbl):
            if it + 1 < n_dbl:
                res = [_mm(npow[p], cat1([_bd(_bf(x[p]), lo_t2), _bd(npow[p], lo_p)])) for p in pairs]
                npow = [_bf(res[p][:, 2 * pw:]) for p in pairs]
            else:
                res = [_mm(npow[p], _bd(_bf(x[p]), lo_t2)) for p in pairs]
            x = [x[p] + res[p][:, :2 * pw] for p in pairs]
            yield
        ry = [_mm(a_rb[p], _bd(_bf(x[p]), lo_t2)) for p in pairs]
        rp = [_bf(f32("rt", p) + ry[p][:, :pw]) for p in pairs]
        y0 = [ry[p][:, pw:] + _mm(a_rk[p], v_bd[p]) for p in pairs]
        yield
        m_mat = [_bf(jnp.where(diag, pc_scr[rd, gi:gi + 1, p * pw:(p + 1) * pw], 0.0)
                     + tn(x[p][:, :pw], b16("bp", p))) for p in pairs]
        yield
        n_mat = [tn(cat0([x[p][:, pw:], f32("v", p)]), cat0([b16("bp", p), b16("kp", p)])) for p in pairs]
        return rp, y0, m_mat, n_mat

    def lockstep(*gens):
        done = [None] * len(gens)
        live = list(range(len(gens)))
        while live:
            for i in list(live):
                try:
                    next(gens[i])
                except StopIteration as stop:
                    done[i] = stop.value
                    live.remove(i)
            yield
        return done

    def main_block(rd):
        terms = yield from lockstep(*[chunk_stages(rd, gi) for gi in range(cps)])
        state = [s_scr[:, p * pw:(p + 1) * pw] for p in pairs]
        for gi, (rp, y0, m_mat, n_mat) in enumerate(terms):
            for p in pairs:
                s_old = _bf(state[p])
                y_scr[gi * ct:(gi + 1) * ct, p * pw:(p + 1) * pw] = _mm(rp[p], _bd(s_old, lo_s), _NT) + y0[p]
                state[p] = _mm(s_old, _bd(m_mat[p], lo_s)) + n_mat[p]
            yield
        for p in pairs:
            s_scr[:, p * pw:(p + 1) * pw] = state[p]
        y = y_scr[...]
        mean = _seg_sum(y, ones_seg, passes=2) * (1.0 / HEAD_DIM)
        dv = y - mean
        yield
        var = _seg_sum(dv * dv, ones_seg, passes=1) * (1.0 / HEAD_DIM)
        yn = dv * lax.rsqrt(var + GN_EPS) * lnw_ref[...] + lnb_ref[...]
        out = (yn + f32_scr[rd, _TOK_F32.index("bonus")]) * f32_scr[rd, _TOK_F32.index("g")]
        o_ref[...] = out[:(cps - 1) * ct + rows].astype(BF16)

    def cast_block(parity):
        for src, dst, period in zip(cast_in, cast_out, cast_periods):
            if parity % period == 0:
                dst[...] = _bf(src[...])
            yield

    def run(*gens):
        for _ in lockstep(*gens):
            pass

    @pl.when(c == 0)
    def _():
        for p in pairs:
            s_scr[:, p * pw:(p + 1) * pw] = jnp.concatenate([s0_ref[2 * p], s0_ref[2 * p + 1]], axis=1)
        run(pre_block(zf_ref, sh_ref[...], 0))

    if nck == 1:
        assert n_cast == 0
        run(main_block(0))
    else:
        for parity in (0, 1):
            @pl.when(c % 2 == parity)
            def _(parity=parity):
                run(main_block(parity), pre_block(zn_ref, carry_scr[0:1, :], 1 - parity), cast_block(parity))

    @pl.when(c == nck - 1)
    def _():
        for h in range(N_HEADS):
            sfin_ref[h] = s_scr[:, h * HEAD_DIM:(h + 1) * HEAD_DIM]


def _rwkv(z_rwkv, shift0, s0, lp, *, nb, seq, name, cast_weights=()):
    rows = min(seq, CHUNK_T)
    cps = min(seq // rows, RWKV_CHUNKS_PER_STEP)
    blk_rows = cps * rows
    assert seq % blk_rows == 0 and z_rwkv.shape == (nb * seq, SHIFT_W)
    nck = seq // blk_rows
    tt = cps * CHUNK_T
    row1 = lambda n: pl.BlockSpec((1, n), lambda b, c: (0, 0))
    full = lambda r, n: pl.BlockSpec((r, n), lambda b, c: (0, 0))
    st_blk = pl.BlockSpec((None, N_HEADS, HEAD_DIM, HEAD_DIM), lambda b, c: (b, 0, 0, 0))
    lora_rows = DECAY_RANK + AAA_RANK
    blocks = (2 * _nbytes((blk_rows, SHIFT_W), F32) + 2 * _nbytes((HEAD_DIM, RWKV_WIDTH), F32)
              + _nbytes((lora_rows + GATE_RANK, RWKV_WIDTH), BF16) + _nbytes((16, SHIFT_W), F32)
              + _nbytes((blk_rows, RWKV_WIDTH), BF16))
    stage_shapes = [((2, len(_TOK_F32), tt, RWKV_WIDTH), F32), ((2, len(_TOK_BF16), tt, RWKV_WIDTH), BF16),
                    ((2, 8, RWKV_WIDTH), F32)]
    scratch = (_nbytes((HEAD_DIM, RWKV_WIDTH), F32) + _nbytes((8, SHIFT_W), F32)
               + _nbytes((tt, RWKV_WIDTH), F32) + sum(_nbytes(s, d) for s, d in stage_shapes))
    assert cps <= 8
    bf16_rows = 16
    cast_periods, cast_specs, cast_shapes = [], [], []
    for w in cast_weights:
        assert nb == 1 and nck % 2 == 0
        period = next(p for p in (1, 2) if w.shape[0] % (nck // p) == 0 and (w.shape[0] // (nck // p)) % bf16_rows == 0)
        slab = (w.shape[0] // (nck // period), w.shape[1])
        cast_periods.append(period)
        cast_specs.append(pl.BlockSpec(slab, lambda b, c, period=period: (c // period, 0)))
        cast_shapes.append(jax.ShapeDtypeStruct(w.shape, BF16))
        blocks += _nbytes(slab, F32) + _nbytes(slab, BF16)
    outs = pl.pallas_call(
        functools.partial(_rwkv_kernel, rows=rows, cps=cps, nck=nck, cast_periods=tuple(cast_periods)),
        out_shape=(jax.ShapeDtypeStruct((nb * seq, RWKV_WIDTH), BF16),
                   jax.ShapeDtypeStruct((nb, N_HEADS, HEAD_DIM, HEAD_DIM), F32), *cast_shapes),
        grid=(nb, nck),
        in_specs=[pl.BlockSpec((blk_rows, SHIFT_W), lambda b, c: (b * nck, 0)),
                  pl.BlockSpec((blk_rows, SHIFT_W), lambda b, c: (b * nck + jnp.minimum(c + 1, nck - 1), 0)),
                  pl.BlockSpec((None, 1, SHIFT_W), lambda b, c: (b, 0, 0)),
                  st_blk,
                  row1(SHIFT_W), row1(RWKV_WIDTH), full(lora_rows, RWKV_WIDTH), row1(RWKV_WIDTH),
                  full(GATE_RANK, RWKV_WIDTH), row1(RWKV_WIDTH),
                  row1(RWKV_WIDTH), row1(RWKV_WIDTH), row1(RWKV_WIDTH), row1(RWKV_WIDTH), *cast_specs],
        out_specs=(pl.BlockSpec((blk_rows, RWKV_WIDTH), lambda b, c: (b * nck + c, 0)), st_blk, *cast_specs),
        scratch_shapes=[pltpu.VMEM((HEAD_DIM, RWKV_WIDTH), F32), pltpu.VMEM((8, SHIFT_W), F32),
                        pltpu.VMEM((tt, RWKV_WIDTH), F32)] + [pltpu.VMEM(s, d) for s, d in stage_shapes],
        compiler_params=_params(2, _vmem_limit(blocks, scratch, 64 * _nbytes((tt, RWKV_WIDTH), F32))),
        name=name,
    )(z_rwkv, z_rwkv, shift0, s0, lp["mu"], lp["w0"], lp["w2cat"], lp["a0"], lp["g_w2"],
      lp["k_k"], lp["k_a"], lp["r_k"], lp["lnx_w"], lp["lnx_b"], *cast_weights)
    return outs[0], outs[1], list(outs[2:])


def _oproj_kernel(a_ref, r_ref, wa_ref, wr_ref, x_ref, g_ref, h_ref, n_ref):
    h = (x_ref[...] + jnp.dot(a_ref[...], wa_ref[...], preferred_element_type=F32)
         + jnp.dot(r_ref[...], wr_ref[...], preferred_element_type=F32))
    h_ref[...] = h
    n_ref[...] = _rms(h, g_ref[...]).astype(BF16)


def _oproj(attn, rwkv, w_o, x, g, *, tm, name):
    t = x.shape[0]
    assert t % tm == 0
    half = (ATTN_WIDTH, D_MODEL)
    blocks = (2 * _nbytes((tm, ATTN_WIDTH), BF16) + 2 * _nbytes(half, BF16) + 2 * _nbytes((tm, D_MODEL), F32)
              + _nbytes((tm, D_MODEL), BF16))
    return pl.pallas_call(
        _oproj_kernel,
        out_shape=(jax.ShapeDtypeStruct((t, D_MODEL), F32), jax.ShapeDtypeStruct((t, D_MODEL), BF16)),
        grid=(t // tm,),
        in_specs=[pl.BlockSpec((tm, ATTN_WIDTH), lambda i: (i, 0)),
                  pl.BlockSpec((tm, RWKV_WIDTH), lambda i: (i, 0)),
                  pl.BlockSpec(half, lambda i: (0, 0)),
                  pl.BlockSpec(half, lambda i: (1, 0)),
                  pl.BlockSpec((tm, D_MODEL), lambda i: (i, 0)),
                  pl.BlockSpec((1, D_MODEL), lambda i: (0, 0))],
        out_specs=(pl.BlockSpec((tm, D_MODEL), lambda i: (i, 0)), pl.BlockSpec((tm, D_MODEL), lambda i: (i, 0))),
        compiler_params=_params(1, _vmem_limit(blocks, 0, 2 * _nbytes((tm, D_MODEL), F32))),
        name=name,
    )(attn, rwkv, w_o, w_o, x, g)


def _ffn_kernel(n_ref, h_ref, wg_ref, wu_ref, wd_ref, g_ref, o_ref):
    j = pl.program_id(1)

    @pl.when(j == 0)
    def _():
        o_ref[...] = h_ref[...]

    n = n_ref[...]
    gate = jnp.dot(n, wg_ref[...], preferred_element_type=F32)
    up = jnp.dot(n, wu_ref[...], preferred_element_type=F32)
    act = (gate * _sigmoid(gate) * up).astype(BF16)
    o_ref[...] += jnp.dot(act, wd_ref[...], preferred_element_type=F32)

    @pl.when(j == pl.num_programs(1) - 1)
    def _():
        o_ref[...] = _rms(o_ref[...], g_ref[...])


def _ffn(n2, h, wg, wu, wd, g, *, tm, tf, name):
    t = h.shape[0]
    assert t % tm == 0 and D_FF % tf == 0
    blocks = (_nbytes((tm, D_MODEL), BF16) + 2 * _nbytes((tm, D_MODEL), F32) + 3 * _nbytes((D_MODEL, tf), BF16))
    temps = 3 * _nbytes((tm, tf), F32) + _nbytes((tm, tf), BF16)
    return pl.pallas_call(
        _ffn_kernel,
        out_shape=jax.ShapeDtypeStruct((t, D_MODEL), F32),
        grid=(t // tm, D_FF // tf),
        in_specs=[pl.BlockSpec((tm, D_MODEL), lambda i, j: (i, 0)),
                  pl.BlockSpec((tm, D_MODEL), lambda i, j: (i, 0)),
                  pl.BlockSpec((D_MODEL, tf), lambda i, j: (0, j)),
                  pl.BlockSpec((D_MODEL, tf), lambda i, j: (0, j)),
                  pl.BlockSpec((tf, D_MODEL), lambda i, j: (j, 0)),
                  pl.BlockSpec((1, D_MODEL), lambda i, j: (0, 0))],
        out_specs=pl.BlockSpec((tm, D_MODEL), lambda i, j: (i, 0)),
        compiler_params=_params(2, _vmem_limit(blocks, 0, temps)),
        name=name,
    )(n2, h, wg, wu, wd, g)


_LATE_WEIGHTS = ("w_o", "w_gate", "w_up", "w_down")


def _trunk(x, lp, wts, attn_fn, shift0, s0, *, nb, seq, tag, n_keep):
    t = nb * seq
    tm = min(t, 1024)
    compact = n_keep < seq
    z_attn, nx = _inproj(x, lp["ln1"], wts["w_in"], col0=0, n=3 * ATTN_WIDTH, tm=tm, tn=1024,
                         name=f"inproj_attn_{tag}", out_dtype=BF16 if compact else F32)
    if compact:
        assert nb == 1
        kv_tail = _inproj(nx[t - n_keep:], None, wts["w_in"], col0=ATTN_WIDTH, n=2 * ATTN_WIDTH, tm=n_keep,
                          tn=ATTN_WIDTH, name=f"inproj_kv_tail_{tag}")
    else:
        kv_tail = z_attn[:, ATTN_WIDTH:]
    z_rwkv = _inproj(nx, None, wts["w_in"], col0=3 * ATTN_WIDTH, n=SHIFT_W, tm=tm, tn=SHIFT_W // 2,
                     name=f"inproj_rwkv_{tag}")
    attn = attn_fn(z_attn)
    pending = [k for k in _LATE_WEIGHTS if wts[k].dtype != BF16]
    rwkv, s_fin, converted = _rwkv(z_rwkv, shift0, s0, lp, nb=nb, seq=seq, name=f"rwkv_{tag}",
                                   cast_weights=[wts[k] for k in pending])
    wts.update(zip(pending, converted))
    tm2 = min(t, 512)
    h, n2 = _oproj(attn, rwkv, wts["w_o"], x, lp["ln2"], tm=tm2, name=f"oproj_{tag}")
    y = _ffn(n2, h, wts["w_gate"], wts["w_up"], wts["w_down"], lp["final_norm"], tm=min(t, 1024), tf=512,
             name=f"ffn_{tag}")
    return y, kv_tail, z_rwkv, s_fin


def kernel(x_prompt, x_sample, cache_attn_k, cache_attn_v, state_rwkv_wkv, state_rwkv_shift, ln1, w_in, rel_table,
           mu, w0, w_w2, a0, a_w2, g_w2, k_k, k_a, r_k, lnx_w, lnx_b, w_o, ln2, w_gate, w_up, w_down, final_norm):
    depth = w_in.shape[0]
    assert depth == 1, "final RMSNorm is fused into the (single) layer's FFN call"
    b, t, _ = x_prompt.shape
    bd, s, _ = x_sample.shape
    past = cache_attn_k.shape[2]
    n_keep = min(BAND_PAST, t)
    l = 0
    row = lambda p: p[l].reshape(1, -1)
    lp = dict(ln1=row(ln1), mu=row(mu), w0=row(w0), a0=row(a0),
              w2cat=jnp.concatenate([w_w2[l], a_w2[l]], axis=0).astype(BF16), g_w2=g_w2[l].astype(BF16),
              k_k=row(k_k), k_a=row(k_a), r_k=row(r_k), lnx_w=row(lnx_w), lnx_b=row(lnx_b), ln2=row(ln2),
              final_norm=final_norm.reshape(1, -1))
    wts = dict(w_in=w_in[l].astype(BF16), w_o=w_o[l], w_gate=w_gate[l], w_up=w_up[l], w_down=w_down[l])
    table = rel_table[l]

    bias_p = _bias_tiles(table, mq=CHUNK, off=BAND_PAST + CHUNK, col_lo=CHUNK, col_hi=KEY_WIN, name="bias_prompt")
    bias_s = _bias_tiles(table, mq=s, off=past, col_lo=0, col_hi=past + s, name="bias_sample")

    assert b == 1
    yp, kvp, zrp, wkv_p = _trunk(
        x_prompt.reshape(b * t, D_MODEL), lp, wts, lambda z: _attn_prompt(z, bias_p),
        jnp.zeros((b, 1, SHIFT_W), F32), jnp.zeros((b, N_HEADS, HEAD_DIM, HEAD_DIM), F32), nb=b, seq=t, tag="p",
        n_keep=n_keep)
    kc = cache_attn_k[l].reshape(bd, past, ATTN_WIDTH)
    vc = cache_attn_v[l].reshape(bd, past, ATTN_WIDTH)
    ys, kvs, zrs, wkv_s = _trunk(
        x_sample.reshape(bd * s, D_MODEL), lp, wts, lambda z: _attn_sample(z, kc, vc, bias_s, nb=bd, s=s),
        state_rwkv_shift[l], state_rwkv_wkv[l], nb=bd, seq=s, tag="s", n_keep=s)

    heads = lambda z, n, tt: z.reshape(n, tt, N_HEADS, HEAD_DIM)
    k_p = heads(kvp[:, :ATTN_WIDTH], b, n_keep)
    v_p = heads(kvp[:, ATTN_WIDTH:], b, n_keep)
    k_s = heads(kvs[:, :ATTN_WIDTH], bd, s)
    v_s = heads(kvs[:, ATTN_WIDTH:], bd, s)
    sh_p = zrp.reshape(b, t, SHIFT_W)[:, t - 1:]
    sh_s = zrs.reshape(bd, s, SHIFT_W)[:, s - 1:]
    return (yp.reshape(b, t, D_MODEL), ys.reshape(bd, s, D_MODEL),
            k_p[None], v_p[None], wkv_p[None], sh_p[None],
            k_s[None], v_s[None], wkv_s[None], sh_s[None])
```

```python
import functools
import math

import jax
import jax.numpy as jnp
from jax import lax
from jax.experimental import pallas as pl
from jax.experimental.pallas import tpu as pltpu

F32 = jnp.float32
BF16 = jnp.bfloat16

D_MODEL = 2048
CHUNK = 64
N_PAST_CHUNKS = 8
BAND_PAST = N_PAST_CHUNKS * CHUNK
ATTN_WIDTH = 1024
HEAD_DIM = 64
N_HEADS = 16
REL_CLIP = 128
N_REL = 2 * REL_CLIP + 1
RWKV_WIDTH = 1024
DECAY_RANK = 64
AAA_RANK = 64
GATE_RANK = 128
SHIFT_W = 3 * RWKV_WIDTH + DECAY_RANK + AAA_RANK + GATE_RANK
D_FF = 5632
RMS_EPS = 1e-6
GN_EPS = 64e-5

V7X_LANES = 128
V7X_VMEM_BYTES = 64 * 1024 * 1024
V7X_SCOPED_VMEM_CAP_BYTES = 60000 * 1024

PAIR_W = 2 * HEAD_DIM
N_PAIRS = N_HEADS // 2
CHUNK_T = 64
KEY_WIN = (N_PAST_CHUNKS + 2) * CHUNK
NEG_BIG = -1e30
ATTN_QB = 512
ATTN_CHUNKS_PER_ITER = 2
ATTN_SOFTMAX_ROWS = 128
RWKV_CHUNKS_PER_STEP = 2

assert PAIR_W == V7X_LANES


def _vmem_limit(pipelined_bytes, scratch_bytes, temp_bytes):
    del pipelined_bytes, scratch_bytes, temp_bytes
    return V7X_SCOPED_VMEM_CAP_BYTES


def _nbytes(shape, dtype):
    return math.prod(shape) * jnp.dtype(dtype).itemsize


def _params(n_grid, vmem, flags=None):
    return pltpu.CompilerParams(dimension_semantics=("arbitrary",) * n_grid, vmem_limit_bytes=vmem, flags=flags)


def _rms(x, g):
    return x * lax.rsqrt(jnp.mean(x * x, axis=-1, keepdims=True) + RMS_EPS) * g


def _inproj_kernel(*refs, normalise, emit_w):
    n_in = 3 if normalise else 2
    w_ref, o_ref, extra = refs[n_in - 1], refs[n_in], list(refs[n_in + 1:])
    if normalise:
        x_ref, g_ref, nx_ref = refs[0], refs[1], extra.pop(0)

        @pl.when(pl.program_id(1) == 0)
        def _():
            nx_ref[...] = _rms(x_ref[...], g_ref[...]).astype(BF16)
    else:
        nx_ref = refs[0]
    w = w_ref[...]
    if emit_w:
        w = w.astype(BF16)
        extra.pop(0)[...] = w
    o_ref[...] = jnp.dot(nx_ref[...], w, preferred_element_type=F32)


def _inproj(x, g, w, *, col0, n, tm, tn, name):
    t, d = x.shape
    assert t % tm == 0 and n % tn == 0 and col0 % V7X_LANES == 0 and tn % V7X_LANES == 0 and col0 + n <= w.shape[1]
    normalise = x.dtype != BF16
    emit_w = w.dtype != BF16
    assert normalise == (g is not None) and (not emit_w or t == tm)
    w_spec = pl.BlockSpec((pl.Element(d), pl.Element(tn)),
                          lambda i, j: (0, (col0 // V7X_LANES + j * (tn // V7X_LANES)) * V7X_LANES))
    row_blk = pl.BlockSpec((tm, d), lambda i, j: (i, 0))
    operands = [x, g, w] if normalise else [x, w]
    in_specs = ([row_blk, pl.BlockSpec((1, d), lambda i, j: (0, 0)), w_spec] if normalise else [row_blk, w_spec])
    out_shape = [jax.ShapeDtypeStruct((t, n), F32)]
    out_specs = [pl.BlockSpec((tm, tn), lambda i, j: (i, j))]
    blocks = _nbytes((tm, d), x.dtype) + _nbytes((d, tn), w.dtype) + _nbytes((tm, tn), F32)
    if normalise:
        out_shape.append(jax.ShapeDtypeStruct((t, d), BF16))
        out_specs.append(row_blk)
        blocks += _nbytes((tm, d), BF16)
    if emit_w:
        out_shape.append(jax.ShapeDtypeStruct((d, n), BF16))
        out_specs.append(pl.BlockSpec((d, tn), lambda i, j: (0, j)))
        blocks += _nbytes((d, tn), BF16)
    outs = list(pl.pallas_call(
        functools.partial(_inproj_kernel, normalise=normalise, emit_w=emit_w),
        out_shape=out_shape, grid=(t // tm, n // tn), in_specs=in_specs, out_specs=out_specs,
        compiler_params=_params(2, _vmem_limit(blocks, 0, 2 * _nbytes((tm, d), F32))), name=name,
    )(*operands))
    z = outs.pop(0)
    nx = outs.pop(0) if normalise else None
    return z, nx, (outs.pop(0) if emit_w else None)


BIAS_VAR_COL0 = KEY_WIN - 2 * V7X_LANES
BIAS_DIAG_W = 512


def _bias_kernel(tab_ref, o_ref, *, mq, off, col_lo, col_hi):
    nvar = KEY_WIN - BIAS_VAR_COL0
    w = BIAS_DIAG_W
    assert nvar + mq <= w
    lane = lax.broadcasted_iota(jnp.int32, (1, w), 1)
    idx = jnp.clip(off - (BIAS_VAR_COL0 - mq + lane), -REL_CLIP, REL_CLIP) + REL_CLIP
    idx_min = max(0, min(REL_CLIP, off - (BIAS_VAR_COL0 - mq + w - 1)) + REL_CLIP)
    heads = range(N_HEADS)

    def body(j, profs):
        hit = idx == j
        return tuple(jnp.where(hit, tab_ref[h, j], prof) for h, prof in zip(heads, profs))

    init = tuple(jnp.full((1, w), tab_ref[h, N_REL - 1], F32) for h in heads)
    profs = lax.fori_loop(idx_min, N_REL - 1, body, init)

    c = lax.broadcasted_iota(jnp.int32, (mq, nvar), 1) + BIAS_VAR_COL0
    valid_var = (c >= col_lo) & (c < col_hi)
    c_far = lax.broadcasted_iota(jnp.int32, (mq, BIAS_VAR_COL0), 1)
    valid_far = (c_far >= col_lo) & (c_far < col_hi)
    for h in heads:
        tile = pltpu.roll(jnp.broadcast_to(profs[h], (mq, w)), w - mq, axis=1, stride=1, stride_axis=0)
        rows = slice((h % 2) * mq, (h % 2 + 1) * mq)
        far = jnp.full((mq, BIAS_VAR_COL0), tab_ref[h, N_REL - 1], F32)
        o_ref[h // 2, rows, 0:BIAS_VAR_COL0] = jnp.where(valid_far, far, NEG_BIG)
        o_ref[h // 2, rows, BIAS_VAR_COL0:KEY_WIN] = jnp.where(valid_var, tile[:, 0:nvar], NEG_BIG)


def _bias_tiles(table, *, mq, off, col_lo, col_hi, name):
    assert off - (BIAS_VAR_COL0 - 1) >= REL_CLIP
    return pl.pallas_call(
        functools.partial(_bias_kernel, mq=mq, off=off, col_lo=col_lo, col_hi=col_hi),
        out_shape=jax.ShapeDtypeStruct((N_PAIRS, 2 * mq, KEY_WIN), F32),
        in_specs=[pl.BlockSpec(memory_space=pltpu.SMEM)],
        out_specs=pl.BlockSpec(memory_space=pltpu.VMEM),
        name=name,
    )(table)


def _lane_lo(rows):
    return lax.broadcasted_iota(jnp.int32, (rows, PAIR_W), 1) < HEAD_DIM


def _attend(q_pair, k_pair, v_pair, bias_ref, o_put, s_scr, p_scr, *, mq, n_win, min_col):
    lo = _lane_lo(mq)
    units = [(w, p) for w in range(n_win) for p in range(N_PAIRS)]
    for u, (w, p) in enumerate(units):
        qs = q_pair(w, p) * (HEAD_DIM ** -0.5)
        q2 = jnp.concatenate([jnp.where(lo, qs, 0.0), jnp.where(lo, 0.0, qs)], axis=0).astype(BF16)
        s = lax.dot_general(q2, k_pair(w, p), (((1,), (1,)), ((), ())), preferred_element_type=F32) + bias_ref[p]
        first = min_col(w)
        if first is not None:
            col = lax.broadcasted_iota(jnp.int32, (1, KEY_WIN), 1)
            s = jnp.where(col >= first, s, NEG_BIG)
        s_scr[u] = s
    inv_l = []
    rb = min(2 * mq, ATTN_SOFTMAX_ROWS)
    for u in range(len(units)):
        parts = []
        for r0 in range(0, 2 * mq, rb):
            s = s_scr[u, r0:r0 + rb, :]
            e = jnp.exp(s - jnp.max(s, axis=-1, keepdims=True))
            parts.append(1.0 / jnp.sum(e, axis=-1, keepdims=True))
            p_scr[u, r0:r0 + rb, :] = e.astype(BF16)
        inv_l.append(jnp.concatenate(parts, axis=0))
    for u, (w, p) in enumerate(units):
        pv = jnp.dot(p_scr[u], v_pair(w, p), preferred_element_type=F32) * inv_l[u]
        o_put(w, p, jnp.where(lo, pv[:mq], pv[mq:]))


def _attn_prompt_kernel(q_ref, kc_ref, vc_ref, bias_ref, o_ref, kbuf, vbuf, s_scr, p_scr):
    i = pl.program_id(0)
    prev0, cur0, end = CHUNK, CHUNK + ATTN_QB, CHUNK + 2 * ATTN_QB

    @pl.when(i == 0)
    def _():
        zeros = jnp.zeros((cur0, ATTN_WIDTH), BF16)
        kbuf[0:cur0, :] = zeros
        vbuf[0:cur0, :] = zeros

    kbuf[cur0:end, :] = kc_ref[...].astype(BF16)
    vbuf[cur0:end, :] = vc_ref[...].astype(BF16)
    chunks_per_step = ATTN_QB // CHUNK
    lead_chunks = N_PAST_CHUNKS + 1
    n_win = ATTN_CHUNKS_PER_ITER

    def iter_body(j, carry, *, masked):
        sl = lambda p: slice(p * PAIR_W, (p + 1) * PAIR_W)
        r0 = lambda w: pl.multiple_of((j * n_win + w) * CHUNK, CHUNK)
        min_col = lambda w: (lead_chunks - (i * chunks_per_step + j * n_win + w)) * CHUNK if masked else None

        def o_put(w, p, o):
            o_ref[pl.ds(r0(w), CHUNK), sl(p)] = o.astype(BF16)

        _attend(lambda w, p: q_ref[pl.ds(r0(w), CHUNK), sl(p)], lambda w, p: kbuf[pl.ds(r0(w), KEY_WIN), sl(p)],
                lambda w, p: vbuf[pl.ds(r0(w), KEY_WIN), sl(p)], bias_ref, o_put, s_scr, p_scr, mq=CHUNK,
                n_win=n_win, min_col=min_col)
        return carry

    masked_steps = -(-lead_chunks // chunks_per_step)

    @pl.when(i < masked_steps)
    def _():
        lax.fori_loop(0, chunks_per_step // n_win, functools.partial(iter_body, masked=True), 0)

    @pl.when(i >= masked_steps)
    def _():
        lax.fori_loop(0, chunks_per_step // n_win, functools.partial(iter_body, masked=False), 0)

    kbuf[prev0:cur0, :] = kbuf[cur0:end, :]
    vbuf[prev0:cur0, :] = vbuf[cur0:end, :]


def _attn_prompt(z_attn, bias):
    t = z_attn.shape[0]
    assert t % ATTN_QB == 0 and (ATTN_QB // CHUNK) % ATTN_CHUNKS_PER_ITER == 0 and ATTN_QB >= BAND_PAST
    blk = (ATTN_QB, ATTN_WIDTH)
    buf_rows = CHUNK + 2 * ATTN_QB
    s_shape = (ATTN_CHUNKS_PER_ITER * N_PAIRS, 2 * CHUNK, KEY_WIN)
    blocks = 3 * _nbytes(blk, F32) + _nbytes(bias.shape, F32) + _nbytes(blk, BF16)
    scratch = 2 * _nbytes((buf_rows, ATTN_WIDTH), BF16) + _nbytes(s_shape, F32) + _nbytes(s_shape, BF16)
    return pl.pallas_call(
        _attn_prompt_kernel,
        out_shape=jax.ShapeDtypeStruct((t, ATTN_WIDTH), BF16),
        grid=(t // ATTN_QB,),
        in_specs=[pl.BlockSpec(blk, lambda i: (i, 0)),
                  pl.BlockSpec(blk, lambda i: (i, 1)),
                  pl.BlockSpec(blk, lambda i: (i, 2)),
                  pl.BlockSpec(bias.shape, lambda i: (0, 0, 0))],
        out_specs=pl.BlockSpec(blk, lambda i: (i, 0)),
        scratch_shapes=[pltpu.VMEM((buf_rows, ATTN_WIDTH), BF16), pltpu.VMEM((buf_rows, ATTN_WIDTH), BF16),
                        pltpu.VMEM(s_shape, F32), pltpu.VMEM(s_shape, BF16)],
        compiler_params=_params(1, _vmem_limit(blocks, scratch, 4 * _nbytes((2 * CHUNK, KEY_WIN), F32))),
        name="attn_prompt",
    )(z_attn, z_attn, z_attn, bias)


def _attn_sample_kernel(q_ref, kn_ref, vn_ref, kc_ref, vc_ref, bias_ref, o_ref, kbuf, vbuf, s_scr, p_scr, *,
                        s, past):
    tail = jnp.zeros((KEY_WIN - past - s, ATTN_WIDTH), BF16)
    kbuf[0:past, :] = kc_ref[...].astype(BF16)
    vbuf[0:past, :] = vc_ref[...].astype(BF16)
    kbuf[past:past + s, :] = kn_ref[...].astype(BF16)
    vbuf[past:past + s, :] = vn_ref[...].astype(BF16)
    kbuf[past + s:KEY_WIN, :] = tail
    vbuf[past + s:KEY_WIN, :] = tail
    sl = lambda p: slice(p * PAIR_W, (p + 1) * PAIR_W)

    def o_put(w, p, o):
        o_ref[:, sl(p)] = o.astype(BF16)

    _attend(lambda w, p: q_ref[:, sl(p)], lambda w, p: kbuf[:, sl(p)], lambda w, p: vbuf[:, sl(p)], bias_ref,
            o_put, s_scr, p_scr, mq=s, n_win=1, min_col=lambda w: None)


def _attn_sample(z_attn, k_cache, v_cache, bias, *, nb, s):
    past = k_cache.shape[1]
    assert past + s <= KEY_WIN and z_attn.shape[0] == nb * s
    blk = (s, ATTN_WIDTH)
    cblk = (None, past, ATTN_WIDTH)
    s_shape = (N_PAIRS, 2 * s, KEY_WIN)
    blocks = (3 * _nbytes(blk, F32) + 2 * _nbytes((past, ATTN_WIDTH), F32) + _nbytes(bias.shape, F32)
              + _nbytes(blk, BF16))
    scratch = 2 * _nbytes((KEY_WIN, ATTN_WIDTH), BF16) + _nbytes(s_shape, F32) + _nbytes(s_shape, BF16)
    return pl.pallas_call(
        functools.partial(_attn_sample_kernel, s=s, past=past),
        out_shape=jax.ShapeDtypeStruct((nb * s, ATTN_WIDTH), BF16),
        grid=(nb,),
        in_specs=[pl.BlockSpec(blk, lambda b: (b, 0)),
                  pl.BlockSpec(blk, lambda b: (b, 1)),
                  pl.BlockSpec(blk, lambda b: (b, 2)),
                  pl.BlockSpec(cblk, lambda b: (b, 0, 0)),
                  pl.BlockSpec(cblk, lambda b: (b, 0, 0)),
                  pl.BlockSpec(bias.shape, lambda b: (0, 0, 0))],
        out_specs=pl.BlockSpec(blk, lambda b: (b, 0)),
        scratch_shapes=[pltpu.VMEM((KEY_WIN, ATTN_WIDTH), BF16), pltpu.VMEM((KEY_WIN, ATTN_WIDTH), BF16),
                        pltpu.VMEM(s_shape, F32), pltpu.VMEM(s_shape, BF16)],
        compiler_params=_params(1, _vmem_limit(blocks, scratch, 4 * _nbytes((2 * s, KEY_WIN), F32))),
        name="attn_sample",
    )(z_attn, z_attn, z_attn, k_cache, v_cache, bias)


def _bf(x):
    return x.astype(BF16)


def _split2(x):
    hi = _bf(x)
    return hi, _bf(x - hi.astype(F32))


_NN = (((1,), (0,)), ((), ()))
_NT = (((1,), (1,)), ((), ()))


def _mm(x, y, dims=_NN):
    return lax.dot_general(x, y, dims, preferred_element_type=F32)


def _bd(y, lo):
    zero = jnp.zeros_like(y)
    return jnp.concatenate([jnp.where(lo, y, zero), jnp.where(lo, zero, y)], axis=0)


def _seg_sum(x, ones_seg, *, passes):
    r = x.shape[0]
    half = x.shape[1] // (2 * PAIR_W)
    slab = lambda p: x[:, p * PAIR_W:(p + 1) * PAIR_W]
    xx = jnp.concatenate([jnp.concatenate([slab(p) for p in range(half)], axis=0),
                          jnp.concatenate([slab(half + p) for p in range(half)], axis=0)], axis=1)
    hi, lo = _split2(xx)
    s = _mm(hi, ones_seg)
    if passes == 2:
        s = s + _mm(lo, ones_seg)
    return jnp.concatenate([s[p * r:(p + 1) * r, 0:PAIR_W] for p in range(half)]
                           + [s[p * r:(p + 1) * r, PAIR_W:2 * PAIR_W] for p in range(half)], axis=1)


def _sigmoid(x):
    return 1.0 / (1.0 + jnp.exp(-x))


def _sigmoid_tanh(x):
    return 0.5 + 0.5 * jnp.tanh(0.5 * x)


_TOK_F32 = ("at", "rt", "v", "bonus", "g")
_TOK_BF16 = ("at", "bt", "kt", "rt", "v", "bp", "kp")


_RWKV_N_IN = 14
_RWKV_N_OUT = 2
_RWKV_N_SCRATCH = 6


def _rwkv_kernel(*refs, rows, cps, nck, cast_periods):
    n_cast = len(cast_periods)
    (zf_ref, zn_ref, sh_ref, s0_ref, mu_ref, w0_ref, w2cat_ref, a0_ref, gw2_ref, kk_ref, ka_ref, rk_ref, lnw_ref,
     lnb_ref) = refs[:_RWKV_N_IN]
    cast_in = refs[_RWKV_N_IN:_RWKV_N_IN + n_cast]
    o_ref, sfin_ref = refs[_RWKV_N_IN + n_cast:_RWKV_N_IN + n_cast + _RWKV_N_OUT]
    cast_out = refs[_RWKV_N_IN + n_cast + _RWKV_N_OUT:_RWKV_N_IN + 2 * n_cast + _RWKV_N_OUT]
    s_scr, carry_scr, y_scr, f32_scr, bf_scr, pc_scr = refs[-_RWKV_N_SCRATCH:]
    ct = CHUNK_T
    pw = PAIR_W
    tt = cps * ct
    assert rows == ct or cps == 1
    c = pl.program_id(1)

    seg = 2 * pw
    ones_seg = _bf(lax.broadcasted_iota(jnp.int32, (seg, seg), 0) // HEAD_DIM
                   == lax.broadcasted_iota(jnp.int32, (seg, seg), 1) // HEAD_DIM)
    row = lax.broadcasted_iota(jnp.int32, (ct, 1), 0)
    row_all = lax.broadcasted_iota(jnp.int32, (tt, 1), 0)
    tri = _bf(lax.broadcasted_iota(jnp.int32, (ct, ct), 1) <= lax.broadcasted_iota(jnp.int32, (ct, ct), 0))
    t_lane = lax.broadcasted_iota(jnp.int32, (ct, 2 * ct), 1) % ct
    strict = t_lane < row
    incl = t_lane <= row
    lo_t = _lane_lo(ct)
    lo_p = lax.broadcasted_iota(jnp.int32, (ct, 2 * ct), 1) < ct
    lo_t2 = lax.broadcasted_iota(jnp.int32, (ct, 2 * pw), 1) % pw < HEAD_DIM
    diag = (lax.broadcasted_iota(jnp.int32, (HEAD_DIM, pw), 1) % HEAD_DIM
            == lax.broadcasted_iota(jnp.int32, (HEAD_DIM, pw), 0))
    lo_s = _lane_lo(HEAD_DIM)
    n_dbl = int(math.log2(ct))
    assert 1 << n_dbl == ct and 2 * ct == pw
    pairs = range(N_PAIRS)
    cat0 = lambda xs: jnp.concatenate(xs, axis=0)
    cat1 = lambda xs: jnp.concatenate(xs, axis=1)

    def tn(xv, yv):
        zz = _mm(_bf(xv.T), yv)
        return jnp.where(lo_s, zz[:HEAD_DIM], zz[HEAD_DIM:])

    def chunk_pre(zc):
        r = zc[:, 0:RWKV_WIDTH]
        k = zc[:, RWKV_WIDTH:2 * RWKV_WIDTH]
        v = zc[:, 2 * RWKV_WIDTH:3 * RWKV_WIDTH]
        zwa = zc[:, 3 * RWKV_WIDTH:3 * RWKV_WIDTH + DECAY_RANK + AAA_RANK]
        zg = zc[:, 3 * RWKV_WIDTH + DECAY_RANK + AAA_RANK:SHIFT_W]
        lora_in = cat0([jnp.where(lo_t, jnp.tanh(zwa), 0.0), jnp.where(lo_t, 0.0, zwa)])
        lora = _mm(_bf(lora_in), w2cat_ref[...])
        yield
        lw = -math.exp(-0.5) * _sigmoid_tanh(w0_ref[...] + lora[:ct])
        if rows < ct:
            live = row < rows
            lw = jnp.where(live, lw, 0.0)
        l_hi, l_lo = _split2(lw)
        cl = _mm(tri, l_hi) + _mm(tri, l_lo)
        yield
        a = _sigmoid_tanh(a0_ref[...] + lora[ct:])
        g = _mm(_bf(_sigmoid_tanh(zg)), gw2_ref[...])
        yield
        kkr = k * kk_ref[...]
        kn = kkr * lax.rsqrt(jnp.maximum(_seg_sum(kkr * kkr, ones_seg, passes=1), 1e-24))
        yield
        k2 = k * (1.0 + (a - 1.0) * ka_ref[...])
        bonus = _seg_sum(r * k2 * rk_ref[...], ones_seg, passes=1) * v
        yield
        e_in = jnp.exp(cl)
        rt = r * e_in
        pc = e_in[ct - 1:ct, :]
        yield
        at = -kn * jnp.exp(cl - lw)
        yield
        e_inv = jnp.exp(-cl)
        bt = kn * a * e_inv
        kt = k2 * e_inv
        if rows < ct:
            bt = jnp.where(live, bt, 0.0)
            kt = jnp.where(live, kt, 0.0)
        yield
        return dict(at=at, bt=bt, kt=kt, rt=rt, v=v, bp=bt * pc, kp=kt * pc, pc=pc, bonus=bonus, g=g)

    def pre_block(z_ref, carry, wr):
        z = z_ref[...]
        if rows < ct:
            z = jnp.concatenate([z, jnp.zeros((ct - rows, SHIFT_W), F32)], axis=0)
        zprev = jnp.where(row_all == 0, carry, pltpu.roll(z, 1, axis=0))
        last = (cps - 1) * ct + rows - 1
        carry_scr[0:1, :] = z[last:last + 1, :]
        zs = z + (zprev - z) * mu_ref[...]
        yield
        for gi in range(cps):
            t = yield from chunk_pre(zs[gi * ct:(gi + 1) * ct])
            rs = slice(gi * ct, (gi + 1) * ct)
            for i, name in enumerate(_TOK_F32):
                f32_scr[wr, i, rs, :] = t[name]
            yield
            for i, name in enumerate(_TOK_BF16):
                bf_scr[wr, i, rs, :] = _bf(t[name])
            pc_scr[wr, gi:gi + 1, :] = t["pc"]
            yield

    def chunk_stages(rd, gi):
        rs = slice(gi * ct, (gi + 1) * ct)
        f32 = lambda name, p: f32_scr[rd, _TOK_F32.index(name), rs, p * pw:(p + 1) * pw]
        b16 = lambda name, p: bf_scr[rd, _TOK_BF16.index(name), rs, p * pw:(p + 1) * pw]
        v_bd = [_bd(b16("v", p), lo_t) for p in pairs]
        amat = [_mm(cat0([b16("at", p), b16("rt", p)]), cat0([_bd(b16("bt", p), lo_t), _bd(b16("kt", p), lo_t)]),
                    _NT) for p in pairs]
        yield
        a_ab = [jnp.where(strict, amat[p][:ct, :pw], 0.0) for p in pairs]
        a_ak = [_bf(jnp.where(strict, amat[p][:ct, pw:], 0.0)) for p in pairs]
        a_rb = [_bf(jnp.where(incl, amat[p][ct:, :pw], 0.0)) for p in pairs]
        a_rk = [_bf(jnp.where(incl, amat[p][ct:, pw:], 0.0)) for p in pairs]
        x = [cat1([f32("at", p), _mm(a_ak[p], v_bd[p])]) for p in pairs]
        npow = [_bf(a_ab[p]) for p in pairs]
        yield
        for it in range(n_dbl):
            if it + 1 < n_dbl:
                res = [_mm(npow[p], cat1([_bd(_bf(x[p]), lo_t2), _bd(npow[p], lo_p)])) for p in pairs]
                npow = [_bf(res[p][:, 2 * pw:]) for p in pairs]
            else:
                res = [_mm(npow[p], _bd(_bf(x[p]), lo_t2)) for p in pairs]
            x = [x[p] + res[p][:, :2 * pw] for p in pairs]
            yield
        ry = [_mm(a_rb[p], _bd(_bf(x[p]), lo_t2)) for p in pairs]
        rp = [_bf(f32("rt", p) + ry[p][:, :pw]) for p in pairs]
        y0 = [ry[p][:, pw:] + _mm(a_rk[p], v_bd[p]) for p in pairs]
        yield
        m_mat = [_bf(jnp.where(diag, pc_scr[rd, gi:gi + 1, p * pw:(p + 1) * pw], 0.0)
                     + tn(x[p][:, :pw], b16("bp", p))) for p in pairs]
        yield
        n_mat = [tn(cat0([x[p][:, pw:], f32("v", p)]), cat0([b16("bp", p), b16("kp", p)])) for p in pairs]
        return rp, y0, m_mat, n_mat

    def lockstep(*gens):
        done = [None] * len(gens)
        live = list(range(len(gens)))
        while live:
            for i in list(live):
                try:
                    next(gens[i])
                except StopIteration as stop:
                    done[i] = stop.value
                    live.remove(i)
            yield
        return done

    def main_block(rd):
        terms = yield from lockstep(*[chunk_stages(rd, gi) for gi in range(cps)])
        state = [s_scr[:, p * pw:(p + 1) * pw] for p in pairs]
        for gi, (rp, y0, m_mat, n_mat) in enumerate(terms):
            for p in pairs:
                s_old = _bf(state[p])
                y_scr[gi * ct:(gi + 1) * ct, p * pw:(p + 1) * pw] = _mm(rp[p], _bd(s_old, lo_s), _NT) + y0[p]
                state[p] = _mm(s_old, _bd(m_mat[p], lo_s)) + n_mat[p]
            yield
        for p in pairs:
            s_scr[:, p * pw:(p + 1) * pw] = state[p]
        y = y_scr[...]
        mean = _seg_sum(y, ones_seg, passes=2) * (1.0 / HEAD_DIM)
        dv = y - mean
        yield
        var = _seg_sum(dv * dv, ones_seg, passes=1) * (1.0 / HEAD_DIM)
        yn = dv * lax.rsqrt(var + GN_EPS) * lnw_ref[...] + lnb_ref[...]
        out = (yn + f32_scr[rd, _TOK_F32.index("bonus")]) * f32_scr[rd, _TOK_F32.index("g")]
        o_ref[...] = out[:(cps - 1) * ct + rows].astype(BF16)

    def cast_block(parity):
        for src, dst, period in zip(cast_in, cast_out, cast_periods):
            if parity % period == 0:
                dst[...] = _bf(src[...])
            yield

    def run(*gens):
        for _ in lockstep(*gens):
            pass

    @pl.when(c == 0)
    def _():
        for p in pairs:
            s_scr[:, p * pw:(p + 1) * pw] = jnp.concatenate([s0_ref[2 * p], s0_ref[2 * p + 1]], axis=1)
        run(pre_block(zf_ref, sh_ref[...], 0))

    if nck == 1:
        assert n_cast == 0
        run(main_block(0))
    else:
        for parity in (0, 1):
            @pl.when(c % 2 == parity)
            def _(parity=parity):
                run(main_block(parity), pre_block(zn_ref, carry_scr[0:1, :], 1 - parity), cast_block(parity))

    @pl.when(c == nck - 1)
    def _():
        for h in range(N_HEADS):
            sfin_ref[h] = s_scr[:, h * HEAD_DIM:(h + 1) * HEAD_DIM]


def _rwkv(z_rwkv, shift0, s0, lp, *, nb, seq, name, cast_weights=()):
    rows = min(seq, CHUNK_T)
    cps = min(seq // rows, RWKV_CHUNKS_PER_STEP)
    blk_rows = cps * rows
    assert seq % blk_rows == 0 and z_rwkv.shape == (nb * seq, SHIFT_W)
    nck = seq // blk_rows
    tt = cps * CHUNK_T
    row1 = lambda n: pl.BlockSpec((1, n), lambda b, c: (0, 0))
    full = lambda r, n: pl.BlockSpec((r, n), lambda b, c: (0, 0))
    st_blk = pl.BlockSpec((None, N_HEADS, HEAD_DIM, HEAD_DIM), lambda b, c: (b, 0, 0, 0))
    lora_rows = DECAY_RANK + AAA_RANK
    blocks = (2 * _nbytes((blk_rows, SHIFT_W), F32) + 2 * _nbytes((HEAD_DIM, RWKV_WIDTH), F32)
              + _nbytes((lora_rows + GATE_RANK, RWKV_WIDTH), BF16) + _nbytes((16, SHIFT_W), F32)
              + _nbytes((blk_rows, RWKV_WIDTH), BF16))
    stage_shapes = [((2, len(_TOK_F32), tt, RWKV_WIDTH), F32), ((2, len(_TOK_BF16), tt, RWKV_WIDTH), BF16),
                    ((2, 8, RWKV_WIDTH), F32)]
    scratch = (_nbytes((HEAD_DIM, RWKV_WIDTH), F32) + _nbytes((8, SHIFT_W), F32)
               + _nbytes((tt, RWKV_WIDTH), F32) + sum(_nbytes(s, d) for s, d in stage_shapes))
    assert cps <= 8
    bf16_rows = 16
    cast_periods, cast_specs, cast_shapes = [], [], []
    for w in cast_weights:
        assert nb == 1 and nck % 2 == 0
        period = next(p for p in (1, 2) if w.shape[0] % (nck // p) == 0 and (w.shape[0] // (nck // p)) % bf16_rows == 0)
        slab = (w.shape[0] // (nck // period), w.shape[1])
        cast_periods.append(period)
        cast_specs.append(pl.BlockSpec(slab, lambda b, c, period=period: (c // period, 0)))
        cast_shapes.append(jax.ShapeDtypeStruct(w.shape, BF16))
        blocks += _nbytes(slab, F32) + _nbytes(slab, BF16)
    outs = pl.pallas_call(
        functools.partial(_rwkv_kernel, rows=rows, cps=cps, nck=nck, cast_periods=tuple(cast_periods)),
        out_shape=(jax.ShapeDtypeStruct((nb * seq, RWKV_WIDTH), BF16),
                   jax.ShapeDtypeStruct((nb, N_HEADS, HEAD_DIM, HEAD_DIM), F32), *cast_shapes),
        grid=(nb, nck),
        in_specs=[pl.BlockSpec((blk_rows, SHIFT_W), lambda b, c: (b * nck, 0)),
                  pl.BlockSpec((blk_rows, SHIFT_W), lambda b, c: (b * nck + jnp.minimum(c + 1, nck - 1), 0)),
                  pl.BlockSpec((None, 1, SHIFT_W), lambda b, c: (b, 0, 0)),
                  st_blk,
                  row1(SHIFT_W), row1(RWKV_WIDTH), full(lora_rows, RWKV_WIDTH), row1(RWKV_WIDTH),
                  full(GATE_RANK, RWKV_WIDTH), row1(RWKV_WIDTH),
                  row1(RWKV_WIDTH), row1(RWKV_WIDTH), row1(RWKV_WIDTH), row1(RWKV_WIDTH), *cast_specs],
        out_specs=(pl.BlockSpec((blk_rows, RWKV_WIDTH), lambda b, c: (b * nck + c, 0)), st_blk, *cast_specs),
        scratch_shapes=[pltpu.VMEM((HEAD_DIM, RWKV_WIDTH), F32), pltpu.VMEM((8, SHIFT_W), F32),
                        pltpu.VMEM((tt, RWKV_WIDTH), F32)] + [pltpu.VMEM(s, d) for s, d in stage_shapes],
        compiler_params=_params(2, _vmem_limit(blocks, scratch, 64 * _nbytes((tt, RWKV_WIDTH), F32))),
        name=name,
    )(z_rwkv, z_rwkv, shift0, s0, lp["mu"], lp["w0"], lp["w2cat"], lp["a0"], lp["g_w2"],
      lp["k_k"], lp["k_a"], lp["r_k"], lp["lnx_w"], lp["lnx_b"], *cast_weights)
    return outs[0], outs[1], list(outs[2:])


def _oproj_kernel(a_ref, r_ref, wa_ref, wr_ref, x_ref, g_ref, h_ref, n_ref):
    h = (x_ref[...] + jnp.dot(a_ref[...], wa_ref[...], preferred_element_type=F32)
         + jnp.dot(r_ref[...], wr_ref[...], preferred_element_type=F32))
    h_ref[...] = h
    n_ref[...] = _rms(h, g_ref[...]).astype(BF16)


def _oproj(attn, rwkv, w_o, x, g, *, tm, name):
    t = x.shape[0]
    assert t % tm == 0
    half = (ATTN_WIDTH, D_MODEL)
    blocks = (2 * _nbytes((tm, ATTN_WIDTH), BF16) + 2 * _nbytes(half, BF16) + 2 * _nbytes((tm, D_MODEL), F32)
              + _nbytes((tm, D_MODEL), BF16))
    return pl.pallas_call(
        _oproj_kernel,
        out_shape=(jax.ShapeDtypeStruct((t, D_MODEL), F32), jax.ShapeDtypeStruct((t, D_MODEL), BF16)),
        grid=(t // tm,),
        in_specs=[pl.BlockSpec((tm, ATTN_WIDTH), lambda i: (i, 0)),
                  pl.BlockSpec((tm, RWKV_WIDTH), lambda i: (i, 0)),
                  pl.BlockSpec(half, lambda i: (0, 0)),
                  pl.BlockSpec(half, lambda i: (1, 0)),
                  pl.BlockSpec((tm, D_MODEL), lambda i: (i, 0)),
                  pl.BlockSpec((1, D_MODEL), lambda i: (0, 0))],
        out_specs=(pl.BlockSpec((tm, D_MODEL), lambda i: (i, 0)), pl.BlockSpec((tm, D_MODEL), lambda i: (i, 0))),
        compiler_params=_params(1, _vmem_limit(blocks, 0, 2 * _nbytes((tm, D_MODEL), F32))),
        name=name,
    )(attn, rwkv, w_o, w_o, x, g)


def _ffn_kernel(n_ref, h_ref, wg_ref, wu_ref, wd_ref, g_ref, o_ref):
    j = pl.program_id(1)

    @pl.when(j == 0)
    def _():
        o_ref[...] = h_ref[...]

    n = n_ref[...]
    gate = jnp.dot(n, wg_ref[...], preferred_element_type=F32)
    up = jnp.dot(n, wu_ref[...], preferred_element_type=F32)
    act = (gate * _sigmoid(gate) * up).astype(BF16)
    o_ref[...] += jnp.dot(act, wd_ref[...], preferred_element_type=F32)

    @pl.when(j == pl.num_programs(1) - 1)
    def _():
        o_ref[...] = _rms(o_ref[...], g_ref[...])


def _ffn(n2, h, wg, wu, wd, g, *, tm, tf, name):
    t = h.shape[0]
    assert t % tm == 0 and D_FF % tf == 0
    blocks = (_nbytes((tm, D_MODEL), BF16) + 2 * _nbytes((tm, D_MODEL), F32) + 3 * _nbytes((D_MODEL, tf), BF16))
    temps = 3 * _nbytes((tm, tf), F32) + _nbytes((tm, tf), BF16)
    return pl.pallas_call(
        _ffn_kernel,
        out_shape=jax.ShapeDtypeStruct((t, D_MODEL), F32),
        grid=(t // tm, D_FF // tf),
        in_specs=[pl.BlockSpec((tm, D_MODEL), lambda i, j: (i, 0)),
                  pl.BlockSpec((tm, D_MODEL), lambda i, j: (i, 0)),
                  pl.BlockSpec((D_MODEL, tf), lambda i, j: (0, j)),
                  pl.BlockSpec((D_MODEL, tf), lambda i, j: (0, j)),
                  pl.BlockSpec((tf, D_MODEL), lambda i, j: (j, 0)),
                  pl.BlockSpec((1, D_MODEL), lambda i, j: (0, 0))],
        out_specs=pl.BlockSpec((tm, D_MODEL), lambda i, j: (i, 0)),
        compiler_params=_params(2, _vmem_limit(blocks, 0, temps)),
        name=name,
    )(n2, h, wg, wu, wd, g)


_LATE_WEIGHTS = ("w_o", "w_gate", "w_up", "w_down")


def _project(x, lp, wts, *, tag):
    tm = min(x.shape[0], 1024)
    source = lambda key, col0: (wts[key], 0) if key in wts else (wts["w_in"], col0)
    w, col0 = source("w_attn", 0)
    z_attn, nx, w_bf = _inproj(x, lp["ln1"], w, col0=col0, n=3 * ATTN_WIDTH, tm=tm, tn=1024,
                               name=f"inproj_attn_{tag}")
    if w_bf is not None:
        wts["w_attn"] = w_bf
    w, col0 = source("w_rwkv", 3 * ATTN_WIDTH)
    z_rwkv, _, w_bf = _inproj(nx, None, w, col0=col0, n=SHIFT_W, tm=tm, tn=SHIFT_W // 2, name=f"inproj_rwkv_{tag}")
    if w_bf is not None:
        wts["w_rwkv"] = w_bf
    return z_attn, z_rwkv


def _mix(z_attn, z_rwkv, lp, wts, attn_fn, shift0, s0, *, nb, seq, tag):
    attn = attn_fn(z_attn)
    convert = seq > RWKV_CHUNKS_PER_STEP * CHUNK_T
    pending = [k for k in _LATE_WEIGHTS if wts[k].dtype != BF16] if convert else []
    rwkv, s_fin, converted = _rwkv(z_rwkv, shift0, s0, lp, nb=nb, seq=seq, name=f"rwkv_{tag}",
                                   cast_weights=[wts[k] for k in pending])
    wts.update(zip(pending, converted))
    return attn, rwkv, s_fin


def _mlp(x, attn, rwkv, lp, wts, *, tag):
    t = x.shape[0]
    h, n2 = _oproj(attn, rwkv, wts["w_o"], x, lp["ln2"], tm=min(t, 512), name=f"oproj_{tag}")
    return _ffn(n2, h, wts["w_gate"], wts["w_up"], wts["w_down"], lp["final_norm"], tm=min(t, 1024), tf=512,
                name=f"ffn_{tag}")


def kernel(x_prompt, x_sample, cache_attn_k, cache_attn_v, state_rwkv_wkv, state_rwkv_shift, ln1, w_in, rel_table,
           mu, w0, w_w2, a0, a_w2, g_w2, k_k, k_a, r_k, lnx_w, lnx_b, w_o, ln2, w_gate, w_up, w_down, final_norm):
    depth = w_in.shape[0]
    assert depth == 1, "final RMSNorm is fused into the (single) layer's FFN call"
    b, t, _ = x_prompt.shape
    bd, s, _ = x_sample.shape
    past = cache_attn_k.shape[2]
    n_keep = min(BAND_PAST, t)
    l = 0
    row = lambda p: p[l].reshape(1, -1)
    lp = dict(ln1=row(ln1), mu=row(mu), w0=row(w0), a0=row(a0),
              w2cat=jnp.concatenate([w_w2[l], a_w2[l]], axis=0).astype(BF16), g_w2=g_w2[l].astype(BF16),
              k_k=row(k_k), k_a=row(k_a), r_k=row(r_k), lnx_w=row(lnx_w), lnx_b=row(lnx_b), ln2=row(ln2),
              final_norm=final_norm.reshape(1, -1))
    wts = dict(w_in=w_in[l], w_o=w_o[l], w_gate=w_gate[l], w_up=w_up[l], w_down=w_down[l])
    table = rel_table[l]

    bias_p = _bias_tiles(table, mq=CHUNK, off=BAND_PAST + CHUNK, col_lo=CHUNK, col_hi=KEY_WIN, name="bias_prompt")
    bias_s = _bias_tiles(table, mq=s, off=past, col_lo=0, col_hi=past + s, name="bias_sample")

    assert b == 1
    xp = x_prompt.reshape(b * t, D_MODEL)
    xs = x_sample.reshape(bd * s, D_MODEL)
    kc = cache_attn_k[l].reshape(bd, past, ATTN_WIDTH)
    vc = cache_attn_v[l].reshape(bd, past, ATTN_WIDTH)
    zas, zrs = _project(xs, lp, wts, tag="s")
    zap, zrp = _project(xp, lp, wts, tag="p")
    attn_p, rwkv_p, wkv_p = _mix(zap, zrp, lp, wts, lambda z: _attn_prompt(z, bias_p),
                                 jnp.zeros((b, 1, SHIFT_W), F32), jnp.zeros((b, N_HEADS, HEAD_DIM, HEAD_DIM), F32),
                                 nb=b, seq=t, tag="p")
    attn_s, rwkv_s, wkv_s = _mix(zas, zrs, lp, wts, lambda z: _attn_sample(z, kc, vc, bias_s, nb=bd, s=s),
                                 state_rwkv_shift[l], state_rwkv_wkv[l], nb=bd, seq=s, tag="s")
    yp = _mlp(xp, attn_p, rwkv_p, lp, wts, tag="p")
    ys = _mlp(xs, attn_s, rwkv_s, lp, wts, tag="s")

    heads = lambda z, n, tt: z.reshape(n, tt, N_HEADS, HEAD_DIM)
    zap = zap.reshape(b, t, 3 * ATTN_WIDTH)
    zas = zas.reshape(bd, s, 3 * ATTN_WIDTH)
    k_p = heads(zap[:, t - n_keep:, ATTN_WIDTH:2 * ATTN_WIDTH], b, n_keep)
    v_p = heads(zap[:, t - n_keep:, 2 * ATTN_WIDTH:], b, n_keep)
    k_s = heads(zas[:, :, ATTN_WIDTH:2 * ATTN_WIDTH], bd, s)
    v_s = heads(zas[:, :, 2 * ATTN_WIDTH:], bd, s)
    sh_p = zrp.reshape(b, t, SHIFT_W)[:, t - 1:]
    sh_s = zrs.reshape(bd, s, SHIFT_W)[:, s - 1:]
    return (yp.reshape(b, t, D_MODEL), ys.reshape(bd, s, D_MODEL),
            k_p[None], v_p[None], wkv_p[None], sh_p[None],
            k_s[None], v_s[None], wkv_s[None], sh_s[None])
```

```python
import functools
import math

import jax
import jax.numpy as jnp
from jax import lax
from jax.experimental import pallas as pl
from jax.experimental.pallas import tpu as pltpu

F32 = jnp.float32
BF16 = jnp.bfloat16

D_MODEL = 2048
CHUNK = 64
N_PAST_CHUNKS = 8
BAND_PAST = N_PAST_CHUNKS * CHUNK
ATTN_WIDTH = 1024
HEAD_DIM = 64
N_HEADS = 16
REL_CLIP = 128
N_REL = 2 * REL_CLIP + 1
RWKV_WIDTH = 1024
DECAY_RANK = 64
AAA_RANK = 64
GATE_RANK = 128
SHIFT_W = 3 * RWKV_WIDTH + DECAY_RANK + AAA_RANK + GATE_RANK
D_FF = 5632
RMS_EPS = 1e-6
GN_EPS = 64e-5

V7X_LANES = 128
V7X_VMEM_BYTES = 64 * 1024 * 1024
V7X_SCOPED_VMEM_CAP_BYTES = 60000 * 1024

PAIR_W = 2 * HEAD_DIM
N_PAIRS = N_HEADS // 2
CHUNK_T = 64
KEY_WIN = (N_PAST_CHUNKS + 2) * CHUNK
NEG_BIG = -1e30
ATTN_QB = 512
ATTN_CHUNKS_PER_ITER = 2
ATTN_SOFTMAX_ROWS = 128
RWKV_CHUNKS_PER_STEP = 2

assert PAIR_W == V7X_LANES


def _vmem_limit(pipelined_bytes, scratch_bytes, temp_bytes):
    del pipelined_bytes, scratch_bytes, temp_bytes
    return V7X_SCOPED_VMEM_CAP_BYTES


def _nbytes(shape, dtype):
    return math.prod(shape) * jnp.dtype(dtype).itemsize


def _params(n_grid, vmem, flags=None):
    return pltpu.CompilerParams(dimension_semantics=("arbitrary",) * n_grid, vmem_limit_bytes=vmem, flags=flags)


def _rms(x, g):
    return x * lax.rsqrt(jnp.mean(x * x, axis=-1, keepdims=True) + RMS_EPS) * g


def _inproj_kernel(*refs, normalise, emit_w):
    n_in = 3 if normalise else 2
    w_ref, o_ref, extra = refs[n_in - 1], refs[n_in], list(refs[n_in + 1:])
    if normalise:
        x_ref, g_ref, nx_ref = refs[0], refs[1], extra.pop(0)

        @pl.when(pl.program_id(1) == 0)
        def _():
            nx_ref[...] = _rms(x_ref[...], g_ref[...]).astype(BF16)
    else:
        nx_ref = refs[0]
    w = w_ref[...]
    if emit_w:
        w = w.astype(BF16)
        extra.pop(0)[...] = w
    o_ref[...] = jnp.dot(nx_ref[...], w, preferred_element_type=F32)


def _inproj(x, g, w, *, col0, n, tm, tn, name):
    t, d = x.shape
    assert t % tm == 0 and n % tn == 0 and col0 % V7X_LANES == 0 and tn % V7X_LANES == 0 and col0 + n <= w.shape[1]
    normalise = x.dtype != BF16
    emit_w = w.dtype != BF16
    assert normalise == (g is not None) and (not emit_w or t == tm)
    w_spec = pl.BlockSpec((pl.Element(d), pl.Element(tn)),
                          lambda i, j: (0, (col0 // V7X_LANES + j * (tn // V7X_LANES)) * V7X_LANES))
    row_blk = pl.BlockSpec((tm, d), lambda i, j: (i, 0))
    operands = [x, g, w] if normalise else [x, w]
    in_specs = ([row_blk, pl.BlockSpec((1, d), lambda i, j: (0, 0)), w_spec] if normalise else [row_blk, w_spec])
    out_shape = [jax.ShapeDtypeStruct((t, n), F32)]
    out_specs = [pl.BlockSpec((tm, tn), lambda i, j: (i, j))]
    blocks = _nbytes((tm, d), x.dtype) + _nbytes((d, tn), w.dtype) + _nbytes((tm, tn), F32)
    if normalise:
        out_shape.append(jax.ShapeDtypeStruct((t, d), BF16))
        out_specs.append(row_blk)
        blocks += _nbytes((tm, d), BF16)
    if emit_w:
        out_shape.append(jax.ShapeDtypeStruct((d, n), BF16))
        out_specs.append(pl.BlockSpec((d, tn), lambda i, j: (0, j)))
        blocks += _nbytes((d, tn), BF16)
    outs = list(pl.pallas_call(
        functools.partial(_inproj_kernel, normalise=normalise, emit_w=emit_w),
        out_shape=out_shape, grid=(t // tm, n // tn), in_specs=in_specs, out_specs=out_specs,
        compiler_params=_params(2, _vmem_limit(blocks, 0, 2 * _nbytes((tm, d), F32))), name=name,
    )(*operands))
    z = outs.pop(0)
    nx = outs.pop(0) if normalise else None
    return z, nx, (outs.pop(0) if emit_w else None)


BIAS_VAR_COL0 = KEY_WIN - 2 * V7X_LANES
BIAS_DIAG_W = 512


def _bias_kernel(tab_ref, o_ref, *, mq, off, col_lo, col_hi):
    nvar = KEY_WIN - BIAS_VAR_COL0
    w = BIAS_DIAG_W
    assert nvar + mq <= w
    lane = lax.broadcasted_iota(jnp.int32, (1, w), 1)
    idx = jnp.clip(off - (BIAS_VAR_COL0 - mq + lane), -REL_CLIP, REL_CLIP) + REL_CLIP
    idx_min = max(0, min(REL_CLIP, off - (BIAS_VAR_COL0 - mq + w - 1)) + REL_CLIP)
    heads = range(N_HEADS)

    def body(j, profs):
        hit = idx == j
        return tuple(jnp.where(hit, tab_ref[h, j], prof) for h, prof in zip(heads, profs))

    init = tuple(jnp.full((1, w), tab_ref[h, N_REL - 1], F32) for h in heads)
    profs = lax.fori_loop(idx_min, N_REL - 1, body, init)

    c = lax.broadcasted_iota(jnp.int32, (mq, nvar), 1) + BIAS_VAR_COL0
    valid_var = (c >= col_lo) & (c < col_hi)
    c_far = lax.broadcasted_iota(jnp.int32, (mq, BIAS_VAR_COL0), 1)
    valid_far = (c_far >= col_lo) & (c_far < col_hi)
    for h in heads:
        tile = pltpu.roll(jnp.broadcast_to(profs[h], (mq, w)), w - mq, axis=1, stride=1, stride_axis=0)
        rows = slice((h % 2) * mq, (h % 2 + 1) * mq)
        far = jnp.full((mq, BIAS_VAR_COL0), tab_ref[h, N_REL - 1], F32)
        o_ref[h // 2, rows, 0:BIAS_VAR_COL0] = jnp.where(valid_far, far, NEG_BIG)
        o_ref[h // 2, rows, BIAS_VAR_COL0:KEY_WIN] = jnp.where(valid_var, tile[:, 0:nvar], NEG_BIG)


def _bias_tiles(table, *, mq, off, col_lo, col_hi, name):
    assert off - (BIAS_VAR_COL0 - 1) >= REL_CLIP
    return pl.pallas_call(
        functools.partial(_bias_kernel, mq=mq, off=off, col_lo=col_lo, col_hi=col_hi),
        out_shape=jax.ShapeDtypeStruct((N_PAIRS, 2 * mq, KEY_WIN), F32),
        in_specs=[pl.BlockSpec(memory_space=pltpu.SMEM)],
        out_specs=pl.BlockSpec(memory_space=pltpu.VMEM),
        name=name,
    )(table)


def _lane_lo(rows):
    return lax.broadcasted_iota(jnp.int32, (rows, PAIR_W), 1) < HEAD_DIM


def _attend(q_pair, k_pair, v_pair, bias_ref, o_put, s_scr, p_scr, *, mq, n_win, min_col):
    lo = _lane_lo(mq)
    units = [(w, p) for w in range(n_win) for p in range(N_PAIRS)]
    for u, (w, p) in enumerate(units):
        qs = q_pair(w, p) * (HEAD_DIM ** -0.5)
        q2 = jnp.concatenate([jnp.where(lo, qs, 0.0), jnp.where(lo, 0.0, qs)], axis=0).astype(BF16)
        s = lax.dot_general(q2, k_pair(w, p), (((1,), (1,)), ((), ())), preferred_element_type=F32) + bias_ref[p]
        first = min_col(w)
        if first is not None:
            col = lax.broadcasted_iota(jnp.int32, (1, KEY_WIN), 1)
            s = jnp.where(col >= first, s, NEG_BIG)
        s_scr[u] = s
    inv_l = []
    rb = min(2 * mq, ATTN_SOFTMAX_ROWS)
    for u in range(len(units)):
        parts = []
        for r0 in range(0, 2 * mq, rb):
            s = s_scr[u, r0:r0 + rb, :]
            e = jnp.exp(s - jnp.max(s, axis=-1, keepdims=True))
            parts.append(1.0 / jnp.sum(e, axis=-1, keepdims=True))
            p_scr[u, r0:r0 + rb, :] = e.astype(BF16)
        inv_l.append(jnp.concatenate(parts, axis=0))
    for u, (w, p) in enumerate(units):
        pv = jnp.dot(p_scr[u], v_pair(w, p), preferred_element_type=F32) * inv_l[u]
        o_put(w, p, jnp.where(lo, pv[:mq], pv[mq:]))


def _attn_prompt_kernel(q_ref, kc_ref, vc_ref, bias_ref, o_ref, kbuf, vbuf, s_scr, p_scr):
    i = pl.program_id(0)
    prev0, cur0, end = CHUNK, CHUNK + ATTN_QB, CHUNK + 2 * ATTN_QB

    @pl.when(i == 0)
    def _():
        zeros = jnp.zeros((cur0, ATTN_WIDTH), BF16)
        kbuf[0:cur0, :] = zeros
        vbuf[0:cur0, :] = zeros

    kbuf[cur0:end, :] = kc_ref[...].astype(BF16)
    vbuf[cur0:end, :] = vc_ref[...].astype(BF16)
    chunks_per_step = ATTN_QB // CHUNK
    lead_chunks = N_PAST_CHUNKS + 1
    n_win = ATTN_CHUNKS_PER_ITER

    def iter_body(j, carry, *, masked):
        sl = lambda p: slice(p * PAIR_W, (p + 1) * PAIR_W)
        r0 = lambda w: pl.multiple_of((j * n_win + w) * CHUNK, CHUNK)
        min_col = lambda w: (lead_chunks - (i * chunks_per_step + j * n_win + w)) * CHUNK if masked else None

        def o_put(w, p, o):
            o_ref[pl.ds(r0(w), CHUNK), sl(p)] = o.astype(BF16)

        _attend(lambda w, p: q_ref[pl.ds(r0(w), CHUNK), sl(p)], lambda w, p: kbuf[pl.ds(r0(w), KEY_WIN), sl(p)],
                lambda w, p: vbuf[pl.ds(r0(w), KEY_WIN), sl(p)], bias_ref, o_put, s_scr, p_scr, mq=CHUNK,
                n_win=n_win, min_col=min_col)
        return carry

    masked_steps = -(-lead_chunks // chunks_per_step)

    @pl.when(i < masked_steps)
    def _():
        lax.fori_loop(0, chunks_per_step // n_win, functools.partial(iter_body, masked=True), 0)

    @pl.when(i >= masked_steps)
    def _():
        lax.fori_loop(0, chunks_per_step // n_win, functools.partial(iter_body, masked=False), 0)

    kbuf[prev0:cur0, :] = kbuf[cur0:end, :]
    vbuf[prev0:cur0, :] = vbuf[cur0:end, :]


def _attn_prompt(z_attn, bias):
    t = z_attn.shape[0]
    assert t % ATTN_QB == 0 and (ATTN_QB // CHUNK) % ATTN_CHUNKS_PER_ITER == 0 and ATTN_QB >= BAND_PAST
    blk = (ATTN_QB, ATTN_WIDTH)
    buf_rows = CHUNK + 2 * ATTN_QB
    s_shape = (ATTN_CHUNKS_PER_ITER * N_PAIRS, 2 * CHUNK, KEY_WIN)
    blocks = 3 * _nbytes(blk, F32) + _nbytes(bias.shape, F32) + _nbytes(blk, BF16)
    scratch = 2 * _nbytes((buf_rows, ATTN_WIDTH), BF16) + _nbytes(s_shape, F32) + _nbytes(s_shape, BF16)
    return pl.pallas_call(
        _attn_prompt_kernel,
        out_shape=jax.ShapeDtypeStruct((t, ATTN_WIDTH), BF16),
        grid=(t // ATTN_QB,),
        in_specs=[pl.BlockSpec(blk, lambda i: (i, 0)),
                  pl.BlockSpec(blk, lambda i: (i, 1)),
                  pl.BlockSpec(blk, lambda i: (i, 2)),
                  pl.BlockSpec(bias.shape, lambda i: (0, 0, 0))],
        out_specs=pl.BlockSpec(blk, lambda i: (i, 0)),
        scratch_shapes=[pltpu.VMEM((buf_rows, ATTN_WIDTH), BF16), pltpu.VMEM((buf_rows, ATTN_WIDTH), BF16),
                        pltpu.VMEM(s_shape, F32), pltpu.VMEM(s_shape, BF16)],
        compiler_params=_params(1, _vmem_limit(blocks, scratch, 4 * _nbytes((2 * CHUNK, KEY_WIN), F32))),
        name="attn_prompt",
    )(z_attn, z_attn, z_attn, bias)


def _attn_sample_kernel(q_ref, kn_ref, vn_ref, kc_ref, vc_ref, bias_ref, o_ref, kbuf, vbuf, s_scr, p_scr, *,
                        s, past):
    tail = jnp.zeros((KEY_WIN - past - s, ATTN_WIDTH), BF16)
    kbuf[0:past, :] = kc_ref[...].reshape(past, ATTN_WIDTH).astype(BF16)
    vbuf[0:past, :] = vc_ref[...].reshape(past, ATTN_WIDTH).astype(BF16)
    kbuf[past:past + s, :] = kn_ref[...].astype(BF16)
    vbuf[past:past + s, :] = vn_ref[...].astype(BF16)
    kbuf[past + s:KEY_WIN, :] = tail
    vbuf[past + s:KEY_WIN, :] = tail
    sl = lambda p: slice(p * PAIR_W, (p + 1) * PAIR_W)

    def o_put(w, p, o):
        o_ref[:, sl(p)] = o.astype(BF16)

    _attend(lambda w, p: q_ref[:, sl(p)], lambda w, p: kbuf[:, sl(p)], lambda w, p: vbuf[:, sl(p)], bias_ref,
            o_put, s_scr, p_scr, mq=s, n_win=1, min_col=lambda w: None)


def _attn_sample(z_attn, k_cache, v_cache, bias, *, nb, s):
    past = k_cache.shape[1]
    assert past + s <= KEY_WIN and z_attn.shape[0] == nb * s
    blk = (s, ATTN_WIDTH)
    cblk = (None, past, N_HEADS, HEAD_DIM)
    s_shape = (N_PAIRS, 2 * s, KEY_WIN)
    blocks = (3 * _nbytes(blk, F32) + 2 * _nbytes((past, ATTN_WIDTH), F32) + _nbytes(bias.shape, F32)
              + _nbytes(blk, BF16))
    scratch = 2 * _nbytes((KEY_WIN, ATTN_WIDTH), BF16) + _nbytes(s_shape, F32) + _nbytes(s_shape, BF16)
    return pl.pallas_call(
        functools.partial(_attn_sample_kernel, s=s, past=past),
        out_shape=jax.ShapeDtypeStruct((nb * s, ATTN_WIDTH), BF16),
        grid=(nb,),
        in_specs=[pl.BlockSpec(blk, lambda b: (b, 0)),
                  pl.BlockSpec(blk, lambda b: (b, 1)),
                  pl.BlockSpec(blk, lambda b: (b, 2)),
                  pl.BlockSpec(cblk, lambda b: (b, 0, 0, 0)),
                  pl.BlockSpec(cblk, lambda b: (b, 0, 0, 0)),
                  pl.BlockSpec(bias.shape, lambda b: (0, 0, 0))],
        out_specs=pl.BlockSpec(blk, lambda b: (b, 0)),
        scratch_shapes=[pltpu.VMEM((KEY_WIN, ATTN_WIDTH), BF16), pltpu.VMEM((KEY_WIN, ATTN_WIDTH), BF16),
                        pltpu.VMEM(s_shape, F32), pltpu.VMEM(s_shape, BF16)],
        compiler_params=_params(1, _vmem_limit(blocks, scratch, 4 * _nbytes((2 * s, KEY_WIN), F32))),
        name="attn_sample",
    )(z_attn, z_attn, z_attn, k_cache, v_cache, bias)


def _bf(x):
    return x.astype(BF16)


def _split2(x):
    hi = _bf(x)
    return hi, _bf(x - hi.astype(F32))


_NN = (((1,), (0,)), ((), ()))
_NT = (((1,), (1,)), ((), ()))


def _mm(x, y, dims=_NN):
    return lax.dot_general(x, y, dims, preferred_element_type=F32)


def _bd(y, lo):
    zero = jnp.zeros_like(y)
    return jnp.concatenate([jnp.where(lo, y, zero), jnp.where(lo, zero, y)], axis=0)


def _seg_sum(x, ones_seg, *, passes):
    r = x.shape[0]
    half = x.shape[1] // (2 * PAIR_W)
    slab = lambda p: x[:, p * PAIR_W:(p + 1) * PAIR_W]
    xx = jnp.concatenate([jnp.concatenate([slab(p) for p in range(half)], axis=0),
                          jnp.concatenate([slab(half + p) for p in range(half)], axis=0)], axis=1)
    hi, lo = _split2(xx)
    s = _mm(hi, ones_seg)
    if passes == 2:
        s = s + _mm(lo, ones_seg)
    return jnp.concatenate([s[p * r:(p + 1) * r, 0:PAIR_W] for p in range(half)]
                           + [s[p * r:(p + 1) * r, PAIR_W:2 * PAIR_W] for p in range(half)], axis=1)


def _sigmoid(x):
    return 1.0 / (1.0 + jnp.exp(-x))


def _sigmoid_tanh(x):
    return 0.5 + 0.5 * jnp.tanh(0.5 * x)


_TOK_F32 = ("at", "rt", "v", "bonus", "g")
_TOK_BF16 = ("at", "bt", "kt", "rt", "v", "bp", "kp")


_RWKV_N_IN = 14
_RWKV_N_OUT = 2
_RWKV_N_SCRATCH = 6


def _rwkv_kernel(*refs, rows, cps, nck, cast_periods):
    n_cast = len(cast_periods)
    (zf_ref, zn_ref, sh_ref, s0_ref, mu_ref, w0_ref, w2cat_ref, a0_ref, gw2_ref, kk_ref, ka_ref, rk_ref, lnw_ref,
     lnb_ref) = refs[:_RWKV_N_IN]
    cast_in = refs[_RWKV_N_IN:_RWKV_N_IN + n_cast]
    o_ref, sfin_ref = refs[_RWKV_N_IN + n_cast:_RWKV_N_IN + n_cast + _RWKV_N_OUT]
    cast_out = refs[_RWKV_N_IN + n_cast + _RWKV_N_OUT:_RWKV_N_IN + 2 * n_cast + _RWKV_N_OUT]
    s_scr, carry_scr, y_scr, f32_scr, bf_scr, pc_scr = refs[-_RWKV_N_SCRATCH:]
    ct = CHUNK_T
    pw = PAIR_W
    tt = cps * ct
    assert rows == ct or cps == 1
    c = pl.program_id(1)

    seg = 2 * pw
    ones_seg = _bf(lax.broadcasted_iota(jnp.int32, (seg, seg), 0) // HEAD_DIM
                   == lax.broadcasted_iota(jnp.int32, (seg, seg), 1) // HEAD_DIM)
    row = lax.broadcasted_iota(jnp.int32, (ct, 1), 0)
    row_all = lax.broadcasted_iota(jnp.int32, (tt, 1), 0)
    tri = _bf(lax.broadcasted_iota(jnp.int32, (ct, ct), 1) <= lax.broadcasted_iota(jnp.int32, (ct, ct), 0))
    t_lane = lax.broadcasted_iota(jnp.int32, (ct, 2 * ct), 1) % ct
    strict = t_lane < row
    incl = t_lane <= row
    lo_t = _lane_lo(ct)
    lo_p = lax.broadcasted_iota(jnp.int32, (ct, 2 * ct), 1) < ct
    lo_t2 = lax.broadcasted_iota(jnp.int32, (ct, 2 * pw), 1) % pw < HEAD_DIM
    diag = (lax.broadcasted_iota(jnp.int32, (HEAD_DIM, pw), 1) % HEAD_DIM
            == lax.broadcasted_iota(jnp.int32, (HEAD_DIM, pw), 0))
    lo_s = _lane_lo(HEAD_DIM)
    n_dbl = int(math.log2(ct))
    assert 1 << n_dbl == ct and 2 * ct == pw
    pairs = range(N_PAIRS)
    cat0 = lambda xs: jnp.concatenate(xs, axis=0)
    cat1 = lambda xs: jnp.concatenate(xs, axis=1)

    def tn(xv, yv):
        zz = _mm(_bf(xv.T), yv)
        return jnp.where(lo_s, zz[:HEAD_DIM], zz[HEAD_DIM:])

    def chunk_pre(zc):
        r = zc[:, 0:RWKV_WIDTH]
        k = zc[:, RWKV_WIDTH:2 * RWKV_WIDTH]
        v = zc[:, 2 * RWKV_WIDTH:3 * RWKV_WIDTH]
        zwa = zc[:, 3 * RWKV_WIDTH:3 * RWKV_WIDTH + DECAY_RANK + AAA_RANK]
        zg = zc[:, 3 * RWKV_WIDTH + DECAY_RANK + AAA_RANK:SHIFT_W]
        lora_in = cat0([jnp.where(lo_t, jnp.tanh(zwa), 0.0), jnp.where(lo_t, 0.0, zwa)])
        lora = _mm(_bf(lora_in), w2cat_ref[...])
        yield
        lw = -math.exp(-0.5) * _sigmoid_tanh(w0_ref[...] + lora[:ct])
        if rows < ct:
            live = row < rows
            lw = jnp.where(live, lw, 0.0)
        l_hi, l_lo = _split2(lw)
        cl = _mm(tri, l_hi) + _mm(tri, l_lo)
        yield
        a = _sigmoid_tanh(a0_ref[...] + lora[ct:])
        g = _mm(_bf(_sigmoid_tanh(zg)), gw2_ref[...])
        yield
        kkr = k * kk_ref[...]
        kn = kkr * lax.rsqrt(jnp.maximum(_seg_sum(kkr * kkr, ones_seg, passes=1), 1e-24))
        yield
        k2 = k * (1.0 + (a - 1.0) * ka_ref[...])
        bonus = _seg_sum(r * k2 * rk_ref[...], ones_seg, passes=1) * v
        yield
        e_in = jnp.exp(cl)
        rt = r * e_in
        pc = e_in[ct - 1:ct, :]
        yield
        at = -kn * jnp.exp(cl - lw)
        yield
        e_inv = jnp.exp(-cl)
        bt = kn * a * e_inv
        kt = k2 * e_inv
        if rows < ct:
            bt = jnp.where(live, bt, 0.0)
            kt = jnp.where(live, kt, 0.0)
        yield
        return dict(at=at, bt=bt, kt=kt, rt=rt, v=v, bp=bt * pc, kp=kt * pc, pc=pc, bonus=bonus, g=g)

    def pre_block(z_ref, carry, wr):
        z = z_ref[...]
        if rows < ct:
            z = jnp.concatenate([z, jnp.zeros((ct - rows, SHIFT_W), F32)], axis=0)
        zprev = jnp.where(row_all == 0, carry, pltpu.roll(z, 1, axis=0))
        last = (cps - 1) * ct + rows - 1
        carry_scr[0:1, :] = z[last:last + 1, :]
        zs = z + (zprev - z) * mu_ref[...]
        yield
        for gi in range(cps):
            t = yield from chunk_pre(zs[gi * ct:(gi + 1) * ct])
            rs = slice(gi * ct, (gi + 1) * ct)
            for i, name in enumerate(_TOK_F32):
                f32_scr[wr, i, rs, :] = t[name]
            yield
            for i, name in enumerate(_TOK_BF16):
                bf_scr[wr, i, rs, :] = _bf(t[name])
            pc_scr[wr, gi:gi + 1, :] = t["pc"]
            yield

    def chunk_stages(rd, gi):
        rs = slice(gi * ct, (gi + 1) * ct)
        f32 = lambda name, p: f32_scr[rd, _TOK_F32.index(name), rs, p * pw:(p + 1) * pw]
        b16 = lambda name, p: bf_scr[rd, _TOK_BF16.index(name), rs, p * pw:(p + 1) * pw]
        v_bd = [_bd(b16("v", p), lo_t) for p in pairs]
        amat = [_mm(cat0([b16("at", p), b16("rt", p)]), cat0([_bd(b16("bt", p), lo_t), _bd(b16("kt", p), lo_t)]),
                    _NT) for p in pairs]
        yield
        a_ab = [jnp.where(strict, amat[p][:ct, :pw], 0.0) for p in pairs]
        a_ak = [_bf(jnp.where(strict, amat[p][:ct, pw:], 0.0)) for p in pairs]
        a_rb = [_bf(jnp.where(incl, amat[p][ct:, :pw], 0.0)) for p in pairs]
        a_rk = [_bf(jnp.where(incl, amat[p][ct:, pw:], 0.0)) for p in pairs]
        x = [cat1([f32("at", p), _mm(a_ak[p], v_bd[p])]) for p in pairs]
        npow = [_bf(a_ab[p]) for p in pairs]
        yield
        for it in range(n_dbl):
            if it + 1 < n_dbl:
                res = [_mm(npow[p], cat1([_bd(_bf(x[p]), lo_t2), _bd(npow[p], lo_p)])) for p in pairs]
                npow = [_bf(res[p][:, 2 * pw:]) for p in pairs]
            else:
                res = [_mm(npow[p], _bd(_bf(x[p]), lo_t2)) for p in pairs]
            x = [x[p] + res[p][:, :2 * pw] for p in pairs]
            yield
        ry = [_mm(a_rb[p], _bd(_bf(x[p]), lo_t2)) for p in pairs]
        rp = [_bf(f32("rt", p) + ry[p][:, :pw]) for p in pairs]
        y0 = [ry[p][:, pw:] + _mm(a_rk[p], v_bd[p]) for p in pairs]
        yield
        m_mat = [_bf(jnp.where(diag, pc_scr[rd, gi:gi + 1, p * pw:(p + 1) * pw], 0.0)
                     + tn(x[p][:, :pw], b16("bp", p))) for p in pairs]
        yield
        n_mat = [tn(cat0([x[p][:, pw:], f32("v", p)]), cat0([b16("bp", p), b16("kp", p)])) for p in pairs]
        return rp, y0, m_mat, n_mat

    def lockstep(*gens):
        done = [None] * len(gens)
        live = list(range(len(gens)))
        while live:
            for i in list(live):
                try:
                    next(gens[i])
                except StopIteration as stop:
                    done[i] = stop.value
                    live.remove(i)
            yield
        return done

    def main_block(rd):
        terms = yield from lockstep(*[chunk_stages(rd, gi) for gi in range(cps)])
        state = [s_scr[:, p * pw:(p + 1) * pw] for p in pairs]
        for gi, (rp, y0, m_mat, n_mat) in enumerate(terms):
            for p in pairs:
                s_old = _bf(state[p])
                y_scr[gi * ct:(gi + 1) * ct, p * pw:(p + 1) * pw] = _mm(rp[p], _bd(s_old, lo_s), _NT) + y0[p]
                state[p] = _mm(s_old, _bd(m_mat[p], lo_s)) + n_mat[p]
            yield
        for p in pairs:
            s_scr[:, p * pw:(p + 1) * pw] = state[p]
        y = y_scr[...]
        mean = _seg_sum(y, ones_seg, passes=2) * (1.0 / HEAD_DIM)
        dv = y - mean
        yield
        var = _seg_sum(dv * dv, ones_seg, passes=1) * (1.0 / HEAD_DIM)
        yn = dv * lax.rsqrt(var + GN_EPS) * lnw_ref[...] + lnb_ref[...]
        out = (yn + f32_scr[rd, _TOK_F32.index("bonus")]) * f32_scr[rd, _TOK_F32.index("g")]
        o_ref[...] = out[:(cps - 1) * ct + rows].astype(BF16)

    def cast_block(parity):
        for src, dst, period in zip(cast_in, cast_out, cast_periods):
            if parity % period == 0:
                dst[...] = _bf(src[...])
            yield

    def run(*gens):
        for _ in lockstep(*gens):
            pass

    @pl.when(c == 0)
    def _():
        for p in pairs:
            s_scr[:, p * pw:(p + 1) * pw] = jnp.concatenate([s0_ref[2 * p], s0_ref[2 * p + 1]], axis=1)
        run(pre_block(zf_ref, sh_ref[...], 0))

    if nck == 1:
        assert n_cast == 0
        run(main_block(0))
    else:
        for parity in (0, 1):
            @pl.when(c % 2 == parity)
            def _(parity=parity):
                run(main_block(parity), pre_block(zn_ref, carry_scr[0:1, :], 1 - parity), cast_block(parity))

    @pl.when(c == nck - 1)
    def _():
        for h in range(N_HEADS):
            sfin_ref[h] = s_scr[:, h * HEAD_DIM:(h + 1) * HEAD_DIM]


def _rwkv(z_rwkv, shift0, s0, lp, *, nb, seq, name, cast_weights=()):
    rows = min(seq, CHUNK_T)
    cps = min(seq // rows, RWKV_CHUNKS_PER_STEP)
    blk_rows = cps * rows
    assert seq % blk_rows == 0 and z_rwkv.shape == (nb * seq, SHIFT_W)
    nck = seq // blk_rows
    tt = cps * CHUNK_T
    row1 = lambda n: pl.BlockSpec((1, n), lambda b, c: (0, 0))
    full = lambda r, n: pl.BlockSpec((r, n), lambda b, c: (0, 0))
    st_blk = pl.BlockSpec((None, N_HEADS, HEAD_DIM, HEAD_DIM), lambda b, c: (b, 0, 0, 0))
    lora_rows = DECAY_RANK + AAA_RANK
    blocks = (2 * _nbytes((blk_rows, SHIFT_W), F32) + 2 * _nbytes((HEAD_DIM, RWKV_WIDTH), F32)
              + _nbytes((lora_rows + GATE_RANK, RWKV_WIDTH), BF16) + _nbytes((16, SHIFT_W), F32)
              + _nbytes((blk_rows, RWKV_WIDTH), BF16))
    stage_shapes = [((2, len(_TOK_F32), tt, RWKV_WIDTH), F32), ((2, len(_TOK_BF16), tt, RWKV_WIDTH), BF16),
                    ((2, 8, RWKV_WIDTH), F32)]
    scratch = (_nbytes((HEAD_DIM, RWKV_WIDTH), F32) + _nbytes((8, SHIFT_W), F32)
               + _nbytes((tt, RWKV_WIDTH), F32) + sum(_nbytes(s, d) for s, d in stage_shapes))
    assert cps <= 8
    bf16_rows = 16
    cast_periods, cast_specs, cast_shapes = [], [], []
    for w in cast_weights:
        assert nb == 1 and nck % 2 == 0
        period = next(p for p in (1, 2) if w.shape[0] % (nck // p) == 0 and (w.shape[0] // (nck // p)) % bf16_rows == 0)
        slab = (w.shape[0] // (nck // period), w.shape[1])
        cast_periods.append(period)
        cast_specs.append(pl.BlockSpec(slab, lambda b, c, period=period: (c // period, 0)))
        cast_shapes.append(jax.ShapeDtypeStruct(w.shape, BF16))
        blocks += _nbytes(slab, F32) + _nbytes(slab, BF16)
    outs = pl.pallas_call(
        functools.partial(_rwkv_kernel, rows=rows, cps=cps, nck=nck, cast_periods=tuple(cast_periods)),
        out_shape=(jax.ShapeDtypeStruct((nb * seq, RWKV_WIDTH), BF16),
                   jax.ShapeDtypeStruct((nb, N_HEADS, HEAD_DIM, HEAD_DIM), F32), *cast_shapes),
        grid=(nb, nck),
        in_specs=[pl.BlockSpec((blk_rows, SHIFT_W), lambda b, c: (b * nck, 0)),
                  pl.BlockSpec((blk_rows, SHIFT_W), lambda b, c: (b * nck + jnp.minimum(c + 1, nck - 1), 0)),
                  pl.BlockSpec((None, 1, SHIFT_W), lambda b, c: (b, 0, 0)),
                  st_blk,
                  row1(SHIFT_W), row1(RWKV_WIDTH), full(lora_rows, RWKV_WIDTH), row1(RWKV_WIDTH),
                  full(GATE_RANK, RWKV_WIDTH), row1(RWKV_WIDTH),
                  row1(RWKV_WIDTH), row1(RWKV_WIDTH), row1(RWKV_WIDTH), row1(RWKV_WIDTH), *cast_specs],
        out_specs=(pl.BlockSpec((blk_rows, RWKV_WIDTH), lambda b, c: (b * nck + c, 0)), st_blk, *cast_specs),
        scratch_shapes=[pltpu.VMEM((HEAD_DIM, RWKV_WIDTH), F32), pltpu.VMEM((8, SHIFT_W), F32),
                        pltpu.VMEM((tt, RWKV_WIDTH), F32)] + [pltpu.VMEM(s, d) for s, d in stage_shapes],
        compiler_params=_params(2, _vmem_limit(blocks, scratch, 64 * _nbytes((tt, RWKV_WIDTH), F32))),
        name=name,
    )(z_rwkv, z_rwkv, shift0, s0, lp["mu"], lp["w0"], lp["w2cat"], lp["a0"], lp["g_w2"],
      lp["k_k"], lp["k_a"], lp["r_k"], lp["lnx_w"], lp["lnx_b"], *cast_weights)
    return outs[0], outs[1], list(outs[2:])


def _oproj_kernel(a_ref, r_ref, wa_ref, wr_ref, x_ref, g_ref, h_ref, n_ref):
    h = (x_ref[...] + jnp.dot(a_ref[...], wa_ref[...], preferred_element_type=F32)
         + jnp.dot(r_ref[...], wr_ref[...], preferred_element_type=F32))
    h_ref[...] = h
    n_ref[...] = _rms(h, g_ref[...]).astype(BF16)


def _oproj(attn, rwkv, w_o, x, g, *, tm, name):
    t = x.shape[0]
    assert t % tm == 0
    half = (ATTN_WIDTH, D_MODEL)
    blocks = (2 * _nbytes((tm, ATTN_WIDTH), BF16) + 2 * _nbytes(half, BF16) + 2 * _nbytes((tm, D_MODEL), F32)
              + _nbytes((tm, D_MODEL), BF16))
    return pl.pallas_call(
        _oproj_kernel,
        out_shape=(jax.ShapeDtypeStruct((t, D_MODEL), F32), jax.ShapeDtypeStruct((t, D_MODEL), BF16)),
        grid=(t // tm,),
        in_specs=[pl.BlockSpec((tm, ATTN_WIDTH), lambda i: (i, 0)),
                  pl.BlockSpec((tm, RWKV_WIDTH), lambda i: (i, 0)),
                  pl.BlockSpec(half, lambda i: (0, 0)),
                  pl.BlockSpec(half, lambda i: (1, 0)),
                  pl.BlockSpec((tm, D_MODEL), lambda i: (i, 0)),
                  pl.BlockSpec((1, D_MODEL), lambda i: (0, 0))],
        out_specs=(pl.BlockSpec((tm, D_MODEL), lambda i: (i, 0)), pl.BlockSpec((tm, D_MODEL), lambda i: (i, 0))),
        compiler_params=_params(1, _vmem_limit(blocks, 0, 2 * _nbytes((tm, D_MODEL), F32))),
        name=name,
    )(attn, rwkv, w_o, w_o, x, g)


def _ffn_kernel(n_ref, h_ref, wg_ref, wu_ref, wd_ref, g_ref, o_ref):
    j = pl.program_id(1)

    @pl.when(j == 0)
    def _():
        o_ref[...] = h_ref[...]

    n = n_ref[...]
    gate = jnp.dot(n, wg_ref[...], preferred_element_type=F32)
    up = jnp.dot(n, wu_ref[...], preferred_element_type=F32)
    act = (gate * _sigmoid(gate) * up).astype(BF16)
    o_ref[...] += jnp.dot(act, wd_ref[...], preferred_element_type=F32)

    @pl.when(j == pl.num_programs(1) - 1)
    def _():
        o_ref[...] = _rms(o_ref[...], g_ref[...])


def _ffn(n2, h, wg, wu, wd, g, *, tm, tf, name):
    t = h.shape[0]
    assert t % tm == 0 and D_FF % tf == 0
    blocks = (_nbytes((tm, D_MODEL), BF16) + 2 * _nbytes((tm, D_MODEL), F32) + 3 * _nbytes((D_MODEL, tf), BF16))
    temps = 3 * _nbytes((tm, tf), F32) + _nbytes((tm, tf), BF16)
    return pl.pallas_call(
        _ffn_kernel,
        out_shape=jax.ShapeDtypeStruct((t, D_MODEL), F32),
        grid=(t // tm, D_FF // tf),
        in_specs=[pl.BlockSpec((tm, D_MODEL), lambda i, j: (i, 0)),
                  pl.BlockSpec((tm, D_MODEL), lambda i, j: (i, 0)),
                  pl.BlockSpec((D_MODEL, tf), lambda i, j: (0, j)),
                  pl.BlockSpec((D_MODEL, tf), lambda i, j: (0, j)),
                  pl.BlockSpec((tf, D_MODEL), lambda i, j: (j, 0)),
                  pl.BlockSpec((1, D_MODEL), lambda i, j: (0, 0))],
        out_specs=pl.BlockSpec((tm, D_MODEL), lambda i, j: (i, 0)),
        compiler_params=_params(2, _vmem_limit(blocks, 0, temps)),
        name=name,
    )(n2, h, wg, wu, wd, g)


_LATE_WEIGHTS = ("w_o", "w_gate", "w_up", "w_down")


def _project(x, lp, wts, *, tag):
    tm = min(x.shape[0], 1024)
    source = lambda key, col0: (wts[key], 0) if key in wts else (wts["w_in"], col0)
    w, col0 = source("w_attn", 0)
    z_attn, nx, w_bf = _inproj(x, lp["ln1"], w, col0=col0, n=3 * ATTN_WIDTH, tm=tm, tn=1024,
                               name=f"inproj_attn_{tag}")
    if w_bf is not None:
        wts["w_attn"] = w_bf
    w, col0 = source("w_rwkv", 3 * ATTN_WIDTH)
    z_rwkv, _, w_bf = _inproj(nx, None, w, col0=col0, n=SHIFT_W, tm=tm, tn=SHIFT_W // 2, name=f"inproj_rwkv_{tag}")
    if w_bf is not None:
        wts["w_rwkv"] = w_bf
    return z_attn, z_rwkv


def _mix(z_attn, z_rwkv, lp, wts, attn_fn, shift0, s0, *, nb, seq, tag):
    attn = attn_fn(z_attn)
    convert = seq > RWKV_CHUNKS_PER_STEP * CHUNK_T
    pending = [k for k in _LATE_WEIGHTS if wts[k].dtype != BF16] if convert else []
    rwkv, s_fin, converted = _rwkv(z_rwkv, shift0, s0, lp, nb=nb, seq=seq, name=f"rwkv_{tag}",
                                   cast_weights=[wts[k] for k in pending])
    wts.update(zip(pending, converted))
    return attn, rwkv, s_fin


def _mlp(x, attn, rwkv, lp, wts, *, tag):
    t = x.shape[0]
    h, n2 = _oproj(attn, rwkv, wts["w_o"], x, lp["ln2"], tm=min(t, 512), name=f"oproj_{tag}")
    return _ffn(n2, h, wts["w_gate"], wts["w_up"], wts["w_down"], lp["final_norm"], tm=min(t, 1024), tf=512,
                name=f"ffn_{tag}")


def kernel(x_prompt, x_sample, cache_attn_k, cache_attn_v, state_rwkv_wkv, state_rwkv_shift, ln1, w_in, rel_table,
           mu, w0, w_w2, a0, a_w2, g_w2, k_k, k_a, r_k, lnx_w, lnx_b, w_o, ln2, w_gate, w_up, w_down, final_norm):
    depth = w_in.shape[0]
    assert depth == 1, "final RMSNorm is fused into the (single) layer's FFN call"
    b, t, _ = x_prompt.shape
    bd, s, _ = x_sample.shape
    past = cache_attn_k.shape[2]
    n_keep = min(BAND_PAST, t)
    l = 0
    row = lambda p: p[l].reshape(1, -1)
    lp = dict(ln1=row(ln1), mu=row(mu), w0=row(w0), a0=row(a0),
              w2cat=jnp.concatenate([w_w2[l], a_w2[l]], axis=0).astype(BF16), g_w2=g_w2[l].astype(BF16),
              k_k=row(k_k), k_a=row(k_a), r_k=row(r_k), lnx_w=row(lnx_w), lnx_b=row(lnx_b), ln2=row(ln2),
              final_norm=final_norm.reshape(1, -1))
    wts = dict(w_in=w_in[l], w_o=w_o[l], w_gate=w_gate[l], w_up=w_up[l], w_down=w_down[l])
    table = rel_table[l]

    bias_p = _bias_tiles(table, mq=CHUNK, off=BAND_PAST + CHUNK, col_lo=CHUNK, col_hi=KEY_WIN, name="bias_prompt")
    bias_s = _bias_tiles(table, mq=s, off=past, col_lo=0, col_hi=past + s, name="bias_sample")

    assert b == 1
    xp = x_prompt.reshape(b * t, D_MODEL)
    xs = x_sample.reshape(bd * s, D_MODEL)
    kc = cache_attn_k[l]
    vc = cache_attn_v[l]
    zas, zrs = _project(xs, lp, wts, tag="s")
    zap, zrp = _project(xp, lp, wts, tag="p")
    attn_p, rwkv_p, wkv_p = _mix(zap, zrp, lp, wts, lambda z: _attn_prompt(z, bias_p),
                                 jnp.zeros((b, 1, SHIFT_W), F32), jnp.zeros((b, N_HEADS, HEAD_DIM, HEAD_DIM), F32),
                                 nb=b, seq=t, tag="p")
    attn_s, rwkv_s, wkv_s = _mix(zas, zrs, lp, wts, lambda z: _attn_sample(z, kc, vc, bias_s, nb=bd, s=s),
                                 state_rwkv_shift[l], state_rwkv_wkv[l], nb=bd, seq=s, tag="s")
    yp = _mlp(xp, attn_p, rwkv_p, lp, wts, tag="p")
    ys = _mlp(xs, attn_s, rwkv_s, lp, wts, tag="s")

    heads = lambda z, n, tt: z.reshape(n, tt, N_HEADS, HEAD_DIM)
    zap = zap.reshape(b, t, 3 * ATTN_WIDTH)
    zas = zas.reshape(bd, s, 3 * ATTN_WIDTH)
    k_p = heads(zap[:, t - n_keep:, ATTN_WIDTH:2 * ATTN_WIDTH], b, n_keep)
    v_p = heads(zap[:, t - n_keep:, 2 * ATTN_WIDTH:], b, n_keep)
    k_s = heads(zas[:, :, ATTN_WIDTH:2 * ATTN_WIDTH], bd, s)
    v_s = heads(zas[:, :, 2 * ATTN_WIDTH:], bd, s)
    sh_p = zrp.reshape(b, t, SHIFT_W)[:, t - 1:]
    sh_s = zrs.reshape(bd, s, SHIFT_W)[:, s - 1:]
    return (yp.reshape(b, t, D_MODEL), ys.reshape(bd, s, D_MODEL),
            k_p[None], v_p[None], wkv_p[None], sh_p[None],
            k_s[None], v_s[None], wkv_s[None], sh_s[None])
```
